```python
import math
import jax, jax.numpy as jnp
from jax import lax
import numpy as np

D_MODEL = 2048
BATCH = 16
SEQ = 2048
DEPTH = 2

D_FF = 256 * ((8 * D_MODEL // 3 + 255) // 256)
FFN_RES = 0.5
N_MOD = 9
EPS = 1e-6

GDN_HEADS = 8
GDN_HEAD_DIM = 128
GDN_WIDTH = GDN_HEADS * GDN_HEAD_DIM
GDN_CONV = 4
GDN_CHUNK = 64

S5_GROUP = 16
S5_WIDTH = 768
S5_GROUPS = S5_WIDTH // S5_GROUP
S5_STATE = 64
S5_MAX_RE = -1e-4

DIL_PAIRS = ((128, 1), (512, 4), (2048, 16))
DIL_HEADS_PER_GROUP = 4
DIL_HEAD_DIM = 64
DIL_SUBHEADS = len(DIL_PAIRS) * DIL_HEADS_PER_GROUP
DIL_WIDTH = DIL_SUBHEADS * DIL_HEAD_DIM
DIL_OUT = DIL_HEADS_PER_GROUP * DIL_HEAD_DIM
ALIBI_MAX = 8.0

N_BRANCH = 3
IN_SPLITS = (3 * GDN_WIDTH, GDN_WIDTH, GDN_HEADS, GDN_HEADS, S5_WIDTH, 3 * DIL_WIDTH, N_BRANCH * D_MODEL)
IN_COLS = sum(IN_SPLITS)

kernel_name = 'hybrid_gdn_s5_dilated_macaron_block'


def rms_norm(x, g):
    xf = x.astype(jnp.float32)
    y = xf * lax.rsqrt(jnp.mean(xf * xf, axis=-1, keepdims=True) + EPS)
    return (y * g.astype(jnp.float32)).astype(x.dtype)


def l2_normalize(x):
    return x * lax.rsqrt(jnp.sum(x * x, axis=-1, keepdims=True) + EPS)


def modulate(x, shift, scale):
    return x * (1.0 + scale[:, None, :]) + shift[:, None, :]


def swiglu(x, w1, w3, w2):
    return (jax.nn.silu(x @ w1) * (x @ w3)) @ w2


def causal_depthwise_conv(x, w):
    K, C = w.shape
    return lax.conv_general_dilated(x, w[:, None, :].astype(x.dtype), window_strides=(1,),
                                    padding=[(K - 1, 0)], dimension_numbers=('NWC', 'WIO', 'NWC'),
                                    feature_group_count=C)


def gated_delta_rule_chunked(q, k, v, g, beta):
    Bsz, S_, H, Dk = q.shape
    Dv = v.shape[-1]
    C = GDN_CHUNK
    N = S_ // C

    def chunks(t):
        t = t.reshape((Bsz, N, C) + t.shape[2:])
        return jnp.moveaxis(t, 3, 1)

    q, k, v, g, beta = chunks(q), chunks(k), chunks(v), chunks(g), chunks(beta)
    g = jnp.cumsum(g, axis=-1)
    kb = k * beta[..., None]
    vb = v * beta[..., None]
    tril = jnp.tril(jnp.ones((C, C), bool))
    tril_strict = jnp.tril(jnp.ones((C, C), bool), -1)
    decay = jnp.exp(jnp.where(tril, g[..., :, None] - g[..., None, :], -jnp.inf))
    lower = jnp.where(tril_strict, jnp.einsum('bhnid,bhnjd->bhnij', kb, k) * decay, 0.0)
    system = jnp.eye(C, dtype=q.dtype) + lower
    u = lax.linalg.triangular_solve(system, vb, left_side=True, lower=True, unit_diagonal=True)
    w = lax.linalg.triangular_solve(system, kb * jnp.exp(g)[..., None], left_side=True,
                                    lower=True, unit_diagonal=True)

    def step(h, inp):
        qc, kc, uc, wc, gc, dc = inp
        attn = jnp.einsum('bhid,bhjd->bhij', qc, kc) * dc
        v_new = uc - jnp.einsum('bhcd,bhde->bhce', wc, h)
        o = (jnp.einsum('bhcd,bhde->bhce', qc * jnp.exp(gc)[..., None], h)
             + jnp.einsum('bhij,bhje->bhie', attn, v_new))
        g_last = gc[..., -1]
        h = (h * jnp.exp(g_last)[..., None, None]
             + jnp.einsum('bhcd,bhce->bhde', kc * jnp.exp(g_last[..., None] - gc)[..., None], v_new))
        return h, o

    xs = tuple(jnp.moveaxis(t, 2, 0) for t in (q, k, u, w, g, decay))
    h0 = jnp.zeros((Bsz, H, Dk, Dv), q.dtype)
    _, o = lax.scan(step, h0, xs)
    return o.transpose(1, 0, 3, 2, 4).reshape(Bsz, S_, H, Dv)


def gdn_branch(qkv, z, beta_logit, alpha_logit, conv_w, a_log, dt_bias, out_norm):
    Bsz, S_, _ = qkv.shape
    f32 = jnp.float32
    qkv = jax.nn.silu(causal_depthwise_conv(qkv, conv_w)).astype(f32)
    q, k, v = [t.reshape(Bsz, S_, GDN_HEADS, GDN_HEAD_DIM) for t in jnp.split(qkv, 3, axis=-1)]
    q = l2_normalize(q) * GDN_HEAD_DIM ** -0.5
    k = l2_normalize(k)
    beta = jax.nn.sigmoid(beta_logit.astype(f32))
    g = -jnp.exp(a_log.astype(f32)) * jax.nn.softplus(alpha_logit.astype(f32) + dt_bias.astype(f32))
    o = gated_delta_rule_chunked(q, k, v, g, beta)
    o = rms_norm(o, out_norm) * jax.nn.silu(z.astype(f32).reshape(Bsz, S_, GDN_HEADS, GDN_HEAD_DIM))
    return o.reshape(Bsz, S_, GDN_WIDTH).astype(z.dtype)


def s5_branch(u, a_re, a_im, b_re, b_im, c_re, c_im, d_skip, log_step, glu_w, glu_b):
    Bsz, S_, _ = u.shape
    f32 = jnp.float32
    uf = u.astype(f32).reshape(Bsz, S_, S5_GROUPS, S5_GROUP)
    lam = lax.complex(jnp.minimum(a_re.astype(f32), S5_MAX_RE), a_im.astype(f32))
    step = jnp.exp(log_step.astype(f32))[:, None]
    lam_bar = jnp.exp(lam * step)
    b = lax.complex(b_re.astype(f32), b_im.astype(f32))
    b_bar = ((lam_bar - 1.0) / lam)[..., None] * b
    bu = jnp.einsum('gpi,bsgi->bsgp', b_bar, uf)
    a = jnp.broadcast_to(lam_bar, (1, S_) + lam_bar.shape)

    def combine(left, right):
        a_l, b_l = left
        a_r, b_r = right
        return a_r * a_l, a_r * b_l + b_r

    _, states = lax.associative_scan(combine, (a, bu), axis=1)
    cmat = lax.complex(c_re.astype(f32), c_im.astype(f32))
    y = jnp.real(jnp.einsum('gip,bsgp->bsgi', cmat, states)) + d_skip.astype(f32).reshape(S5_GROUPS, S5_GROUP) * uf
    y = jax.nn.gelu(y.reshape(Bsz, S_, S5_WIDTH))
    val, gate = jnp.split(y @ glu_w.astype(f32) + glu_b.astype(f32), 2, axis=-1)
    return (val * jax.nn.sigmoid(gate)).astype(u.dtype)


def dilated_window_attention(q, k, v, slopes, window, dilation):
    Bsz, S_, H, E = q.shape
    span = window // dilation
    sd = S_ // dilation
    blk = min(span, sd)
    nb = -(-sd // blk)
    sp = nb * blk

    def residues(t):
        t = t.reshape(Bsz, sd, dilation, H, E).transpose(0, 2, 1, 3, 4)
        return jnp.pad(t, ((0, 0), (0, 0), (0, sp - sd), (0, 0), (0, 0)))

    def key_blocks(t):
        t = jnp.pad(t, ((0, 0), (0, 0), (blk, 0), (0, 0), (0, 0))).reshape(Bsz, dilation, nb + 1, blk, H, E)
        return jnp.concatenate([t[:, :, :-1], t[:, :, 1:]], axis=3)

    qb = residues(q).reshape(Bsz, dilation, nb, blk, H, E)
    kb = key_blocks(residues(k))
    vb = key_blocks(residues(v))
    s = jnp.einsum('brnqhe,brnkhe->brnhqk', qb, kb)
    qi = jnp.arange(blk)[:, None]
    kj = jnp.arange(2 * blk)[None, :]
    steps = qi - kj + blk
    key_pos = jnp.arange(nb)[:, None, None] * blk + kj[None] - blk
    valid = (steps >= 0) & (steps <= span) & (key_pos >= 0)
    bias = -slopes[:, None, None] * (steps * dilation).astype(jnp.float32)[None]
    s = jnp.where(valid[None, None, :, None], s + bias[None, None, None], -jnp.inf)
    m = jnp.max(s, axis=-1, keepdims=True)
    p = jnp.exp(s - m)
    l = jnp.sum(p, axis=-1, keepdims=True)
    o = jnp.einsum('brnhqk,brnkhe->brnqhe', p / l, vb)
    lse = (m + jnp.log(l))[..., 0].transpose(0, 1, 2, 4, 3)

    def back(t):
        t = t.reshape((Bsz, dilation, sp) + t.shape[4:])[:, :, :sd]
        t = jnp.moveaxis(t, 1, 2)
        return t.reshape((Bsz, S_) + t.shape[3:])

    return back(o), back(lse)


def dilated_branch(qkv, q_norm, k_norm):
    Bsz, S_, _ = qkv.shape
    f32 = jnp.float32
    shape = (Bsz, S_, DIL_SUBHEADS, DIL_HEAD_DIM)
    q, k, v = [t.astype(f32).reshape(shape) for t in jnp.split(qkv, 3, axis=-1)]
    q = rms_norm(q, q_norm) * DIL_HEAD_DIM ** -0.5
    k = rms_norm(k, k_norm)
    slopes = jnp.power(2.0, -ALIBI_MAX * jnp.arange(1, DIL_SUBHEADS + 1, dtype=f32) / DIL_SUBHEADS)
    outs, lses = [], []
    for gi, (window, dilation) in enumerate(DIL_PAIRS):
        hs = slice(gi * DIL_HEADS_PER_GROUP, (gi + 1) * DIL_HEADS_PER_GROUP)
        o, lse = dilated_window_attention(q[:, :, hs], k[:, :, hs], v[:, :, hs], slopes[hs], window, dilation)
        outs.append(o)
        lses.append(lse)
    weights = jax.nn.softmax(jnp.stack(lses, 0), axis=0)
    o = jnp.sum(weights[..., None] * jnp.stack(outs, 0), axis=0)
    return o.reshape(Bsz, S_, DIL_OUT).astype(qkv.dtype)


def hybrid_mixer(u, w_in, gdn_conv, gdn_a_log, gdn_dt_bias, gdn_out_norm,
                 s5_a_re, s5_a_im, s5_b_re, s5_b_im, s5_c_re, s5_c_im, s5_d, s5_log_step, s5_glu_w, s5_glu_b,
                 dil_q_norm, dil_k_norm, w_branch_a, w_branch_b, w_branch_c, w_out):
    Bsz, S_, D = u.shape
    proj = u @ w_in
    offsets = np.cumsum(IN_SPLITS)[:-1].tolist()
    a_qkv, a_z, a_beta, a_alpha, b_u, c_qkv, gate_logits = jnp.split(proj, offsets, axis=-1)
    y_a = gdn_branch(a_qkv, a_z, a_beta, a_alpha, gdn_conv, gdn_a_log, gdn_dt_bias, gdn_out_norm)
    y_b = s5_branch(b_u, s5_a_re, s5_a_im, s5_b_re, s5_b_im, s5_c_re, s5_c_im, s5_d, s5_log_step, s5_glu_w, s5_glu_b)
    y_c = dilated_branch(c_qkv, dil_q_norm, dil_k_norm)
    gates = jax.nn.sigmoid(gate_logits.astype(jnp.float32)).astype(u.dtype).reshape(Bsz, S_, N_BRANCH, D)
    merged = (gates[:, :, 0] * (y_a @ w_branch_a)
              + gates[:, :, 1] * (y_b @ w_branch_b)
              + gates[:, :, 2] * (y_c @ w_branch_c))
    return merged @ w_out


def _fwd_setup_inputs(seed: int = 0) -> dict:
    key = jax.random.key(seed)
    keys = list(jax.random.split(key, 48))
    f32 = jnp.float32
    L, D = DEPTH, D_MODEL
    G, P, I = S5_GROUPS, S5_STATE, S5_GROUP

    def nrm(shape, scale):
        return jax.random.normal(keys.pop(), shape, f32) * scale

    def gain(shape):
        return 1.0 + nrm(shape, 0.05)

    def unif(shape, lo, hi):
        return jax.random.uniform(keys.pop(), shape, f32, lo, hi)

    dt = jnp.exp(unif((L, GDN_HEADS), math.log(1e-3), math.log(1e-1)))
    return {
        'x': nrm((BATCH, SEQ, D), 1.0),
        'c': nrm((BATCH, D), 1.0),
        'ada_w': nrm((L, D, N_MOD * D), 0.5 * D ** -0.5),
        'ada_b': nrm((L, N_MOD * D), 0.01),
        'norm_ffn1': gain((L, D)),
        'ffn1_w1': nrm((L, D, D_FF), D ** -0.5),
        'ffn1_w3': nrm((L, D, D_FF), D ** -0.5),
        'ffn1_w2': nrm((L, D_FF, D), D_FF ** -0.5),
        'norm_mix': gain((L, D)),
        'w_in': nrm((L, D, IN_COLS), D ** -0.5),
        'gdn_conv': nrm((L, GDN_CONV, 3 * GDN_WIDTH), GDN_CONV ** -0.5),
        'gdn_a_log': jnp.log(unif((L, GDN_HEADS), 1.0, 16.0)),
        'gdn_dt_bias': dt + jnp.log(-jnp.expm1(-dt)),
        'gdn_out_norm': gain((L, GDN_HEAD_DIM)),
        's5_a_re': -0.5 + nrm((L, G, P), 0.01),
        's5_a_im': math.pi * jnp.arange(P, dtype=f32) + nrm((L, G, P), 0.01),
        's5_b_re': nrm((L, G, P, I), (2 * I) ** -0.5),
        's5_b_im': nrm((L, G, P, I), (2 * I) ** -0.5),
        's5_c_re': nrm((L, G, I, P), (2 * P) ** -0.5),
        's5_c_im': nrm((L, G, I, P), (2 * P) ** -0.5),
        's5_d': nrm((L, S5_WIDTH), 1.0),
        's5_log_step': unif((L, G), math.log(1e-3), math.log(1e-1)),
        's5_glu_w': nrm((L, S5_WIDTH, 2 * S5_WIDTH), S5_WIDTH ** -0.5),
        's5_glu_b': nrm((L, 2 * S5_WIDTH), 0.01),
        'dil_q_norm': gain((L, DIL_HEAD_DIM)),
        'dil_k_norm': gain((L, DIL_HEAD_DIM)),
        'w_branch_a': nrm((L, GDN_WIDTH, D), GDN_WIDTH ** -0.5),
        'w_branch_b': nrm((L, S5_WIDTH, D), S5_WIDTH ** -0.5),
        'w_branch_c': nrm((L, DIL_OUT, D), DIL_OUT ** -0.5),
        'w_out': nrm((L, D, D), D ** -0.5),
        'norm_ffn2': gain((L, D)),
        'ffn2_w1': nrm((L, D, D_FF), D ** -0.5),
        'ffn2_w3': nrm((L, D, D_FF), D ** -0.5),
        'ffn2_w2': nrm((L, D_FF, D), D_FF ** -0.5),
    }


def _fwd_reference(x, c, ada_w, ada_b, norm_ffn1, ffn1_w1, ffn1_w3, ffn1_w2, norm_mix, w_in,
              gdn_conv, gdn_a_log, gdn_dt_bias, gdn_out_norm,
              s5_a_re, s5_a_im, s5_b_re, s5_b_im, s5_c_re, s5_c_im, s5_d, s5_log_step, s5_glu_w, s5_glu_b,
              dil_q_norm, dil_k_norm, w_branch_a, w_branch_b, w_branch_c, w_out,
              norm_ffn2, ffn2_w1, ffn2_w3, ffn2_w2):
    for l in range(DEPTH):
        mod = jax.nn.silu(c) @ ada_w[l] + ada_b[l]
        sh1, sc1, g1, sh2, sc2, g2, sh3, sc3, g3 = jnp.split(mod, N_MOD, axis=-1)
        h = modulate(rms_norm(x, norm_ffn1[l]), sh1, sc1)
        x = x + FFN_RES * g1[:, None, :] * swiglu(h, ffn1_w1[l], ffn1_w3[l], ffn1_w2[l])
        h = modulate(rms_norm(x, norm_mix[l]), sh2, sc2)
        x = x + g2[:, None, :] * hybrid_mixer(
            h, w_in[l], gdn_conv[l], gdn_a_log[l], gdn_dt_bias[l], gdn_out_norm[l],
            s5_a_re[l], s5_a_im[l], s5_b_re[l], s5_b_im[l], s5_c_re[l], s5_c_im[l], s5_d[l], s5_log_step[l],
            s5_glu_w[l], s5_glu_b[l], dil_q_norm[l], dil_k_norm[l],
            w_branch_a[l], w_branch_b[l], w_branch_c[l], w_out[l])
        h = modulate(rms_norm(x, norm_ffn2[l]), sh3, sc3)
        x = x + FFN_RES * g3[:, None, :] * swiglu(h, ffn2_w1[l], ffn2_w3[l], ffn2_w2[l])
    return x


import jax as _jax
import jax.numpy as _jnp

TWIN_FORMAT = 'train_step'
FWD_PARAMS = ['x', 'c', 'ada_w', 'ada_b', 'norm_ffn1', 'ffn1_w1', 'ffn1_w3', 'ffn1_w2', 'norm_mix', 'w_in', 'gdn_conv', 'gdn_a_log', 'gdn_dt_bias', 'gdn_out_norm', 's5_a_re', 's5_a_im', 's5_b_re', 's5_b_im', 's5_c_re', 's5_c_im', 's5_d', 's5_log_step', 's5_glu_w', 's5_glu_b', 'dil_q_norm', 'dil_k_norm', 'w_branch_a', 'w_branch_b', 'w_branch_c', 'w_out', 'norm_ffn2', 'ffn2_w1', 'ffn2_w3', 'ffn2_w2']
TWIN_WEIGHTS = ['ada_w', 'ada_b', 'norm_ffn1', 'ffn1_w1', 'ffn1_w3', 'ffn1_w2', 'norm_mix', 'w_in', 'gdn_conv', 'gdn_a_log', 'gdn_dt_bias', 'gdn_out_norm', 's5_a_re', 's5_a_im', 's5_b_re', 's5_b_im', 's5_c_re', 's5_c_im', 's5_d', 's5_log_step', 's5_glu_w', 's5_glu_b', 'dil_q_norm', 'dil_k_norm', 'w_branch_a', 'w_branch_b', 'w_branch_c', 'w_out', 'norm_ffn2', 'ffn2_w1', 'ffn2_w3', 'ffn2_w2']
TWIN_DIFF_INPUT = 'x'
TWIN_INPUTS = ['x', 'c', 'ada_w', 'ada_b', 'norm_ffn1', 'ffn1_w1', 'ffn1_w3', 'ffn1_w2', 'norm_mix', 'w_in', 'gdn_conv', 'gdn_a_log', 'gdn_dt_bias', 'gdn_out_norm', 's5_a_re', 's5_a_im', 's5_b_re', 's5_b_im', 's5_c_re', 's5_c_im', 's5_d', 's5_log_step', 's5_glu_w', 's5_glu_b', 'dil_q_norm', 'dil_k_norm', 'w_branch_a', 'w_branch_b', 'w_branch_c', 'w_out', 'norm_ffn2', 'ffn2_w1', 'ffn2_w3', 'ffn2_w2', 'loss_target', 'm_ada_w', 'm_ada_b', 'm_norm_ffn1', 'm_ffn1_w1', 'm_ffn1_w3', 'm_ffn1_w2', 'm_norm_mix', 'm_w_in', 'm_gdn_conv', 'm_gdn_a_log', 'm_gdn_dt_bias', 'm_gdn_out_norm', 'm_s5_a_re', 'm_s5_a_im', 'm_s5_b_re', 'm_s5_b_im', 'm_s5_c_re', 'm_s5_c_im', 'm_s5_d', 'm_s5_log_step', 'm_s5_glu_w', 'm_s5_glu_b', 'm_dil_q_norm', 'm_dil_k_norm', 'm_w_branch_a', 'm_w_branch_b', 'm_w_branch_c', 'm_w_out', 'm_norm_ffn2', 'm_ffn2_w1', 'm_ffn2_w3', 'm_ffn2_w2', 'v_ada_w', 'v_ada_b', 'v_norm_ffn1', 'v_ffn1_w1', 'v_ffn1_w3', 'v_ffn1_w2', 'v_norm_mix', 'v_w_in', 'v_gdn_conv', 'v_gdn_a_log', 'v_gdn_dt_bias', 'v_gdn_out_norm', 'v_s5_a_re', 'v_s5_a_im', 'v_s5_b_re', 'v_s5_b_im', 'v_s5_c_re', 'v_s5_c_im', 'v_s5_d', 'v_s5_log_step', 'v_s5_glu_w', 'v_s5_glu_b', 'v_dil_q_norm', 'v_dil_k_norm', 'v_w_branch_a', 'v_w_branch_b', 'v_w_branch_c', 'v_w_out', 'v_norm_ffn2', 'v_ffn2_w1', 'v_ffn2_w3', 'v_ffn2_w2']
TWIN_OUTPUTS = ['loss', 'grad_x', 'grad_ada_w', 'grad_ada_b', 'grad_norm_ffn1', 'grad_ffn1_w1', 'grad_ffn1_w3', 'grad_ffn1_w2', 'grad_norm_mix', 'grad_w_in', 'grad_gdn_conv', 'grad_gdn_a_log', 'grad_gdn_dt_bias', 'grad_gdn_out_norm', 'grad_s5_a_re', 'grad_s5_a_im', 'grad_s5_b_re', 'grad_s5_b_im', 'grad_s5_c_re', 'grad_s5_c_im', 'grad_s5_d', 'grad_s5_log_step', 'grad_s5_glu_w', 'grad_s5_glu_b', 'grad_dil_q_norm', 'grad_dil_k_norm', 'grad_w_branch_a', 'grad_w_branch_b', 'grad_w_branch_c', 'grad_w_out', 'grad_norm_ffn2', 'grad_ffn2_w1', 'grad_ffn2_w3', 'grad_ffn2_w2', 'delta_ada_w', 'delta_ada_b', 'delta_norm_ffn1', 'delta_ffn1_w1', 'delta_ffn1_w3', 'delta_ffn1_w2', 'delta_norm_mix', 'delta_w_in', 'delta_gdn_conv', 'delta_gdn_a_log', 'delta_gdn_dt_bias', 'delta_gdn_out_norm', 'delta_s5_a_re', 'delta_s5_a_im', 'delta_s5_b_re', 'delta_s5_b_im', 'delta_s5_c_re', 'delta_s5_c_im', 'delta_s5_d', 'delta_s5_log_step', 'delta_s5_glu_w', 'delta_s5_glu_b', 'delta_dil_q_norm', 'delta_dil_k_norm', 'delta_w_branch_a', 'delta_w_branch_b', 'delta_w_branch_c', 'delta_w_out', 'delta_norm_ffn2', 'delta_ffn2_w1', 'delta_ffn2_w3', 'delta_ffn2_w2', 'new_m_ada_w', 'new_m_ada_b', 'new_m_norm_ffn1', 'new_m_ffn1_w1', 'new_m_ffn1_w3', 'new_m_ffn1_w2', 'new_m_norm_mix', 'new_m_w_in', 'new_m_gdn_conv', 'new_m_gdn_a_log', 'new_m_gdn_dt_bias', 'new_m_gdn_out_norm', 'new_m_s5_a_re', 'new_m_s5_a_im', 'new_m_s5_b_re', 'new_m_s5_b_im', 'new_m_s5_c_re', 'new_m_s5_c_im', 'new_m_s5_d', 'new_m_s5_log_step', 'new_m_s5_glu_w', 'new_m_s5_glu_b', 'new_m_dil_q_norm', 'new_m_dil_k_norm', 'new_m_w_branch_a', 'new_m_w_branch_b', 'new_m_w_branch_c', 'new_m_w_out', 'new_m_norm_ffn2', 'new_m_ffn2_w1', 'new_m_ffn2_w3', 'new_m_ffn2_w2', 'new_v_ada_w', 'new_v_ada_b', 'new_v_norm_ffn1', 'new_v_ffn1_w1', 'new_v_ffn1_w3', 'new_v_ffn1_w2', 'new_v_norm_mix', 'new_v_w_in', 'new_v_gdn_conv', 'new_v_gdn_a_log', 'new_v_gdn_dt_bias', 'new_v_gdn_out_norm', 'new_v_s5_a_re', 'new_v_s5_a_im', 'new_v_s5_b_re', 'new_v_s5_b_im', 'new_v_s5_c_re', 'new_v_s5_c_im', 'new_v_s5_d', 'new_v_s5_log_step', 'new_v_s5_glu_w', 'new_v_s5_glu_b', 'new_v_dil_q_norm', 'new_v_dil_k_norm', 'new_v_w_branch_a', 'new_v_w_branch_b', 'new_v_w_branch_c', 'new_v_w_out', 'new_v_norm_ffn2', 'new_v_ffn2_w1', 'new_v_ffn2_w3', 'new_v_ffn2_w2']
TWIN_LEAF_KINDS = {'loss': 'loss', 'grad_x': 'grad_x', 'grad_ada_w': 'grad_w', 'grad_ada_b': 'grad_w', 'grad_norm_ffn1': 'grad_w', 'grad_ffn1_w1': 'grad_w', 'grad_ffn1_w3': 'grad_w', 'grad_ffn1_w2': 'grad_w', 'grad_norm_mix': 'grad_w', 'grad_w_in': 'grad_w', 'grad_gdn_conv': 'grad_w', 'grad_gdn_a_log': 'grad_w', 'grad_gdn_dt_bias': 'grad_w', 'grad_gdn_out_norm': 'grad_w', 'grad_s5_a_re': 'grad_w', 'grad_s5_a_im': 'grad_w', 'grad_s5_b_re': 'grad_w', 'grad_s5_b_im': 'grad_w', 'grad_s5_c_re': 'grad_w', 'grad_s5_c_im': 'grad_w', 'grad_s5_d': 'grad_w', 'grad_s5_log_step': 'grad_w', 'grad_s5_glu_w': 'grad_w', 'grad_s5_glu_b': 'grad_w', 'grad_dil_q_norm': 'grad_w', 'grad_dil_k_norm': 'grad_w', 'grad_w_branch_a': 'grad_w', 'grad_w_branch_b': 'grad_w', 'grad_w_branch_c': 'grad_w', 'grad_w_out': 'grad_w', 'grad_norm_ffn2': 'grad_w', 'grad_ffn2_w1': 'grad_w', 'grad_ffn2_w3': 'grad_w', 'grad_ffn2_w2': 'grad_w', 'delta_ada_w': 'delta_w', 'delta_ada_b': 'delta_w', 'delta_norm_ffn1': 'delta_w', 'delta_ffn1_w1': 'delta_w', 'delta_ffn1_w3': 'delta_w', 'delta_ffn1_w2': 'delta_w', 'delta_norm_mix': 'delta_w', 'delta_w_in': 'delta_w', 'delta_gdn_conv': 'delta_w', 'delta_gdn_a_log': 'delta_w', 'delta_gdn_dt_bias': 'delta_w', 'delta_gdn_out_norm': 'delta_w', 'delta_s5_a_re': 'delta_w', 'delta_s5_a_im': 'delta_w', 'delta_s5_b_re': 'delta_w', 'delta_s5_b_im': 'delta_w', 'delta_s5_c_re': 'delta_w', 'delta_s5_c_im': 'delta_w', 'delta_s5_d': 'delta_w', 'delta_s5_log_step': 'delta_w', 'delta_s5_glu_w': 'delta_w', 'delta_s5_glu_b': 'delta_w', 'delta_dil_q_norm': 'delta_w', 'delta_dil_k_norm': 'delta_w', 'delta_w_branch_a': 'delta_w', 'delta_w_branch_b': 'delta_w', 'delta_w_branch_c': 'delta_w', 'delta_w_out': 'delta_w', 'delta_norm_ffn2': 'delta_w', 'delta_ffn2_w1': 'delta_w', 'delta_ffn2_w3': 'delta_w', 'delta_ffn2_w2': 'delta_w', 'new_m_ada_w': 'new_m', 'new_m_ada_b': 'new_m', 'new_m_norm_ffn1': 'new_m', 'new_m_ffn1_w1': 'new_m', 'new_m_ffn1_w3': 'new_m', 'new_m_ffn1_w2': 'new_m', 'new_m_norm_mix': 'new_m', 'new_m_w_in': 'new_m', 'new_m_gdn_conv': 'new_m', 'new_m_gdn_a_log': 'new_m', 'new_m_gdn_dt_bias': 'new_m', 'new_m_gdn_out_norm': 'new_m', 'new_m_s5_a_re': 'new_m', 'new_m_s5_a_im': 'new_m', 'new_m_s5_b_re': 'new_m', 'new_m_s5_b_im': 'new_m', 'new_m_s5_c_re': 'new_m', 'new_m_s5_c_im': 'new_m', 'new_m_s5_d': 'new_m', 'new_m_s5_log_step': 'new_m', 'new_m_s5_glu_w': 'new_m', 'new_m_s5_glu_b': 'new_m', 'new_m_dil_q_norm': 'new_m', 'new_m_dil_k_norm': 'new_m', 'new_m_w_branch_a': 'new_m', 'new_m_w_branch_b': 'new_m', 'new_m_w_branch_c': 'new_m', 'new_m_w_out': 'new_m', 'new_m_norm_ffn2': 'new_m', 'new_m_ffn2_w1': 'new_m', 'new_m_ffn2_w3': 'new_m', 'new_m_ffn2_w2': 'new_m', 'new_v_ada_w': 'new_v', 'new_v_ada_b': 'new_v', 'new_v_norm_ffn1': 'new_v', 'new_v_ffn1_w1': 'new_v', 'new_v_ffn1_w3': 'new_v', 'new_v_ffn1_w2': 'new_v', 'new_v_norm_mix': 'new_v', 'new_v_w_in': 'new_v', 'new_v_gdn_conv': 'new_v', 'new_v_gdn_a_log': 'new_v', 'new_v_gdn_dt_bias': 'new_v', 'new_v_gdn_out_norm': 'new_v', 'new_v_s5_a_re': 'new_v', 'new_v_s5_a_im': 'new_v', 'new_v_s5_b_re': 'new_v', 'new_v_s5_b_im': 'new_v', 'new_v_s5_c_re': 'new_v', 'new_v_s5_c_im': 'new_v', 'new_v_s5_d': 'new_v', 'new_v_s5_log_step': 'new_v', 'new_v_s5_glu_w': 'new_v', 'new_v_s5_glu_b': 'new_v', 'new_v_dil_q_norm': 'new_v', 'new_v_dil_k_norm': 'new_v', 'new_v_w_branch_a': 'new_v', 'new_v_w_branch_b': 'new_v', 'new_v_w_branch_c': 'new_v', 'new_v_w_out': 'new_v', 'new_v_norm_ffn2': 'new_v', 'new_v_ffn2_w1': 'new_v', 'new_v_ffn2_w3': 'new_v', 'new_v_ffn2_w2': 'new_v'}


def _forward(args):
    return _fwd_reference(*[args[k] for k in FWD_PARAMS])


def _output_shape():
    out = _jax.eval_shape(lambda: _forward(_fwd_setup_inputs(0)))
    return out.shape, out.dtype

N_MICROBATCH = 1
ADAM_LR = 0.001
ADAM_B1 = 0.9
ADAM_B2 = 0.999
ADAM_EPS = 1e-08
ADAM_WD = 0.01
ADAM_STEP = 10
PER_EXAMPLE_BATCH_AXIS = {'x': 0, 'c': 0, 'loss_target': 0}
SHARED_INPUTS = []
_WEIGHT_DTYPES = {'ada_w': _jnp.float32, 'ada_b': _jnp.float32, 'norm_ffn1': _jnp.float32, 'ffn1_w1': _jnp.float32, 'ffn1_w3': _jnp.float32, 'ffn1_w2': _jnp.float32, 'norm_mix': _jnp.float32, 'w_in': _jnp.float32, 'gdn_conv': _jnp.float32, 'gdn_a_log': _jnp.float32, 'gdn_dt_bias': _jnp.float32, 'gdn_out_norm': _jnp.float32, 's5_a_re': _jnp.float32, 's5_a_im': _jnp.float32, 's5_b_re': _jnp.float32, 's5_b_im': _jnp.float32, 's5_c_re': _jnp.float32, 's5_c_im': _jnp.float32, 's5_d': _jnp.float32, 's5_log_step': _jnp.float32, 's5_glu_w': _jnp.float32, 's5_glu_b': _jnp.float32, 'dil_q_norm': _jnp.float32, 'dil_k_norm': _jnp.float32, 'w_branch_a': _jnp.float32, 'w_branch_b': _jnp.float32, 'w_branch_c': _jnp.float32, 'w_out': _jnp.float32, 'norm_ffn2': _jnp.float32, 'ffn2_w1': _jnp.float32, 'ffn2_w3': _jnp.float32, 'ffn2_w2': _jnp.float32}
MOMENT_SCALE = {'ada_w': 1.336214e-01, 'ada_b': 3.055124e-01, 'norm_ffn1': 3.551290e-01, 'ffn1_w1': 1.035842e-02, 'ffn1_w3': 9.704022e-03, 'ffn1_w2': 1.578022e-02, 'norm_mix': 3.051843e-01, 'w_in': 2.348618e-02, 'gdn_conv': 3.721622e-02, 'gdn_a_log': 8.247058e-01, 'gdn_dt_bias': 7.884484e-01, 'gdn_out_norm': 3.102353e+00, 's5_a_re': 3.564852e-03, 's5_a_im': 3.147966e-03, 's5_b_re': 2.543730e-03, 's5_b_im': 2.348335e-03, 's5_c_re': 5.775248e-03, 's5_c_im': 4.382276e-03, 's5_d': 1.957060e-01, 's5_log_step': 7.047599e-01, 's5_glu_w': 4.120432e-02, 's5_glu_b': 1.231703e-01, 'dil_q_norm': 4.985643e-01, 'dil_k_norm': 5.083419e-01, 'w_branch_a': 2.567793e-02, 'w_branch_b': 2.470447e-02, 'w_branch_c': 1.567694e-02, 'w_out': 3.305962e-02, 'norm_ffn2': 3.602147e-01, 'ffn2_w1': 1.014997e-02, 'ffn2_w3': 9.402240e-03, 'ffn2_w2': 1.533288e-02}


def _to_microbatches(a, axis):
    t = _jnp.moveaxis(a, axis, 0)
    t = t.reshape((N_MICROBATCH, t.shape[0] // N_MICROBATCH) + t.shape[1:])
    return _jnp.moveaxis(t, 1, axis + 1)


def setup_inputs(seed: int = 0) -> dict:
    inp = _fwd_setup_inputs(seed)
    key = _jax.random.fold_in(_jax.random.key(seed), 7919)
    shape, _ = _output_shape()
    out = dict(inp)
    out["loss_target"] = _jax.random.normal(_jax.random.fold_in(key, 0), shape, _jnp.float32)
    for i, name in enumerate(TWIN_WEIGHTS):
        w = inp[name].astype(_jnp.float32)
        if MOMENT_SCALE is None:
            s = _jnp.sqrt(_jnp.mean(_jnp.square(w)) + 1e-30)
        else:
            s = MOMENT_SCALE[name]
        km, kv = _jax.random.split(_jax.random.fold_in(key, i + 1))
        out[name] = w
        out["m_" + name] = s * _jax.random.normal(km, w.shape, _jnp.float32)
        out["v_" + name] = (s * s) * _jax.random.uniform(kv, w.shape, _jnp.float32, 0.5, 1.5)
    if N_MICROBATCH > 1:
        for name, axis in PER_EXAMPLE_BATCH_AXIS.items():
            out[name] = _to_microbatches(out[name], axis)
    return {'x': out['x'], 'c': out['c'], 'ada_w': out['ada_w'], 'ada_b': out['ada_b'], 'norm_ffn1': out['norm_ffn1'], 'ffn1_w1': out['ffn1_w1'], 'ffn1_w3': out['ffn1_w3'], 'ffn1_w2': out['ffn1_w2'], 'norm_mix': out['norm_mix'], 'w_in': out['w_in'], 'gdn_conv': out['gdn_conv'], 'gdn_a_log': out['gdn_a_log'], 'gdn_dt_bias': out['gdn_dt_bias'], 'gdn_out_norm': out['gdn_out_norm'], 's5_a_re': out['s5_a_re'], 's5_a_im': out['s5_a_im'], 's5_b_re': out['s5_b_re'], 's5_b_im': out['s5_b_im'], 's5_c_re': out['s5_c_re'], 's5_c_im': out['s5_c_im'], 's5_d': out['s5_d'], 's5_log_step': out['s5_log_step'], 's5_glu_w': out['s5_glu_w'], 's5_glu_b': out['s5_glu_b'], 'dil_q_norm': out['dil_q_norm'], 'dil_k_norm': out['dil_k_norm'], 'w_branch_a': out['w_branch_a'], 'w_branch_b': out['w_branch_b'], 'w_branch_c': out['w_branch_c'], 'w_out': out['w_out'], 'norm_ffn2': out['norm_ffn2'], 'ffn2_w1': out['ffn2_w1'], 'ffn2_w3': out['ffn2_w3'], 'ffn2_w2': out['ffn2_w2'], 'loss_target': out['loss_target'], 'm_ada_w': out['m_ada_w'], 'm_ada_b': out['m_ada_b'], 'm_norm_ffn1': out['m_norm_ffn1'], 'm_ffn1_w1': out['m_ffn1_w1'], 'm_ffn1_w3': out['m_ffn1_w3'], 'm_ffn1_w2': out['m_ffn1_w2'], 'm_norm_mix': out['m_norm_mix'], 'm_w_in': out['m_w_in'], 'm_gdn_conv': out['m_gdn_conv'], 'm_gdn_a_log': out['m_gdn_a_log'], 'm_gdn_dt_bias': out['m_gdn_dt_bias'], 'm_gdn_out_norm': out['m_gdn_out_norm'], 'm_s5_a_re': out['m_s5_a_re'], 'm_s5_a_im': out['m_s5_a_im'], 'm_s5_b_re': out['m_s5_b_re'], 'm_s5_b_im': out['m_s5_b_im'], 'm_s5_c_re': out['m_s5_c_re'], 'm_s5_c_im': out['m_s5_c_im'], 'm_s5_d': out['m_s5_d'], 'm_s5_log_step': out['m_s5_log_step'], 'm_s5_glu_w': out['m_s5_glu_w'], 'm_s5_glu_b': out['m_s5_glu_b'], 'm_dil_q_norm': out['m_dil_q_norm'], 'm_dil_k_norm': out['m_dil_k_norm'], 'm_w_branch_a': out['m_w_branch_a'], 'm_w_branch_b': out['m_w_branch_b'], 'm_w_branch_c': out['m_w_branch_c'], 'm_w_out': out['m_w_out'], 'm_norm_ffn2': out['m_norm_ffn2'], 'm_ffn2_w1': out['m_ffn2_w1'], 'm_ffn2_w3': out['m_ffn2_w3'], 'm_ffn2_w2': out['m_ffn2_w2'], 'v_ada_w': out['v_ada_w'], 'v_ada_b': out['v_ada_b'], 'v_norm_ffn1': out['v_norm_ffn1'], 'v_ffn1_w1': out['v_ffn1_w1'], 'v_ffn1_w3': out['v_ffn1_w3'], 'v_ffn1_w2': out['v_ffn1_w2'], 'v_norm_mix': out['v_norm_mix'], 'v_w_in': out['v_w_in'], 'v_gdn_conv': out['v_gdn_conv'], 'v_gdn_a_log': out['v_gdn_a_log'], 'v_gdn_dt_bias': out['v_gdn_dt_bias'], 'v_gdn_out_norm': out['v_gdn_out_norm'], 'v_s5_a_re': out['v_s5_a_re'], 'v_s5_a_im': out['v_s5_a_im'], 'v_s5_b_re': out['v_s5_b_re'], 'v_s5_b_im': out['v_s5_b_im'], 'v_s5_c_re': out['v_s5_c_re'], 'v_s5_c_im': out['v_s5_c_im'], 'v_s5_d': out['v_s5_d'], 'v_s5_log_step': out['v_s5_log_step'], 'v_s5_glu_w': out['v_s5_glu_w'], 'v_s5_glu_b': out['v_s5_glu_b'], 'v_dil_q_norm': out['v_dil_q_norm'], 'v_dil_k_norm': out['v_dil_k_norm'], 'v_w_branch_a': out['v_w_branch_a'], 'v_w_branch_b': out['v_w_branch_b'], 'v_w_branch_c': out['v_w_branch_c'], 'v_w_out': out['v_w_out'], 'v_norm_ffn2': out['v_norm_ffn2'], 'v_ffn2_w1': out['v_ffn2_w1'], 'v_ffn2_w3': out['v_ffn2_w3'], 'v_ffn2_w2': out['v_ffn2_w2']}


def _loss(weights, diff, rest, loss_target):
    with _jax.named_scope("forward"):
        args = {**rest, TWIN_DIFF_INPUT: diff, **{k: w.astype(_WEIGHT_DTYPES[k]) for k, w in weights.items()}}
        y = _forward(args)
    with _jax.named_scope("loss_head"):
        err = _jnp.square(y.astype(_jnp.float32) - loss_target)
        return 0.5 * _jnp.sum(_jnp.mean(err, axis=-1)) if err.ndim else 0.5 * err


def _adamw(w, g, m, v):
    m = ADAM_B1 * m + (1.0 - ADAM_B1) * g
    v = ADAM_B2 * v + (1.0 - ADAM_B2) * _jnp.square(g)
    m_hat = m / (1.0 - ADAM_B1 ** ADAM_STEP)
    v_hat = v / (1.0 - ADAM_B2 ** ADAM_STEP)
    delta = -ADAM_LR * (m_hat / (_jnp.sqrt(v_hat) + ADAM_EPS) + ADAM_WD * w)
    return delta, m, v


def reference(x, c, ada_w, ada_b, norm_ffn1, ffn1_w1, ffn1_w3, ffn1_w2, norm_mix, w_in, gdn_conv, gdn_a_log, gdn_dt_bias, gdn_out_norm, s5_a_re, s5_a_im, s5_b_re, s5_b_im, s5_c_re, s5_c_im, s5_d, s5_log_step, s5_glu_w, s5_glu_b, dil_q_norm, dil_k_norm, w_branch_a, w_branch_b, w_branch_c, w_out, norm_ffn2, ffn2_w1, ffn2_w3, ffn2_w2, loss_target, m_ada_w, m_ada_b, m_norm_ffn1, m_ffn1_w1, m_ffn1_w3, m_ffn1_w2, m_norm_mix, m_w_in, m_gdn_conv, m_gdn_a_log, m_gdn_dt_bias, m_gdn_out_norm, m_s5_a_re, m_s5_a_im, m_s5_b_re, m_s5_b_im, m_s5_c_re, m_s5_c_im, m_s5_d, m_s5_log_step, m_s5_glu_w, m_s5_glu_b, m_dil_q_norm, m_dil_k_norm, m_w_branch_a, m_w_branch_b, m_w_branch_c, m_w_out, m_norm_ffn2, m_ffn2_w1, m_ffn2_w3, m_ffn2_w2, v_ada_w, v_ada_b, v_norm_ffn1, v_ffn1_w1, v_ffn1_w3, v_ffn1_w2, v_norm_mix, v_w_in, v_gdn_conv, v_gdn_a_log, v_gdn_dt_bias, v_gdn_out_norm, v_s5_a_re, v_s5_a_im, v_s5_b_re, v_s5_b_im, v_s5_c_re, v_s5_c_im, v_s5_d, v_s5_log_step, v_s5_glu_w, v_s5_glu_b, v_dil_q_norm, v_dil_k_norm, v_w_branch_a, v_w_branch_b, v_w_branch_c, v_w_out, v_norm_ffn2, v_ffn2_w1, v_ffn2_w3, v_ffn2_w2):
    given = dict(x=x, c=c, ada_w=ada_w, ada_b=ada_b, norm_ffn1=norm_ffn1, ffn1_w1=ffn1_w1, ffn1_w3=ffn1_w3, ffn1_w2=ffn1_w2, norm_mix=norm_mix, w_in=w_in, gdn_conv=gdn_conv, gdn_a_log=gdn_a_log, gdn_dt_bias=gdn_dt_bias, gdn_out_norm=gdn_out_norm, s5_a_re=s5_a_re, s5_a_im=s5_a_im, s5_b_re=s5_b_re, s5_b_im=s5_b_im, s5_c_re=s5_c_re, s5_c_im=s5_c_im, s5_d=s5_d, s5_log_step=s5_log_step, s5_glu_w=s5_glu_w, s5_glu_b=s5_glu_b, dil_q_norm=dil_q_norm, dil_k_norm=dil_k_norm, w_branch_a=w_branch_a, w_branch_b=w_branch_b, w_branch_c=w_branch_c, w_out=w_out, norm_ffn2=norm_ffn2, ffn2_w1=ffn2_w1, ffn2_w3=ffn2_w3, ffn2_w2=ffn2_w2, loss_target=loss_target, m_ada_w=m_ada_w, m_ada_b=m_ada_b, m_norm_ffn1=m_norm_ffn1, m_ffn1_w1=m_ffn1_w1, m_ffn1_w3=m_ffn1_w3, m_ffn1_w2=m_ffn1_w2, m_norm_mix=m_norm_mix, m_w_in=m_w_in, m_gdn_conv=m_gdn_conv, m_gdn_a_log=m_gdn_a_log, m_gdn_dt_bias=m_gdn_dt_bias, m_gdn_out_norm=m_gdn_out_norm, m_s5_a_re=m_s5_a_re, m_s5_a_im=m_s5_a_im, m_s5_b_re=m_s5_b_re, m_s5_b_im=m_s5_b_im, m_s5_c_re=m_s5_c_re, m_s5_c_im=m_s5_c_im, m_s5_d=m_s5_d, m_s5_log_step=m_s5_log_step, m_s5_glu_w=m_s5_glu_w, m_s5_glu_b=m_s5_glu_b, m_dil_q_norm=m_dil_q_norm, m_dil_k_norm=m_dil_k_norm, m_w_branch_a=m_w_branch_a, m_w_branch_b=m_w_branch_b, m_w_branch_c=m_w_branch_c, m_w_out=m_w_out, m_norm_ffn2=m_norm_ffn2, m_ffn2_w1=m_ffn2_w1, m_ffn2_w3=m_ffn2_w3, m_ffn2_w2=m_ffn2_w2, v_ada_w=v_ada_w, v_ada_b=v_ada_b, v_norm_ffn1=v_norm_ffn1, v_ffn1_w1=v_ffn1_w1, v_ffn1_w3=v_ffn1_w3, v_ffn1_w2=v_ffn1_w2, v_norm_mix=v_norm_mix, v_w_in=v_w_in, v_gdn_conv=v_gdn_conv, v_gdn_a_log=v_gdn_a_log, v_gdn_dt_bias=v_gdn_dt_bias, v_gdn_out_norm=v_gdn_out_norm, v_s5_a_re=v_s5_a_re, v_s5_a_im=v_s5_a_im, v_s5_b_re=v_s5_b_re, v_s5_b_im=v_s5_b_im, v_s5_c_re=v_s5_c_re, v_s5_c_im=v_s5_c_im, v_s5_d=v_s5_d, v_s5_log_step=v_s5_log_step, v_s5_glu_w=v_s5_glu_w, v_s5_glu_b=v_s5_glu_b, v_dil_q_norm=v_dil_q_norm, v_dil_k_norm=v_dil_k_norm, v_w_branch_a=v_w_branch_a, v_w_branch_b=v_w_branch_b, v_w_branch_c=v_w_branch_c, v_w_out=v_w_out, v_norm_ffn2=v_norm_ffn2, v_ffn2_w1=v_ffn2_w1, v_ffn2_w3=v_ffn2_w3, v_ffn2_w2=v_ffn2_w2)
    weights = {n: given[n] for n in TWIN_WEIGHTS}
    shared = {n: given[n] for n in SHARED_INPUTS}
    per_example = {n: given[n] for n in ['x', 'c']}
    grad_fn = _jax.value_and_grad(_loss, argnums=(0, 1))

    def one_microbatch(ex, loss_target):
        ex = dict(ex)
        diff = ex.pop(TWIN_DIFF_INPUT)
        return grad_fn(weights, diff, {**shared, **ex}, loss_target)

    if N_MICROBATCH == 1:
        loss, (grad_w, grad_x) = one_microbatch(per_example, given["loss_target"])
    else:
        def body(carry, xs):
            loss_sum, grad_sum = carry
            l_k, (gw_k, gx_k) = one_microbatch(xs[0], xs[1])
            with _jax.named_scope("update"):
                return (loss_sum + l_k, _jax.tree.map(_jnp.add, grad_sum, gw_k)), gx_k

        init = (_jnp.zeros((), _jnp.float32), _jax.tree.map(_jnp.zeros_like, weights))
        (loss, grad_w), grad_x = _jax.lax.scan(body, init, (per_example, given["loss_target"]))
    with _jax.named_scope("update"):
        delta_w, new_m, new_v = {}, {}, {}
        for n in TWIN_WEIGHTS:
            delta_w[n], new_m[n], new_v[n] = _adamw(weights[n], grad_w[n], given["m_" + n], given["v_" + n])
    return (loss, grad_x, *[grad_w[n] for n in TWIN_WEIGHTS], *[delta_w[n] for n in TWIN_WEIGHTS],
            *[new_m[n] for n in TWIN_WEIGHTS], *[new_v[n] for n in TWIN_WEIGHTS])
```

```python
import functools
import math

import jax
import jax.numpy as jnp
import numpy as np
from jax import lax
from jax.experimental import pallas as pl
from jax.experimental.pallas import tpu as pltpu

F32 = jnp.float32
BF16 = jnp.bfloat16

LANES = 128
SUBLANES = 8
VMEM_LIMIT = 56 * 1024 * 1024

N_DEV = 8
EPS = 1e-6
FFN_RES = 0.5
GDN_HEADS = 8
GDN_DIM = 128
GDN_CONV = 4
GDN_CHUNK = 128
S5_GROUP = 16
S5_STATE = 64
S5_MAX_RE = -1e-4
S5_TCHUNK = 512
DIL_PAIRS = ((128, 1), (512, 4), (2048, 16))
DIL_HPG = 4
DIL_DIM = 64
DIL_BLK = 128
ALIBI_MAX = 8.0
ADAM_LR, ADAM_B1, ADAM_B2, ADAM_EPS, ADAM_WD, ADAM_STEP = 0.001, 0.9, 0.999, 1e-08, 0.01, 10

HI = lax.Precision.HIGHEST


def _cparams(sem=None, **kw):
    return pltpu.CompilerParams(dimension_semantics=sem, vmem_limit_bytes=VMEM_LIMIT, **kw)


def _pick(n, cands):
    for c in cands:
        if n % c == 0:
            return c
    return n


def _mm(a, b, *, ta=False, tb=False, out_dtype=F32, name):
    M, K = (a.shape[1], a.shape[0]) if ta else a.shape
    N = b.shape[0] if tb else b.shape[1]
    assert (b.shape[1] if tb else b.shape[0]) == K, (a.shape, b.shape, ta, tb)
    tm = _pick(M, (1024, 512, 256, 128))
    tn = _pick(N, (1024, 768, 512, 384, 256, 128))
    tk = _pick(K, (512, 256, 128))
    nk = K // tk

    def body(a_ref, b_ref, o_ref, acc_ref):
        k = pl.program_id(2)

        @pl.when(k == 0)
        def _():
            acc_ref[...] = jnp.zeros_like(acc_ref)

        dn = (((0 if ta else 1,), (1 if tb else 0,)), ((), ()))
        acc_ref[...] += lax.dot_general(a_ref[...], b_ref[...], dn, preferred_element_type=F32)

        @pl.when(k == nk - 1)
        def _():
            o_ref[...] = acc_ref[...].astype(o_ref.dtype)

    a_spec = pl.BlockSpec((tk, tm), lambda i, j, k: (k, i)) if ta else pl.BlockSpec((tm, tk), lambda i, j, k: (i, k))
    b_spec = pl.BlockSpec((tn, tk), lambda i, j, k: (j, k)) if tb else pl.BlockSpec((tk, tn), lambda i, j, k: (k, j))
    return pl.pallas_call(
        body, name=name,
        out_shape=jax.ShapeDtypeStruct((M, N), out_dtype),
        grid=(M // tm, N // tn, nk),
        in_specs=[a_spec, b_spec],
        out_specs=pl.BlockSpec((tm, tn), lambda i, j, k: (i, j)),
        scratch_shapes=[pltpu.VMEM((tm, tn), F32)],
        compiler_params=_cparams(("parallel", "parallel", "arbitrary")),
    )(a, b)


def _norm_arg(a):
    return a if isinstance(a, tuple) else (a, None, 0)


def _ew_specs(rows, exs, ws, tr, tpe):
    specs = []
    for arr, cw, off in rows:
        if cw is None:
            specs.append(pl.BlockSpec((tr, arr.shape[1]), lambda j, i: (i, 0)))
        else:
            specs.append(pl.BlockSpec((tr, cw), lambda j, i, off=off: (i, j + off)))
    for arr, cw, off in exs:
        if cw is None:
            specs.append(pl.BlockSpec((1, 1, arr.shape[2]), lambda j, i: (i // tpe, 0, 0)))
        else:
            specs.append(pl.BlockSpec((1, 1, cw), lambda j, i, off=off: (i // tpe, 0, j + off)))
    for arr, cw, off in ws:
        if cw is None:
            specs.append(pl.BlockSpec((1, arr.shape[1]), lambda j, i: (0, 0)))
        else:
            specs.append(pl.BlockSpec((1, cw), lambda j, i, off=off: (0, j + off)))
    return specs


def _ew_fwd(fn, rows, exs, ws, outs, *, ncb=1, tr=256, seq=None, name):
    rows, exs, ws = [list(map(_norm_arg, g)) for g in (rows, exs, ws)]
    T = rows[0][0].shape[0]
    seq = seq or T
    tr = min(tr, seq)
    tpe = seq // tr
    nr, ne, nw = len(rows), len(exs), len(ws)

    def body(*refs):
        ins = [r[...].astype(F32) for r in refs[:nr]]
        ins += [r[0].astype(F32) for r in refs[nr:nr + ne]]
        ins += [r[...].astype(F32) for r in refs[nr + ne:nr + ne + nw]]
        res = fn(*ins)
        if not isinstance(res, (tuple, list)):
            res = (res,)
        for o_ref, v in zip(refs[nr + ne + nw:], res):
            o_ref[...] = v.astype(o_ref.dtype)

    out_shape, out_specs = [], []
    for width, dtype, blocked in outs:
        out_shape.append(jax.ShapeDtypeStruct((T, width), dtype))
        if blocked:
            out_specs.append(pl.BlockSpec((tr, width // ncb), lambda j, i: (i, j)))
        else:
            out_specs.append(pl.BlockSpec((tr, width), lambda j, i: (i, 0)))
    res = pl.pallas_call(
        body, name=name, out_shape=out_shape, grid=(ncb, T // tr),
        in_specs=_ew_specs(rows, exs, ws, tr, tpe), out_specs=out_specs,
        compiler_params=_cparams(("parallel", "parallel")),
    )(*[a[0] for a in rows + exs + ws])
    return res


def _ew_bwd(fn, rows, exs, ws, douts, need, *, ncb=1, tr=256, seq=None, row_dtypes=None, name):
    rows, exs, ws, douts = [list(map(_norm_arg, g)) for g in (rows, exs, ws, douts)]
    T = rows[0][0].shape[0]
    seq = seq or T
    tr = min(tr, seq)
    tpe = seq // tr
    nrt = T // tr
    nr, ne, nw, nd = len(rows), len(exs), len(ws), len(douts)
    nin = nr + ne + nw
    args = rows + exs + ws
    row_dtypes = row_dtypes or [F32] * nr
    for k, (arr, cw, off) in enumerate(exs):
        assert not (need[nr + k] and cw is None and ncb > 1)

    def body(*refs):
        j, i = pl.program_id(0), pl.program_id(1)
        ins = [r[...].astype(F32) for r in refs[:nr]]
        ins += [r[0].astype(F32) for r in refs[nr:nr + ne]]
        ins += [r[...].astype(F32) for r in refs[nr + ne:nin]]
        cts = [r[...].astype(F32) for r in refs[nin:nin + nd]]
        res, vjp = jax.vjp(fn, *ins)
        if isinstance(res, (tuple, list)):
            grads = vjp(tuple(cts))
        else:
            grads = vjp(cts[0])
        o = nin + nd
        for k in range(nin):
            if not need[k]:
                continue
            o_ref, g = refs[o], grads[k]
            o += 1
            if k < nr:
                o_ref[...] = g.astype(o_ref.dtype)
            elif k < nr + ne:
                first = (i % tpe) == 0

                @pl.when(first)
                def _(o_ref=o_ref, g=g):
                    o_ref[0] = g

                @pl.when(jnp.logical_not(first))
                def _(o_ref=o_ref, g=g):
                    o_ref[0] += g
            else:
                blocked = args[k][1] is not None
                first = (i == 0) if blocked else jnp.logical_and(i == 0, j == 0)

                @pl.when(first)
                def _(o_ref=o_ref, g=g):
                    o_ref[...] = g

                @pl.when(jnp.logical_not(first))
                def _(o_ref=o_ref, g=g):
                    o_ref[...] += g

    in_specs = _ew_specs(rows, exs, ws, tr, tpe) + _ew_specs(douts, [], [], tr, tpe)
    out_shape, out_specs = [], []
    all_specs = _ew_specs(rows, exs, ws, tr, tpe)
    for k in range(nin):
        if not need[k]:
            continue
        arr, cw, off = args[k]
        if k < nr and cw is not None:
            out_shape.append(jax.ShapeDtypeStruct((T, ncb * cw), row_dtypes[k]))
            out_specs.append(pl.BlockSpec((tr, cw), lambda j, i: (i, j)))
        elif k >= nr and cw is not None:
            assert off == 0 and arr.shape[-1] == ncb * cw
            out_shape.append(jax.ShapeDtypeStruct(arr.shape, F32))
            out_specs.append(all_specs[k])
        else:
            out_shape.append(jax.ShapeDtypeStruct(arr.shape, row_dtypes[k] if k < nr else F32))
            out_specs.append(all_specs[k])
    res = pl.pallas_call(
        body, name=name, out_shape=out_shape, grid=(ncb, nrt),
        in_specs=in_specs, out_specs=out_specs,
        compiler_params=_cparams(("arbitrary", "arbitrary")),
    )(*[a[0] for a in args + douts])
    return res


def _rms(x, g):
    return x * lax.rsqrt(jnp.mean(x * x, axis=-1, keepdims=True) + EPS) * g


def _f_rmsmod(x, sh, sc, g):
    return _rms(x, g) * (1.0 + sc) + sh


def _f_swiglu(a, b):
    return a * jax.nn.sigmoid(a) * b


def _f_res(res, x, y, gate):
    return x + res * gate * y


def _f_res_rmsmod(res, x, y, gate, sh, sc, g):
    x1 = x + res * gate * y
    return x1, _rms(x1, g) * (1.0 + sc) + sh


def _ffn_fwd(h, w13, w2, tag):
    F = w2.shape[0]
    cw = _pick(F, (512, 256, 128))
    ncb = F // cw
    ab = _mm(h, w13, name=f"{tag}_up")
    (s,) = _ew_fwd(_f_swiglu, [(ab, cw, 0), (ab, cw, ncb)], [], [], [(F, BF16, True)], ncb=ncb, name=f"{tag}_act")
    y = _mm(s, w2, name=f"{tag}_down")
    return y, (h, ab, s)


def _ffn_bwd(dy, saved, w13, w2, tag):
    h, ab, s = saved
    F = w2.shape[0]
    cw = _pick(F, (512, 256, 128))
    ncb = F // cw
    ds = _mm(dy, w2, tb=True, name=f"{tag}_down_dx")
    dw2 = _mm(s, dy, ta=True, out_dtype=BF16, name=f"{tag}_down_dw")
    da, db = _ew_bwd(_f_swiglu, [(ab, cw, 0), (ab, cw, ncb)], [], [], [(ds, cw, 0)], [True, True], ncb=ncb,
                     row_dtypes=[BF16, BF16], name=f"{tag}_act_bwd")
    dab = jnp.concatenate([da, db], axis=1)
    dw13 = _mm(h, dab, ta=True, out_dtype=BF16, name=f"{tag}_up_dw")
    dh = _mm(dab, w13, tb=True, name=f"{tag}_up_dx")
    return dh, dw13, dw2


def _shift_down(x, j):
    if j == 0:
        return x
    row = lax.broadcasted_iota(jnp.int32, x.shape, 0)
    return jnp.where(row >= j, pltpu.roll(x, j, 0), 0.0)


def _shift_up(x, j):
    if j == 0:
        return x
    n = x.shape[0]
    row = lax.broadcasted_iota(jnp.int32, x.shape, 0)
    return jnp.where(row < n - j, pltpu.roll(x, n - j, 0), 0.0)


def _gdn_post_conv(kind, y):
    s = y * jax.nn.sigmoid(y)
    if kind == "v":
        return s
    n = lax.rsqrt(jnp.sum(s * s, axis=-1, keepdims=True) + EPS)
    return s * n * (GDN_DIM ** -0.5 if kind == "q" else 1.0)


def _conv_taps(w_ref):
    return [w_ref[k:k + 1, :] for k in range(GDN_CONV)]


def _gdn_conv(x, w):
    y = w[GDN_CONV - 1] * x
    for k in range(GDN_CONV - 1):
        y = y + w[k] * _shift_down(x, GDN_CONV - 1 - k)
    return y


def _gdn_pre_fwd(proj, conv_w, kind, first_block, B, name):
    T = proj.shape[0]
    S = T // B
    nh = GDN_HEADS

    def body(x_ref, w_ref, o_ref):
        y = _gdn_conv(x_ref[...], _conv_taps(w_ref))
        o_ref[...] = _gdn_post_conv(kind, y)

    return pl.pallas_call(
        body, name=name, out_shape=jax.ShapeDtypeStruct((T, nh * GDN_DIM), F32), grid=(nh, B),
        in_specs=[pl.BlockSpec((S, GDN_DIM), lambda c, b: (b, c + first_block)),
                  pl.BlockSpec((GDN_CONV, GDN_DIM), lambda c, b: (0, c))],
        out_specs=pl.BlockSpec((S, GDN_DIM), lambda c, b: (b, c)),
        compiler_params=_cparams(("parallel", "parallel")),
    )(proj, conv_w)


def _gdn_pre_bwd(proj, conv_w, dout, kind, first_block, B, name):
    T = proj.shape[0]
    S = T // B
    nh = GDN_HEADS

    def body(x_ref, w_ref, d_ref, dx_ref, dw_ref):
        b = pl.program_id(1)
        x, w = x_ref[...], _conv_taps(w_ref)
        y = _gdn_conv(x, w)
        _, vjp = jax.vjp(functools.partial(_gdn_post_conv, kind), y)
        (dy,) = vjp(d_ref[...])
        dx = w[GDN_CONV - 1] * dy

        @pl.when(b == 0)
        def _():
            dw_ref[...] = jnp.zeros_like(dw_ref)

        for k in range(GDN_CONV):
            j = GDN_CONV - 1 - k
            if j:
                dx = dx + w[k] * _shift_up(dy, j)
            dw_ref[k:k + 1, :] += jnp.sum(dy * _shift_down(x, j), axis=0, keepdims=True)
        dx_ref[...] = dx

    return pl.pallas_call(
        body, name=name,
        out_shape=[jax.ShapeDtypeStruct((T, nh * GDN_DIM), F32), jax.ShapeDtypeStruct((GDN_CONV, nh * GDN_DIM), F32)],
        grid=(nh, B),
        in_specs=[pl.BlockSpec((S, GDN_DIM), lambda c, b: (b, c + first_block)),
                  pl.BlockSpec((GDN_CONV, GDN_DIM), lambda c, b: (0, c)),
                  pl.BlockSpec((S, GDN_DIM), lambda c, b: (b, c))],
        out_specs=[pl.BlockSpec((S, GDN_DIM), lambda c, b: (b, c)),
                   pl.BlockSpec((GDN_CONV, GDN_DIM), lambda c, b: (0, c))],
        compiler_params=_cparams(("arbitrary", "arbitrary")),
    )(proj, conv_w, dout)


def _softplus(x):
    return jnp.maximum(x, 0.0) + jnp.log(1.0 + jnp.exp(-jnp.abs(x)))


def _f_gdn_gates(ba, a_log, dt_bias):
    n = ba.shape[0]
    lane = lax.broadcasted_iota(jnp.int32, ba.shape, 1)
    beta = jax.nn.sigmoid(ba)
    g = -jnp.exp(a_log) * _softplus(ba + dt_bias)
    ri = lax.broadcasted_iota(jnp.int32, (n, n), 0)
    ci = lax.broadcasted_iota(jnp.int32, (n, n), 1)
    tri = jnp.where((ri // GDN_CHUNK == ci // GDN_CHUNK) & (ci <= ri), 1.0, 0.0)
    gc = jnp.dot(tri, g, precision=HI, preferred_element_type=F32)
    return jnp.where(lane < GDN_HEADS, beta, gc)


def _bmm(a, b, ca, cb):
    return lax.dot_general(a, b, (((ca,), (cb,)), ((0,), (0,))), precision=HI, preferred_element_type=F32)


GDN_INV_LEAF = 16


def _unit_lower_inverse(low, ri, ci):
    C = low.shape[1]
    b = GDN_INV_LEAF
    p = jnp.where(ri // b == ci // b, low, 0.0)
    x = jnp.where(ci == ri, 1.0, 0.0) - p
    for _ in range(int(math.log2(b)) - 1):
        p = _bmm(p, p, 2, 1)
        x = x + _bmm(x, p, 2, 1)
    while b < C:
        off = jnp.where(jnp.logical_and(ri // (2 * b) == ci // (2 * b), ri // b != ci // b), low, 0.0)
        x = x - _bmm(_bmm(x, off, 2, 1), x, 2, 1)
        b *= 2
    return x


def _gdn_chunk_fn(q, k, v, gc, beta, h):
    N, C, d = q.shape
    ri = lax.broadcasted_iota(jnp.int32, (N, C, C), 1)
    ci = lax.broadcasted_iota(jnp.int32, (N, C, C), 2)
    kb = k * beta
    vb = v * beta
    gi = jnp.broadcast_to(gc, (N, C, C))
    gj = _bmm(jnp.full((N, C, d), 1.0 / d, F32), jnp.broadcast_to(gc, (N, C, d)), 2, 2)
    decay = jnp.exp(jnp.where(ci <= ri, gi - gj, -1e30))
    low = jnp.where(ci < ri, _bmm(kb, k, 2, 2) * decay, 0.0)
    ainv = _unit_lower_inverse(low, ri, ci)
    eg = jnp.exp(gc)
    u = _bmm(ainv, vb, 2, 1)
    w = _bmm(ainv, kb * eg, 2, 1)
    attn = _bmm(q, k, 2, 2) * decay
    v_new = u - _bmm(w, h, 2, 1)
    o = _bmm(q * eg, h, 2, 1) + _bmm(attn, v_new, 2, 1)
    rc = lax.broadcasted_iota(jnp.int32, (N, C, 1), 1)
    g_last = jnp.sum(jnp.where(rc == C - 1, gc, 0.0), axis=1, keepdims=True)
    h_new = h * jnp.exp(g_last) + _bmm(k * jnp.exp(g_last - gc), v_new, 1, 1)
    return o, h_new


def _gdn_heads(x):
    return jnp.stack([x[:, h * GDN_DIM:(h + 1) * GDN_DIM] for h in range(GDN_HEADS)], axis=0)


def _gdn_gate_cols(G, first_lane):
    lane = lax.broadcasted_iota(jnp.int32, G.shape, 1)
    return jnp.stack([jnp.sum(jnp.where(lane == first_lane + h, G, 0.0), axis=1, keepdims=True)
                      for h in range(GDN_HEADS)], axis=0)


def _gdn_chunk_specs(nc, rev):
    C, W = GDN_CHUNK, GDN_HEADS * GDN_DIM

    def at(n):
        return nc - 1 - n if rev else n

    row = lambda b, n: (b * nc + at(n), 0)
    return [pl.BlockSpec((C, W), row)] * 3 + [pl.BlockSpec((C, LANES), row)]


def _gdn_scan_fwd(q, k, v, G, B, name):
    T, W = q.shape
    C = GDN_CHUNK
    nc = T // B // C

    def body(q_ref, k_ref, v_ref, g_ref, o_ref, hs_ref, h_ref):
        @pl.when(pl.program_id(1) == 0)
        def _():
            h_ref[...] = jnp.zeros_like(h_ref)

        G_ = g_ref[...]
        h = h_ref[...]
        hs_ref[0, 0] = h
        o, hn = _gdn_chunk_fn(_gdn_heads(q_ref[...]), _gdn_heads(k_ref[...]), _gdn_heads(v_ref[...]),
                              _gdn_gate_cols(G_, GDN_HEADS), _gdn_gate_cols(G_, 0), h)
        h_ref[...] = hn
        for hd in range(GDN_HEADS):
            o_ref[:, hd * GDN_DIM:(hd + 1) * GDN_DIM] = o[hd]

    return pl.pallas_call(
        body, name=name,
        out_shape=[jax.ShapeDtypeStruct((T, W), F32), jax.ShapeDtypeStruct((B, nc, GDN_HEADS, GDN_DIM, GDN_DIM), F32)],
        grid=(B, nc), in_specs=_gdn_chunk_specs(nc, False),
        out_specs=[pl.BlockSpec((C, W), lambda b, n: (b * nc + n, 0)),
                   pl.BlockSpec((1, 1, GDN_HEADS, GDN_DIM, GDN_DIM), lambda b, n: (b, n, 0, 0, 0))],
        scratch_shapes=[pltpu.VMEM((GDN_HEADS, GDN_DIM, GDN_DIM), F32)],
        compiler_params=_cparams(("parallel", "arbitrary")),
    )(q, k, v, G)


def _gdn_scan_bwd(q, k, v, G, hs, do, B, name):
    T, W = q.shape
    C = GDN_CHUNK
    nc = T // B // C

    def body(q_ref, k_ref, v_ref, g_ref, hs_ref, do_ref, dq_ref, dk_ref, dv_ref, dg_ref, dh_ref):
        @pl.when(pl.program_id(1) == 0)
        def _():
            dh_ref[...] = jnp.zeros_like(dh_ref)

        G_ = g_ref[...]
        args = (_gdn_heads(q_ref[...]), _gdn_heads(k_ref[...]), _gdn_heads(v_ref[...]),
                _gdn_gate_cols(G_, GDN_HEADS), _gdn_gate_cols(G_, 0), hs_ref[0, 0])
        _, vjp = jax.vjp(_gdn_chunk_fn, *args)
        dq, dk, dv, dgc, dbeta, dh = vjp((_gdn_heads(do_ref[...]), dh_ref[...]))
        dh_ref[...] = dh
        lane = lax.broadcasted_iota(jnp.int32, G_.shape, 1)
        dG = jnp.zeros_like(G_)
        for hd in range(GDN_HEADS):
            sl = slice(hd * GDN_DIM, (hd + 1) * GDN_DIM)
            dq_ref[:, sl] = dq[hd]
            dk_ref[:, sl] = dk[hd]
            dv_ref[:, sl] = dv[hd]
            dG = dG + jnp.where(lane == hd, dbeta[hd], 0.0) + jnp.where(lane == GDN_HEADS + hd, dgc[hd], 0.0)
        dg_ref[...] = dG

    rrow = lambda b, n: (b * nc + nc - 1 - n, 0)
    return pl.pallas_call(
        body, name=name,
        out_shape=[jax.ShapeDtypeStruct((T, W), F32)] * 3 + [jax.ShapeDtypeStruct((T, LANES), F32)],
        grid=(B, nc),
        in_specs=_gdn_chunk_specs(nc, True) + [
            pl.BlockSpec((1, 1, GDN_HEADS, GDN_DIM, GDN_DIM), lambda b, n: (b, nc - 1 - n, 0, 0, 0)),
            pl.BlockSpec((C, W), rrow)],
        out_specs=[pl.BlockSpec((C, W), rrow)] * 3 + [pl.BlockSpec((C, LANES), rrow)],
        scratch_shapes=[pltpu.VMEM((GDN_HEADS, GDN_DIM, GDN_DIM), F32)],
        compiler_params=_cparams(("parallel", "arbitrary")),
    )(q, k, v, G, hs, do)


def _f_gdn_out(o, z, w):
    return _rms(o, w) * z * jax.nn.sigmoid(z)


def _gdn_fwd(qkv, z, ba, conv_w, gate_params, out_norm, B, tag):
    a_log, dt_bias = gate_params
    W = GDN_HEADS * GDN_DIM
    qn, kn, vn = [_gdn_pre_fwd(qkv, conv_w[:, i * W:(i + 1) * W], kd, i * GDN_HEADS, B, f"{tag}_pre_{kd}")
                  for i, kd in enumerate("qkv")]
    (G,) = _ew_fwd(_f_gdn_gates, [ba], [], [a_log, dt_bias], [(LANES, F32, False)], name=f"{tag}_gates")
    o, hs = _gdn_scan_fwd(qn, kn, vn, G, B, f"{tag}_scan")
    (y,) = _ew_fwd(_f_gdn_out, [(o, GDN_DIM, 0), (z, GDN_DIM, 0)], [], [out_norm], [(W, BF16, True)],
                   ncb=GDN_HEADS, name=f"{tag}_out")
    return y, (qkv, z, ba, qn, kn, vn, G, hs, o)


def _gdn_bwd(dy, saved, conv_w, gate_params, out_norm, B, tag):
    qkv, z, ba, qn, kn, vn, G, hs, o = saved
    a_log, dt_bias = gate_params
    W = GDN_HEADS * GDN_DIM
    do, dz, d_out_norm = _ew_bwd(_f_gdn_out, [(o, GDN_DIM, 0), (z, GDN_DIM, 0)], [], [out_norm],
                                 [(dy, GDN_DIM, 0)], [True] * 3, ncb=GDN_HEADS, name=f"{tag}_out_bwd")
    dq, dk, dv, dG = _gdn_scan_bwd(qn, kn, vn, G, hs, do, B, f"{tag}_scan_bwd")
    dba, d_a_log, d_dt_bias = _ew_bwd(_f_gdn_gates, [ba], [], [a_log, dt_bias], [dG], [True] * 3,
                                      name=f"{tag}_gates_bwd")
    dxs, dws = [], []
    for i, (kd, d) in enumerate(zip("qkv", (dq, dk, dv))):
        dx, dw = _gdn_pre_bwd(qkv, conv_w[:, i * W:(i + 1) * W], d, kd, i * GDN_HEADS, B, f"{tag}_pre_{kd}_bwd")
        dxs.append(dx)
        dws.append(dw)
    return (jnp.concatenate(dxs, axis=1), dz, dba,
            dict(conv=jnp.concatenate(dws, axis=1), a_log=d_a_log, dt_bias=d_dt_bias, out_norm=d_out_norm))


S5_GPB = LANES // S5_GROUP
S5_SLANES = S5_GPB * S5_STATE


def _cmul(ar, ai, br, bi):
    return ar * br - ai * bi, ar * bi + ai * br


def _s5_prep_fn(a_re, a_im, ls, b_re, b_im):
    lr = jnp.minimum(a_re, S5_MAX_RE)
    li = a_im
    step = jnp.exp(ls)
    mag = jnp.exp(lr * step)
    lbr, lbi = mag * jnp.cos(li * step), mag * jnp.sin(li * step)
    den = lr * lr + li * li
    cr = ((lbr - 1.0) * lr + lbi * li) / den
    ci = (lbi * lr - (lbr - 1.0) * li) / den
    bbr = cr[:, None, :] * b_re - ci[:, None, :] * b_im
    bbi = cr[:, None, :] * b_im + ci[:, None, :] * b_re
    return lbr, lbi, bbr, bbi


def _s5_prep_fwd(args, name):
    G, I, P = args[3].shape
    shp = [jax.ShapeDtypeStruct((G, P), F32)] * 2 + [jax.ShapeDtypeStruct((G, I, P), F32)] * 2

    def body(*refs):
        for o_ref, v in zip(refs[5:], _s5_prep_fn(*[r[...] for r in refs[:5]])):
            o_ref[...] = v

    return pl.pallas_call(body, name=name, out_shape=shp, compiler_params=_cparams())(*args)


def _s5_prep_bwd(args, cts, name):
    shp = [jax.ShapeDtypeStruct(a.shape, F32) for a in args]

    def body(*refs):
        _, vjp = jax.vjp(_s5_prep_fn, *[r[...] for r in refs[:5]])
        for o_ref, v in zip(refs[9:], vjp(tuple(r[...] for r in refs[5:9]))):
            o_ref[...] = v

    return pl.pallas_call(body, name=name, out_shape=shp, compiler_params=_cparams())(*args, *cts)


def _s5_blockdiag_in(bb):
    G, I, P = bb.shape
    nb = G // S5_GPB
    return jnp.einsum("jgip,gh->jgihp", bb.reshape(nb, S5_GPB, I, P), jnp.eye(S5_GPB, dtype=bb.dtype)).reshape(
        nb, S5_GPB * I, S5_GPB * P)


def _s5_blockdiag_in_t(d):
    nb = d.shape[0]
    d = d.reshape(nb, S5_GPB, S5_GROUP, S5_GPB, S5_STATE)
    return jnp.einsum("jgihp,gh->jgip", d, jnp.eye(S5_GPB, dtype=d.dtype)).reshape(nb * S5_GPB, S5_GROUP, S5_STATE)


def _s5_blockdiag_out(c):
    G, I, P = c.shape
    nb = G // S5_GPB
    return jnp.einsum("jgip,gh->jgphi", c.reshape(nb, S5_GPB, I, P), jnp.eye(S5_GPB, dtype=c.dtype)).reshape(
        nb, S5_GPB * P, S5_GPB * I)


def _s5_blockdiag_out_t(d):
    nb = d.shape[0]
    d = d.reshape(nb, S5_GPB, S5_STATE, S5_GPB, S5_GROUP)
    return jnp.einsum("jgphi,gh->jgip", d, jnp.eye(S5_GPB, dtype=d.dtype)).reshape(nb * S5_GPB, S5_GROUP, S5_STATE)


def _s5_powers(lr, li, n):
    out = []
    for _ in range(int(math.log2(n))):
        out.append((lr, li))
        lr, li = _cmul(lr, li, lr, li)
    return out


def _s5_local_scan(sr, si, powers, up):
    shift = _shift_up if up else _shift_down
    for k, (pr, pi) in enumerate(powers):
        d = 1 << k
        tr_, ti_ = _cmul(pr, pi, shift(sr, d), shift(si, d))
        sr, si = sr + tr_, si + ti_
    return sr, si


def _dot_hi(a, b, ca=1, cb=0):
    return lax.dot_general(a, b, (((ca,), (cb,)), ((), ())), precision=HI, preferred_element_type=F32)


def _s5_specs(nt, rev):
    tc = S5_TCHUNK

    def at(t):
        return nt - 1 - t if rev else t

    return [
        pl.BlockSpec((tc, LANES), lambda j, b, t: (b * nt + at(t), j)),
        pl.BlockSpec((1, S5_SLANES), lambda j, b, t: (0, j)),
        pl.BlockSpec((1, S5_SLANES), lambda j, b, t: (0, j)),
        pl.BlockSpec((1, LANES, S5_SLANES), lambda j, b, t: (j, 0, 0)),
        pl.BlockSpec((1, LANES, S5_SLANES), lambda j, b, t: (j, 0, 0)),
        pl.BlockSpec((1, S5_SLANES, LANES), lambda j, b, t: (j, 0, 0)),
        pl.BlockSpec((1, S5_SLANES, LANES), lambda j, b, t: (j, 0, 0)),
        pl.BlockSpec((1, LANES), lambda j, b, t: (0, j)),
    ]


def _s5_chunk_states(u, lr, li, b_re, b_im, cr, ci, powers):
    bur, bui = _dot_hi(u, b_re), _dot_hi(u, b_im)
    row = lax.broadcasted_iota(jnp.int32, bur.shape, 0)
    inr, ini = _cmul(lr, li, cr, ci)
    bur = bur + jnp.where(row == 0, inr, 0.0)
    bui = bui + jnp.where(row == 0, ini, 0.0)
    return _s5_local_scan(bur, bui, powers, False)


def _s5_scan_fwd(u, lam_re, lam_im, Bre, Bim, Cre, Cim, dskip, B, name):
    T, Wd = u.shape
    nb = Wd // LANES
    tc = S5_TCHUNK
    nt = T // B // tc
    L = nb * S5_SLANES

    def body(u_ref, lr_ref, li_ref, br_ref, bi_ref, cr_ref, ci_ref, d_ref, y_ref, csr_ref, csi_ref, car_ref, cai_ref):
        @pl.when(pl.program_id(2) == 0)
        def _():
            car_ref[...] = jnp.zeros_like(car_ref)
            cai_ref[...] = jnp.zeros_like(cai_ref)

        csr_ref[0, 0] = car_ref[...]
        csi_ref[0, 0] = cai_ref[...]
        u_ = u_ref[...]
        lr, li = lr_ref[...], li_ref[...]
        sr, si = _s5_chunk_states(u_, lr, li, br_ref[0], bi_ref[0], car_ref[0:1, :], cai_ref[0:1, :],
                                  _s5_powers(lr, li, tc))
        y_ref[...] = _dot_hi(sr, cr_ref[0]) - _dot_hi(si, ci_ref[0]) + d_ref[...] * u_
        row = lax.broadcasted_iota(jnp.int32, sr.shape, 0)
        car_ref[0:1, :] = jnp.sum(jnp.where(row == tc - 1, sr, 0.0), axis=0, keepdims=True)
        cai_ref[0:1, :] = jnp.sum(jnp.where(row == tc - 1, si, 0.0), axis=0, keepdims=True)

    cs_shape = jax.ShapeDtypeStruct((B, nt, SUBLANES, L), F32)
    cs_spec = pl.BlockSpec((1, 1, SUBLANES, S5_SLANES), lambda j, b, t: (b, t, 0, j))
    return pl.pallas_call(
        body, name=name, out_shape=[jax.ShapeDtypeStruct((T, Wd), F32), cs_shape, cs_shape],
        grid=(nb, B, nt), in_specs=_s5_specs(nt, False),
        out_specs=[pl.BlockSpec((tc, LANES), lambda j, b, t: (b * nt + t, j)), cs_spec, cs_spec],
        scratch_shapes=[pltpu.VMEM((SUBLANES, S5_SLANES), F32)] * 2,
        compiler_params=_cparams(("parallel", "parallel", "arbitrary")),
    )(u, lam_re, lam_im, Bre, Bim, Cre, Cim, dskip)


def _s5_scan_bwd(u, lam_re, lam_im, Bre, Bim, Cre, Cim, dskip, csr, csi, dy, B, name):
    T, Wd = u.shape
    nb = Wd // LANES
    tc = S5_TCHUNK
    nt = T // B // tc
    L = nb * S5_SLANES

    def body(u_ref, lr_ref, li_ref, br_ref, bi_ref, cr_ref, ci_ref, d_ref, csr_ref, csi_ref, dy_ref,
             du_ref, dlr_ref, dli_ref, dbr_ref, dbi_ref, dcr_ref, dci_ref, dd_ref, gr_ref, gi_ref):
        first = jnp.logical_and(pl.program_id(1) == 0, pl.program_id(2) == 0)

        @pl.when(pl.program_id(2) == 0)
        def _():
            gr_ref[...] = jnp.zeros_like(gr_ref)
            gi_ref[...] = jnp.zeros_like(gi_ref)

        @pl.when(first)
        def _():
            for r in (dlr_ref, dli_ref, dbr_ref, dbi_ref, dcr_ref, dci_ref, dd_ref):
                r[...] = jnp.zeros_like(r)

        u_, dy_ = u_ref[...], dy_ref[...]
        lr, li = lr_ref[...], li_ref[...]
        powers = _s5_powers(lr, li, tc)
        c_in_r, c_in_i = csr_ref[0, 0, 0:1, :], csi_ref[0, 0, 0:1, :]
        sr, si = _s5_chunk_states(u_, lr, li, br_ref[0], bi_ref[0], c_in_r, c_in_i, powers)
        dcr_ref[0] += _dot_hi(sr, dy_, 0, 0)
        dci_ref[0] -= _dot_hi(si, dy_, 0, 0)
        dd_ref[...] += jnp.sum(dy_ * u_, axis=0, keepdims=True)
        row = lax.broadcasted_iota(jnp.int32, sr.shape, 0)
        gr = _dot_hi(dy_, cr_ref[0], 1, 1)
        gi = -_dot_hi(dy_, ci_ref[0], 1, 1)
        inr, ini = _cmul(lr, -li, gr_ref[0:1, :], gi_ref[0:1, :])
        gr = gr + jnp.where(row == tc - 1, inr, 0.0)
        gi = gi + jnp.where(row == tc - 1, ini, 0.0)
        gr, gi = _s5_local_scan(gr, gi, [(pr, -pi) for pr, pi in powers], True)
        gr_ref[0:1, :] = jnp.sum(jnp.where(row == 0, gr, 0.0), axis=0, keepdims=True)
        gi_ref[0:1, :] = jnp.sum(jnp.where(row == 0, gi, 0.0), axis=0, keepdims=True)
        pr_ = _shift_down(sr, 1) + jnp.where(row == 0, c_in_r, 0.0)
        pi_ = _shift_down(si, 1) + jnp.where(row == 0, c_in_i, 0.0)
        dlr_ref[...] += jnp.sum(gr * pr_ + gi * pi_, axis=0, keepdims=True)
        dli_ref[...] += jnp.sum(gi * pr_ - gr * pi_, axis=0, keepdims=True)
        dbr_ref[0] += _dot_hi(u_, gr, 0, 0)
        dbi_ref[0] += _dot_hi(u_, gi, 0, 0)
        du_ref[...] = dy_ * d_ref[...] + _dot_hi(gr, br_ref[0], 1, 1) + _dot_hi(gi, bi_ref[0], 1, 1)

    cs_spec = pl.BlockSpec((1, 1, SUBLANES, S5_SLANES), lambda j, b, t: (b, nt - 1 - t, 0, j))
    rrow = pl.BlockSpec((tc, LANES), lambda j, b, t: (b * nt + nt - 1 - t, j))
    lam_spec = pl.BlockSpec((1, S5_SLANES), lambda j, b, t: (0, j))
    b_spec = pl.BlockSpec((1, LANES, S5_SLANES), lambda j, b, t: (j, 0, 0))
    c_spec = pl.BlockSpec((1, S5_SLANES, LANES), lambda j, b, t: (j, 0, 0))
    return pl.pallas_call(
        body, name=name,
        out_shape=[jax.ShapeDtypeStruct((T, Wd), F32)] + [jax.ShapeDtypeStruct((1, L), F32)] * 2
        + [jax.ShapeDtypeStruct((nb, LANES, S5_SLANES), F32)] * 2
        + [jax.ShapeDtypeStruct((nb, S5_SLANES, LANES), F32)] * 2 + [jax.ShapeDtypeStruct((1, Wd), F32)],
        grid=(nb, B, nt), in_specs=_s5_specs(nt, True) + [cs_spec, cs_spec, rrow],
        out_specs=[rrow, lam_spec, lam_spec, b_spec, b_spec, c_spec, c_spec,
                   pl.BlockSpec((1, LANES), lambda j, b, t: (0, j))],
        scratch_shapes=[pltpu.VMEM((SUBLANES, S5_SLANES), F32)] * 2,
        compiler_params=_cparams(("arbitrary", "arbitrary", "arbitrary")),
    )(u, lam_re, lam_im, Bre, Bim, Cre, Cim, dskip, csr, csi, dy)


def _f_gelu(y):
    return 0.5 * y * (1.0 + jnp.tanh(math.sqrt(2.0 / math.pi) * (y + 0.044715 * (y * y * y))))


def _f_glu(pv, pg, bv, bg):
    return (pv + bv) * jax.nn.sigmoid(pg + bg)


def _s5_params(p):
    prep_in = (p["a_re"], p["a_im"], p["log_step"][:, None], jnp.swapaxes(p["b_re"], 1, 2), jnp.swapaxes(p["b_im"], 1, 2))
    return prep_in


def _s5_fwd(u, p, glu_w, B, tag):
    Wd = u.shape[1]
    prep_in = _s5_params(p)
    lbr, lbi, bbr, bbi = _s5_prep_fwd(prep_in, f"{tag}_prep")
    ops = (lbr.reshape(1, -1), lbi.reshape(1, -1), _s5_blockdiag_in(bbr), _s5_blockdiag_in(bbi),
           _s5_blockdiag_out(p["c_re"]), _s5_blockdiag_out(p["c_im"]), p["d"][None])
    y, csr, csi = _s5_scan_fwd(u, *ops, B, f"{tag}_scan")
    (yg,) = _ew_fwd(_f_gelu, [y], [], [], [(Wd, BF16, False)], name=f"{tag}_gelu")
    pj = _mm(yg, glu_w, name=f"{tag}_glu")
    bv, bg = p["glu_b"][None, :Wd], p["glu_b"][None, Wd:]
    (out,) = _ew_fwd(_f_glu, [(pj, Wd, 0), (pj, Wd, 1)], [], [bv, bg], [(Wd, BF16, False)], name=f"{tag}_gate")
    return out, (u, prep_in, ops, csr, csi, y, yg, pj, bv, bg)


def _s5_bwd(dout, saved, p, glu_w, B, tag):
    u, prep_in, ops, csr, csi, y, yg, pj, bv, bg = saved
    Wd = u.shape[1]
    dpv, dpg, dbv, dbg = _ew_bwd(_f_glu, [(pj, Wd, 0), (pj, Wd, 1)], [], [bv, bg], [dout], [True] * 4,
                                 row_dtypes=[BF16, BF16], name=f"{tag}_gate_bwd")
    dpj = jnp.concatenate([dpv, dpg], axis=1)
    d_glu_w = _mm(yg, dpj, ta=True, out_dtype=BF16, name=f"{tag}_glu_dw")
    dyg = _mm(dpj, glu_w, tb=True, name=f"{tag}_glu_dx")
    (dy,) = _ew_bwd(_f_gelu, [y], [], [], [dyg], [True], name=f"{tag}_gelu_bwd")
    du, dlr, dli, dBr, dBi, dCr, dCi, dd = _s5_scan_bwd(u, *ops, csr, csi, dy, B, f"{tag}_scan_bwd")
    G = p["a_re"].shape[0]
    cts = (dlr.reshape(G, S5_STATE), dli.reshape(G, S5_STATE), _s5_blockdiag_in_t(dBr), _s5_blockdiag_in_t(dBi))
    da_re, da_im, dls, db_re, db_im = _s5_prep_bwd(prep_in, cts, f"{tag}_prep_bwd")
    small = dict(a_re=da_re, a_im=da_im, log_step=dls[:, 0], b_re=jnp.swapaxes(db_re, 1, 2),
                 b_im=jnp.swapaxes(db_im, 1, 2), c_re=_s5_blockdiag_out_t(dCr), c_im=_s5_blockdiag_out_t(dCi),
                 d=dd[0], glu_b=jnp.concatenate([dbv[0], dbg[0]]))
    return du, d_glu_w, small


DIL_GW = DIL_HPG * DIL_DIM


def _f_qknorm(scale, x, w):
    n = x.shape[1]
    ri = lax.broadcasted_iota(jnp.int32, (n, n), 0)
    ci = lax.broadcasted_iota(jnp.int32, (n, n), 1)
    seg = jnp.where(ri // DIL_DIM == ci // DIL_DIM, 1.0 / DIL_DIM, 0.0)
    ms = jnp.dot(x * x, seg, precision=HI, preferred_element_type=F32)
    return x * lax.rsqrt(ms + EPS) * (w * scale)


def _dil_to_blocks(x, B):
    T = x.shape[0]
    S = T // B
    parts = []
    for gi, (_, dil) in enumerate(DIL_PAIRS):
        xg = x[:, gi * DIL_GW:(gi + 1) * DIL_GW].reshape(B, S // dil, dil, DIL_HPG, DIL_DIM)
        parts.append(xg.transpose(0, 2, 3, 1, 4).reshape(-1, DIL_DIM))
    return jnp.concatenate(parts, axis=0)


def _dil_from_blocks(y, B):
    n = y.shape[0] // len(DIL_PAIRS)
    T = n // DIL_HPG
    S = T // B
    parts = []
    for gi, (_, dil) in enumerate(DIL_PAIRS):
        yg = y[gi * n:(gi + 1) * n].reshape(B, dil, DIL_HPG, S // dil, DIL_DIM)
        parts.append(yg.transpose(0, 3, 1, 2, 4).reshape(T, DIL_GW))
    return jnp.concatenate(parts, axis=1)


def _dil_block_fn(slope, has_prev, q, kp, kc, vp, vc):
    n = q.shape[0]
    qi = lax.broadcasted_iota(jnp.int32, (n, n), 0)
    kj = lax.broadcasted_iota(jnp.int32, (n, n), 1)
    dist = (qi - kj).astype(F32)
    sc = _dot_hi(q, kc, 1, 1) - slope * dist
    sp = _dot_hi(q, kp, 1, 1) - slope * (dist + n)
    sc = jnp.where(qi >= kj, sc, -1e30)
    sp = jnp.where(jnp.logical_and(kj >= qi, has_prev), sp, -1e30)
    m = lax.stop_gradient(jnp.maximum(jnp.max(sc, axis=1, keepdims=True), jnp.max(sp, axis=1, keepdims=True)))
    pc = jnp.exp(sc - m)
    pp = jnp.exp(sp - m)
    l = jnp.sum(pc, axis=1, keepdims=True) + jnp.sum(pp, axis=1, keepdims=True)
    o = (_dot_hi(pp, vp) + _dot_hi(pc, vc)) / l
    return o, jnp.broadcast_to(m + jnp.log(l), o.shape)


def _dil_scalars(m, B, S):
    ipg = B * S * DIL_HPG // DIL_BLK
    grp = m // ipg
    r = m % ipg
    sel = lambda vals: jnp.where(grp == 0, vals[0], jnp.where(grp == 1, vals[1], vals[2]))
    nbk = sel([S // d // DIL_BLK for _, d in DIL_PAIRS])
    dil = sel([d for _, d in DIL_PAIRS])
    n = r % nbk
    head = grp * DIL_HPG + (r // nbk) % DIL_HPG
    nh = len(DIL_PAIRS) * DIL_HPG
    slope = jnp.exp(-math.log(2.0) * ALIBI_MAX / nh * (head + 1).astype(F32))
    return slope * dil.astype(F32), n > 0


def _dil_specs():
    cur = pl.BlockSpec((DIL_BLK, DIL_DIM), lambda m: (m, 0))
    prev = pl.BlockSpec((DIL_BLK, DIL_DIM), lambda m: (jnp.maximum(m - 1, 0), 0))
    return [cur, prev, cur, prev, cur]


def _dil_attn_fwd(qb, kb, vb, B, S, name):
    R_ = qb.shape[0]

    def body(q_ref, kp_ref, kc_ref, vp_ref, vc_ref, o_ref, l_ref):
        slope, has_prev = _dil_scalars(pl.program_id(0), B, S)
        o, l = _dil_block_fn(slope, has_prev, q_ref[...], kp_ref[...], kc_ref[...], vp_ref[...], vc_ref[...])
        o_ref[...] = o
        l_ref[...] = l

    cur = pl.BlockSpec((DIL_BLK, DIL_DIM), lambda m: (m, 0))
    return pl.pallas_call(
        body, name=name, out_shape=[jax.ShapeDtypeStruct((R_, DIL_DIM), F32)] * 2, grid=(R_ // DIL_BLK,),
        in_specs=_dil_specs(), out_specs=[cur, cur], compiler_params=_cparams(("parallel",)),
    )(qb, kb, kb, vb, vb)


def _dil_attn_bwd(qb, kb, vb, do, dl, B, S, name):
    R_ = qb.shape[0]

    def body(q_ref, kp_ref, kc_ref, vp_ref, vc_ref, do_ref, dl_ref, *outs):
        slope, has_prev = _dil_scalars(pl.program_id(0), B, S)
        _, vjp = jax.vjp(functools.partial(_dil_block_fn, slope, has_prev),
                         q_ref[...], kp_ref[...], kc_ref[...], vp_ref[...], vc_ref[...])
        for o_ref, g in zip(outs, vjp((do_ref[...], dl_ref[...]))):
            o_ref[...] = g

    cur = pl.BlockSpec((DIL_BLK, DIL_DIM), lambda m: (m, 0))
    return pl.pallas_call(
        body, name=name, out_shape=[jax.ShapeDtypeStruct((R_, DIL_DIM), F32)] * 5, grid=(R_ // DIL_BLK,),
        in_specs=_dil_specs() + [cur, cur], out_specs=[cur] * 5, compiler_params=_cparams(("parallel",)),
    )(qb, kb, kb, vb, vb, do, dl)


def _f_dil_merge(o0, o1, o2, l0, l1, l2):
    m = lax.stop_gradient(jnp.maximum(jnp.maximum(l0, l1), l2))
    e0, e1, e2 = jnp.exp(l0 - m), jnp.exp(l1 - m), jnp.exp(l2 - m)
    return (e0 * o0 + e1 * o1 + e2 * o2) / (e0 + e1 + e2)


def _dil_fwd(qkv, q_norm, k_norm, B, tag):
    T = qkv.shape[0]
    S = T // B
    Wd = qkv.shape[1] // 3
    nblk = Wd // LANES
    (qn,) = _ew_fwd(functools.partial(_f_qknorm, DIL_DIM ** -0.5), [(qkv, LANES, 0)], [], [q_norm],
                    [(Wd, F32, True)], ncb=nblk, name=f"{tag}_qnorm")
    (kn,) = _ew_fwd(functools.partial(_f_qknorm, 1.0), [(qkv, LANES, nblk)], [], [k_norm],
                    [(Wd, F32, True)], ncb=nblk, name=f"{tag}_knorm")
    qb, kb, vb = _dil_to_blocks(qn, B), _dil_to_blocks(kn, B), _dil_to_blocks(qkv[:, 2 * Wd:], B)
    ob, lb = _dil_attn_fwd(qb, kb, vb, B, S, f"{tag}_attn")
    o, l = _dil_from_blocks(ob, B), _dil_from_blocks(lb, B)
    gw = DIL_GW
    rows = [(o, gw, 0), (o, gw, 1), (o, gw, 2), (l, gw, 0), (l, gw, 1), (l, gw, 2)]
    (y,) = _ew_fwd(_f_dil_merge, rows, [], [], [(gw, BF16, False)], name=f"{tag}_merge")
    return y, (qkv, qb, kb, vb, o, l)


def _dil_bwd(dy, saved, q_norm, k_norm, B, tag):
    qkv, qb, kb, vb, o, l = saved
    T = qkv.shape[0]
    S = T // B
    Wd = qkv.shape[1] // 3
    nblk = Wd // LANES
    gw = DIL_GW
    rows = [(o, gw, 0), (o, gw, 1), (o, gw, 2), (l, gw, 0), (l, gw, 1), (l, gw, 2)]
    g = _ew_bwd(_f_dil_merge, rows, [], [], [dy], [True] * 6, name=f"{tag}_merge_bwd")
    do = _dil_to_blocks(jnp.concatenate(g[:3], axis=1), B)
    dl = _dil_to_blocks(jnp.concatenate(g[3:], axis=1), B)
    dq, dkp, dkc, dvp, dvc = _dil_attn_bwd(qb, kb, vb, do, dl, B, S, f"{tag}_attn_bwd")
    nxt = lambda t: jnp.concatenate([t[DIL_BLK:], jnp.zeros((DIL_BLK, DIL_DIM), t.dtype)], axis=0)
    dqn = _dil_from_blocks(dq, B)
    dkn = _dil_from_blocks(dkc + nxt(dkp), B)
    dv = _dil_from_blocks(dvc + nxt(dvp), B)
    dq_raw, dqw = _ew_bwd(functools.partial(_f_qknorm, DIL_DIM ** -0.5), [(qkv, LANES, 0)], [], [q_norm],
                          [(dqn, LANES, 0)], [True, True], ncb=nblk, name=f"{tag}_qnorm_bwd")
    dk_raw, dkw = _ew_bwd(functools.partial(_f_qknorm, 1.0), [(qkv, LANES, nblk)], [], [k_norm],
                          [(dkn, LANES, 0)], [True, True], ncb=nblk, name=f"{tag}_knorm_bwd")
    return jnp.concatenate([dq_raw, dk_raw, dv], axis=1), dqw, dkw


MESH_ID = pl.DeviceIdType.MESH
ANY = pl.BlockSpec(memory_space=pl.ANY)


def _my_place():
    return lax.axis_index("x"), lax.axis_index("y"), lax.axis_index("c")


def _flat_index(px, py, pc):
    return 4 * px + 2 * py + pc


def _all_gather(x, name):
    R_, C = x.shape

    def body(x_ref, out_ref, send_sems, recv_sems, local_sem):
        mx, my, mc = _my_place()
        me, sibling = (mx, my, mc), (mx, my, 1 - mc)
        chips = [(1 - mx, my), (mx, 1 - my), (1 - mx, 1 - my)]

        def rows(p):
            return out_ref.at[_flat_index(*p)]

        def copy(k, block, to, src=None):
            return pltpu.make_async_remote_copy(
                src_ref=rows(block) if src is None else src, dst_ref=rows(block),
                send_sem=send_sems.at[k], recv_sem=recv_sems.at[k], device_id=to, device_id_type=MESH_ID)

        mine = pltpu.make_async_copy(x_ref, rows(me), local_sem)
        mine.start()
        first = [copy(0, me, sibling, src=x_ref)]
        first += [copy(1 + j, me, (*chip, mc), src=x_ref) for j, chip in enumerate(chips)]
        for cp in first:
            cp.start()
        passed = [copy(4 + j, (*chip, mc), sibling) for j, chip in enumerate(chips)]
        for j, chip in enumerate(chips):
            copy(1 + j, (*chip, mc), me).wait_recv()
            passed[j].start()
        copy(0, sibling, me).wait_recv()
        for j, chip in enumerate(chips):
            copy(4 + j, (*chip, 1 - mc), me).wait_recv()
        for cp in first + passed:
            cp.wait_send()
        mine.wait()

    return pl.pallas_call(
        body, name=name, out_shape=jax.ShapeDtypeStruct((N_DEV, R_, C), x.dtype),
        in_specs=[ANY], out_specs=ANY,
        scratch_shapes=[pltpu.SemaphoreType.DMA((7,)), pltpu.SemaphoreType.DMA((7,)), pltpu.SemaphoreType.DMA],
        compiler_params=pltpu.CompilerParams(has_side_effects=True),
    )(x)


def _exchange_pieces(x, name):
    _, R_, C = x.shape

    def body(x_ref, out_ref, send_sems, recv_sems, local_sem):
        mx, my, mc = _my_place()
        me = _flat_index(mx, my, mc)
        flips = [(fx, fy, fc) for fx in (0, 1) for fy in (0, 1) for fc in (0, 1)][1:]
        peers = [(1 - mx if fx else mx, 1 - my if fy else my, 1 - mc if fc else mc) for fx, fy, fc in flips]
        mine = pltpu.make_async_copy(x_ref.at[me], out_ref.at[me], local_sem)
        mine.start()
        copies = [pltpu.make_async_remote_copy(
            src_ref=x_ref.at[_flat_index(*p)], dst_ref=out_ref.at[me],
            send_sem=send_sems.at[k], recv_sem=recv_sems.at[k], device_id=p, device_id_type=MESH_ID)
            for k, p in enumerate(peers)]
        for cp in copies:
            cp.start()
        for k, p in enumerate(peers):
            pltpu.make_async_remote_copy(
                src_ref=x_ref.at[me], dst_ref=out_ref.at[_flat_index(*p)],
                send_sem=send_sems.at[k], recv_sem=recv_sems.at[k], device_id=p, device_id_type=MESH_ID).wait_recv()
        for cp in copies:
            cp.wait_send()
        mine.wait()

    return pl.pallas_call(
        body, name=name, out_shape=jax.ShapeDtypeStruct(x.shape, x.dtype),
        in_specs=[ANY], out_specs=ANY,
        scratch_shapes=[pltpu.SemaphoreType.DMA((7,)), pltpu.SemaphoreType.DMA((7,)), pltpu.SemaphoreType.DMA],
        compiler_params=pltpu.CompilerParams(has_side_effects=True),
    )(x)


def _sum0(x, name, tr=256):
    n, R_, C = x.shape
    tr = _pick(R_, (tr, 128, 64, 32, 16, 8))

    def body(x_ref, o_ref):
        acc = x_ref[0].astype(F32)
        for k in range(1, n):
            acc = acc + x_ref[k].astype(F32)
        o_ref[...] = acc

    return pl.pallas_call(
        body, name=name, out_shape=jax.ShapeDtypeStruct((R_, C), F32), grid=(R_ // tr,),
        in_specs=[pl.BlockSpec((n, tr, C), lambda i: (0, i, 0))], out_specs=pl.BlockSpec((tr, C), lambda i: (i, 0)),
        compiler_params=_cparams(("parallel",)),
    )(x)


PACK_ROWS = 256


def _pack(arrs, dtype, width):
    flat = jnp.concatenate([a.astype(dtype).reshape(-1) for a in arrs])
    quantum = width * PACK_ROWS
    pad = (-flat.shape[0]) % quantum
    if pad:
        flat = jnp.concatenate([flat, jnp.zeros((pad,), dtype)])
    return flat.reshape(-1, width)


def _unpack(flat, shapes):
    out, off = [], 0
    for s in shapes:
        n = int(np.prod(s))
        out.append(flat[..., off:off + n].reshape(flat.shape[:-1] + tuple(s)))
        off += n
    return out


def _ada_fwd(c_all, ada_w, bias, name):
    M, D = c_all.shape
    n = ada_w.shape[1]
    tn = _pick(n, (768, 512, 256, 128))

    def body(c_ref, w_ref, b_ref, o_ref):
        c_ = c_ref[...]
        a = (c_ * jax.nn.sigmoid(c_)).astype(BF16)
        o_ref[...] = jnp.dot(a, w_ref[...].astype(BF16), preferred_element_type=F32) + b_ref[...]

    return pl.pallas_call(
        body, name=name, out_shape=jax.ShapeDtypeStruct((M, n), F32), grid=(n // tn,),
        in_specs=[pl.BlockSpec((M, D), lambda j: (0, 0)), pl.BlockSpec((D, tn), lambda j: (0, j)),
                  pl.BlockSpec((1, tn), lambda j: (0, j))],
        out_specs=pl.BlockSpec((M, tn), lambda j: (0, j)), compiler_params=_cparams(("parallel",)),
    )(c_all, ada_w, bias)


def _ada_bwd(c_all, dmod, name):
    M, D = c_all.shape
    n = dmod.shape[1]
    tn = _pick(n, (768, 512, 256, 128))

    def body(c_ref, d_ref, o_ref):
        c_ = c_ref[...]
        a = (c_ * jax.nn.sigmoid(c_)).astype(BF16)
        o_ref[...] = lax.dot_general(a, d_ref[...].astype(BF16), (((0,), (0,)), ((), ())), preferred_element_type=F32)

    return pl.pallas_call(
        body, name=name, out_shape=jax.ShapeDtypeStruct((D, n), F32), grid=(n // tn,),
        in_specs=[pl.BlockSpec((M, D), lambda j: (0, 0)), pl.BlockSpec((M, tn), lambda j: (0, j))],
        out_specs=pl.BlockSpec((D, tn), lambda j: (0, j)), compiler_params=_cparams(("parallel",)),
    )(c_all, dmod)


def _loss_head(y, target, name, tr=256):
    T, D = y.shape

    def body(y_ref, t_ref, l_ref, d_ref):
        e = y_ref[...] - t_ref[...]
        d_ref[...] = e * (1.0 / D)
        part = jnp.sum(jnp.sum(e * e, axis=1, keepdims=True), axis=0, keepdims=True) * (0.5 / D)

        @pl.when(pl.program_id(0) == 0)
        def _():
            l_ref[...] = jnp.zeros_like(l_ref)

        l_ref[...] += jnp.broadcast_to(part, l_ref.shape)

    row = pl.BlockSpec((tr, D), lambda i: (i, 0))
    return pl.pallas_call(
        body, name=name, out_shape=[jax.ShapeDtypeStruct((1, LANES), F32), jax.ShapeDtypeStruct((T, D), F32)],
        grid=(T // tr,), in_specs=[row, row], out_specs=[pl.BlockSpec((1, LANES), lambda i: (0, 0)), row],
        compiler_params=_cparams(("arbitrary",)),
    )(y, target)


def _adamw(w, g, m, v, name):
    shape = w.shape
    C = shape[-1]
    R_ = int(np.prod(shape[:-1]))
    w2, g2, m2, v2 = [a.reshape(R_, C) for a in (w, g, m, v)]
    tr = _pick(R_, (256, 128, 64, 32, 16, 8)) if R_ > 8 else R_
    c1 = 1.0 / (1.0 - ADAM_B1 ** ADAM_STEP)
    c2 = 1.0 / (1.0 - ADAM_B2 ** ADAM_STEP)

    def body(w_ref, g_ref, m_ref, v_ref, d_ref, nm_ref, nv_ref):
        g_ = g_ref[...]
        nm = ADAM_B1 * m_ref[...] + (1.0 - ADAM_B1) * g_
        nv = ADAM_B2 * v_ref[...] + (1.0 - ADAM_B2) * (g_ * g_)
        d_ref[...] = -ADAM_LR * ((nm * c1) / (jnp.sqrt(nv * c2) + ADAM_EPS) + ADAM_WD * w_ref[...])
        nm_ref[...] = nm
        nv_ref[...] = nv

    spec = pl.BlockSpec((tr, C), lambda i: (i, 0))
    outs = pl.pallas_call(
        body, name=name, out_shape=[jax.ShapeDtypeStruct((R_, C), F32)] * 3, grid=(R_ // tr,),
        in_specs=[spec] * 4, out_specs=[spec] * 3, compiler_params=_cparams(("parallel",)),
    )(w2, g2, m2, v2)
    return [o.reshape(shape) for o in outs]


GDN_W = GDN_HEADS * GDN_DIM
IN_SPLITS = (3 * GDN_W, GDN_W, GDN_HEADS, GDN_HEADS, 768, 3 * 768, None)
IN_PAD_GATES = LANES - 2 * GDN_HEADS


def _f_merge(pa, pb, pc, ga, gb, gc):
    return jax.nn.sigmoid(ga) * pa + jax.nn.sigmoid(gb) * pb + jax.nn.sigmoid(gc) * pc


def _f_id_rmsmod(x, sh, sc, g):
    return x, _rms(x, g) * (1.0 + sc) + sh


def _in_layout(D):
    widths = [3 * GDN_W, GDN_W, LANES, 768, 3 * 768, 3 * D]
    offs = np.concatenate([[0], np.cumsum(widths)]).tolist()
    total = -(-offs[-1] // 768) * 768
    return offs, total


def _pad_w_in(w_in):
    D = w_in.shape[0]
    offs, total = _in_layout(D)
    cut = 4 * GDN_W + 2 * GDN_HEADS
    return jnp.concatenate([w_in[:, :cut], jnp.zeros((D, IN_PAD_GATES), w_in.dtype), w_in[:, cut:],
                            jnp.zeros((D, total - offs[-1]), w_in.dtype)], axis=1)


def _unpad_w_in(d):
    D = d.shape[0]
    offs, _ = _in_layout(D)
    cut = 4 * GDN_W + 2 * GDN_HEADS
    return jnp.concatenate([d[:, :cut], d[:, cut + IN_PAD_GATES:offs[-1]]], axis=1)


def _layer_fwd(x0, mod, W, sm, B, tag):
    T, D = x0.shape
    S = T // B
    sh1, sc1, g1, sh2, sc2, g2, sh3, sc3, g3 = mod
    n1, nm, n3 = sm["norm_ffn1"][None], sm["norm_mix"][None], sm["norm_ffn2"][None]
    (h1,) = _ew_fwd(_f_rmsmod, [x0], [sh1, sc1], [n1], [(D, BF16, False)], seq=S, name=f"{tag}_norm1")
    y1, sv1 = _ffn_fwd(h1, W["ffn1_w13"], W["ffn1_w2"], f"{tag}_ffn1")
    x1, h2 = _ew_fwd(functools.partial(_f_res_rmsmod, FFN_RES), [x0, y1], [g1, sh2, sc2], [nm],
                     [(D, F32, False), (D, BF16, False)], seq=S, name=f"{tag}_res1")
    P = _mm(h2, W["w_in"], name=f"{tag}_in")
    offs, _ = _in_layout(D)
    qkv_a, z, ba, u, qkv_c, gl = [P[:, offs[i]:offs[i + 1]] for i in range(6)]
    lane8 = lambda v: jnp.pad(v, (GDN_HEADS, LANES - 2 * GDN_HEADS))[None]
    gate_params = (lane8(sm["gdn_a_log"]), lane8(sm["gdn_dt_bias"]))
    out_norm = sm["gdn_out_norm"][None]
    y_a, sva = _gdn_fwd(qkv_a, z, ba, sm["gdn_conv"], gate_params, out_norm, B, f"{tag}_gdn")
    s5p = {k[3:]: v for k, v in sm.items() if k.startswith("s5_")}
    y_b, svb = _s5_fwd(u, s5p, W["s5_glu_w"], B, f"{tag}_s5")
    qn2, kn2 = jnp.tile(sm["dil_q_norm"], 2)[None], jnp.tile(sm["dil_k_norm"], 2)[None]
    y_c, svc = _dil_fwd(qkv_c, qn2, kn2, B, f"{tag}_dil")
    pa = _mm(y_a, W["w_branch_a"], name=f"{tag}_pa")
    pb = _mm(y_b, W["w_branch_b"], name=f"{tag}_pb")
    pc = _mm(y_c, W["w_branch_c"], name=f"{tag}_pc")
    mrows = [pa, pb, pc, (gl, D, 0), (gl, D, 1), (gl, D, 2)]
    (merged,) = _ew_fwd(_f_merge, mrows, [], [], [(D, BF16, False)], tr=128, name=f"{tag}_merge")
    mo = _mm(merged, W["w_out"], name=f"{tag}_out")
    x2, h3 = _ew_fwd(functools.partial(_f_res_rmsmod, 1.0), [x1, mo], [g2, sh3, sc3], [n3],
                     [(D, F32, False), (D, BF16, False)], seq=S, name=f"{tag}_res2")
    y3, sv3 = _ffn_fwd(h3, W["ffn2_w13"], W["ffn2_w2"], f"{tag}_ffn2")
    (x3,) = _ew_fwd(functools.partial(_f_res, FFN_RES), [x2, y3], [g3], [], [(D, F32, False)], seq=S,
                    name=f"{tag}_res3")
    saved = dict(x0=x0, x1=x1, x2=x2, y1=y1, y3=y3, mo=mo, h2=h2, sv1=sv1, sv3=sv3, sva=sva, svb=svb, svc=svc,
                 y_a=y_a, y_b=y_b, y_c=y_c, mrows=mrows, merged=merged, gate_params=gate_params, out_norm=out_norm,
                 s5p=s5p, qn2=qn2, kn2=kn2)
    return x3, saved


def _layer_bwd(dx3, sv, mod, W, sm, B, tag):
    T, D = dx3.shape
    S = T // B
    sh1, sc1, g1, sh2, sc2, g2, sh3, sc3, g3 = mod
    n1, nm, n3 = sm["norm_ffn1"][None], sm["norm_mix"][None], sm["norm_ffn2"][None]
    big, small = {}, {}
    dx2, dy3, dg3 = _ew_bwd(functools.partial(_f_res, FFN_RES), [sv["x2"], sv["y3"]], [g3], [], [dx3], [True] * 3,
                            seq=S, row_dtypes=[F32, BF16], name=f"{tag}_res3_bwd")
    dh3, big["ffn2_w13"], big["ffn2_w2"] = _ffn_bwd(dy3, sv["sv3"], W["ffn2_w13"], W["ffn2_w2"], f"{tag}_ffn2")
    dx1, dmo, dg2, dsh3, dsc3, dn3 = _ew_bwd(
        functools.partial(_f_res_rmsmod, 1.0), [sv["x1"], sv["mo"]], [g2, sh3, sc3], [n3], [dx2, dh3], [True] * 6,
        seq=S, row_dtypes=[F32, BF16], name=f"{tag}_res2_bwd")
    big["w_out"] = _mm(sv["merged"], dmo, ta=True, out_dtype=BF16, name=f"{tag}_out_dw")
    dmerged = _mm(dmo, W["w_out"], tb=True, name=f"{tag}_out_dx")
    dpa, dpb, dpc, dga, dgb, dgc = _ew_bwd(_f_merge, sv["mrows"], [], [], [dmerged], [True] * 6, tr=128,
                                           row_dtypes=[BF16] * 6, name=f"{tag}_merge_bwd")
    big["w_branch_a"] = _mm(sv["y_a"], dpa, ta=True, out_dtype=BF16, name=f"{tag}_pa_dw")
    big["w_branch_b"] = _mm(sv["y_b"], dpb, ta=True, out_dtype=BF16, name=f"{tag}_pb_dw")
    big["w_branch_c"] = _mm(sv["y_c"], dpc, ta=True, out_dtype=BF16, name=f"{tag}_pc_dw")
    dy_a = _mm(dpa, W["w_branch_a"], tb=True, name=f"{tag}_pa_dx")
    dy_b = _mm(dpb, W["w_branch_b"], tb=True, name=f"{tag}_pb_dx")
    dy_c = _mm(dpc, W["w_branch_c"], tb=True, name=f"{tag}_pc_dx")
    dqkv_a, dz, dba, gdn_small = _gdn_bwd(dy_a, sv["sva"], sm["gdn_conv"], sv["gate_params"], sv["out_norm"], B,
                                          f"{tag}_gdn")
    du, big["s5_glu_w"], s5_small = _s5_bwd(dy_b, sv["svb"], sv["s5p"], W["s5_glu_w"], B, f"{tag}_s5")
    dqkv_c, dqw, dkw = _dil_bwd(dy_c, sv["svc"], sv["qn2"], sv["kn2"], B, f"{tag}_dil")
    offs, total = _in_layout(D)
    dP = jnp.concatenate([t.astype(BF16) for t in (dqkv_a, dz, dba, du, dqkv_c, dga, dgb, dgc)]
                         + [jnp.zeros((T, total - offs[-1]), BF16)], axis=1)
    big["w_in"] = _mm(sv["h2"], dP, ta=True, out_dtype=BF16, name=f"{tag}_in_dw")
    dh2 = _mm(dP, W["w_in"], tb=True, name=f"{tag}_in_dx")
    dx0a, dy1, dg1, dsh2, dsc2, dnm = _ew_bwd(
        functools.partial(_f_res_rmsmod, FFN_RES), [sv["x0"], sv["y1"]], [g1, sh2, sc2], [nm], [dx1, dh2], [True] * 6,
        seq=S, row_dtypes=[F32, BF16], name=f"{tag}_res1_bwd")
    dh1, big["ffn1_w13"], big["ffn1_w2"] = _ffn_bwd(dy1, sv["sv1"], W["ffn1_w13"], W["ffn1_w2"], f"{tag}_ffn1")
    dx0, dsh1, dsc1, dn1 = _ew_bwd(_f_id_rmsmod, [sv["x0"]], [sh1, sc1], [n1], [dx0a, dh1], [True] * 4, seq=S,
                                   name=f"{tag}_norm1_bwd")
    half = DIL_DIM
    small.update(norm_ffn1=dn1[0], norm_mix=dnm[0], norm_ffn2=dn3[0], gdn_conv=gdn_small["conv"],
                 gdn_a_log=gdn_small["a_log"][0, GDN_HEADS:2 * GDN_HEADS],
                 gdn_dt_bias=gdn_small["dt_bias"][0, GDN_HEADS:2 * GDN_HEADS], gdn_out_norm=gdn_small["out_norm"][0],
                 dil_q_norm=dqw[0, :half] + dqw[0, half:], dil_k_norm=dkw[0, :half] + dkw[0, half:])
    small.update({"s5_" + k: v for k, v in s5_small.items()})
    dmod = [dsh1, dsc1, dg1, dsh2, dsc2, dg2, dsh3, dsc3, dg3]
    return dx0, big, small, dmod


WEIGHTS = ['ada_w', 'ada_b', 'norm_ffn1', 'ffn1_w1', 'ffn1_w3', 'ffn1_w2', 'norm_mix', 'w_in', 'gdn_conv', 'gdn_a_log',
           'gdn_dt_bias', 'gdn_out_norm', 's5_a_re', 's5_a_im', 's5_b_re', 's5_b_im', 's5_c_re', 's5_c_im', 's5_d',
           's5_log_step', 's5_glu_w', 's5_glu_b', 'dil_q_norm', 'dil_k_norm', 'w_branch_a', 'w_branch_b', 'w_branch_c',
           'w_out', 'norm_ffn2', 'ffn2_w1', 'ffn2_w3', 'ffn2_w2']
BIG = dict(ffn1_w1=True, ffn1_w3=True, ffn1_w2=False, w_in=True, s5_glu_w=True, w_branch_a=True, w_branch_b=True,
           w_branch_c=True, w_out=False, ffn2_w1=True, ffn2_w3=True, ffn2_w2=False)
SMALL = ['norm_ffn1', 'norm_mix', 'norm_ffn2', 'gdn_conv', 'gdn_a_log', 'gdn_dt_bias', 'gdn_out_norm', 's5_a_re',
         's5_a_im', 's5_b_re', 's5_b_im', 's5_c_re', 's5_c_im', 's5_d', 's5_log_step', 's5_glu_b', 'dil_q_norm',
         'dil_k_norm']


def _full_from_shards(g, cols):
    n, r, c = g.shape
    return jnp.transpose(g, (1, 0, 2)).reshape(r, n * c) if cols else g.reshape(n * r, c)


def _shards_from_full(w, cols):
    if cols:
        r, nc = w.shape
        return jnp.transpose(w.reshape(r, N_DEV, nc // N_DEV), (1, 0, 2))
    nr, c = w.shape
    return w.reshape(N_DEV, nr // N_DEV, c)


def kernel(x, c, ada_w, ada_b, norm_ffn1, ffn1_w1, ffn1_w3, ffn1_w2, norm_mix, w_in, gdn_conv, gdn_a_log, gdn_dt_bias, gdn_out_norm, s5_a_re, s5_a_im, s5_b_re, s5_b_im, s5_c_re, s5_c_im, s5_d, s5_log_step, s5_glu_w, s5_glu_b, dil_q_norm, dil_k_norm, w_branch_a, w_branch_b, w_branch_c, w_out, norm_ffn2, ffn2_w1, ffn2_w3, ffn2_w2, loss_target, m_ada_w, m_ada_b, m_norm_ffn1, m_ffn1_w1, m_ffn1_w3, m_ffn1_w2, m_norm_mix, m_w_in, m_gdn_conv, m_gdn_a_log, m_gdn_dt_bias, m_gdn_out_norm, m_s5_a_re, m_s5_a_im, m_s5_b_re, m_s5_b_im, m_s5_c_re, m_s5_c_im, m_s5_d, m_s5_log_step, m_s5_glu_w, m_s5_glu_b, m_dil_q_norm, m_dil_k_norm, m_w_branch_a, m_w_branch_b, m_w_branch_c, m_w_out, m_norm_ffn2, m_ffn2_w1, m_ffn2_w3, m_ffn2_w2, v_ada_w, v_ada_b, v_norm_ffn1, v_ffn1_w1, v_ffn1_w3, v_ffn1_w2, v_norm_mix, v_w_in, v_gdn_conv, v_gdn_a_log, v_gdn_dt_bias, v_gdn_out_norm, v_s5_a_re, v_s5_a_im, v_s5_b_re, v_s5_b_im, v_s5_c_re, v_s5_c_im, v_s5_d, v_s5_log_step, v_s5_glu_w, v_s5_glu_b, v_dil_q_norm, v_dil_k_norm, v_w_branch_a, v_w_branch_b, v_w_branch_c, v_w_out, v_norm_ffn2, v_ffn2_w1, v_ffn2_w3, v_ffn2_w2):
    env = dict(locals())
    w = {n: env[n] for n in WEIGHTS}
    m = {n: env["m_" + n] for n in WEIGHTS}
    v = {n: env["v_" + n] for n in WEIGHTS}
    L = ada_w.shape[0]
    B, S, D = x.shape
    T = B * S
    me = _flat_index(*_my_place())

    big_keys = [(n, l) for l in range(L) for n in BIG]
    groups = {}
    for n, l in big_keys:
        r, cc = w[n].shape[1:]
        groups.setdefault((BIG[n], r if BIG[n] else cc), []).append((n, l))
    shards = {}
    for (cols, dim), keys in groups.items():
        buf = jnp.concatenate([w[n][l].astype(BF16) for n, l in keys], axis=1 if cols else 0)
        got = _all_gather(buf, f"gather_weights_{'c' if cols else 'r'}{dim}")
        off = 0
        for n, l in keys:
            k = w[n].shape[2] if cols else w[n].shape[1]
            shards[(n, l)] = got[:, :, off:off + k] if cols else got[:, off:off + k, :]
            off += k
    small_in = _pack([jnp.pad(c, ((0, SUBLANES - B), (0, 0))), gdn_conv], F32, LANES)
    c_g, conv_g = _unpack(_all_gather(small_in, "gather_cond").reshape(N_DEV, -1),
                          [(SUBLANES, D), gdn_conv.shape])
    c_all = c_g[:, :B].reshape(N_DEV * B, D)
    conv_full = jnp.transpose(conv_g, (1, 2, 0, 3)).reshape(L, GDN_CONV, -1)
    Ws = []
    for l in range(L):
        full = {n: _full_from_shards(shards[(n, l)], BIG[n]) for n in BIG}
        Wl = {n: full[n] for n in ("s5_glu_w", "w_branch_a", "w_branch_b", "w_branch_c", "w_out", "ffn1_w2", "ffn2_w2")}
        Wl["ffn1_w13"] = jnp.concatenate([full["ffn1_w1"], full["ffn1_w3"]], axis=1)
        Wl["ffn2_w13"] = jnp.concatenate([full["ffn2_w1"], full["ffn2_w3"]], axis=1)
        Wl["w_in"] = _pad_w_in(full["w_in"])
        Ws.append(Wl)
    sms = [dict({n: w[n][l] for n in SMALL}, gdn_conv=conv_full[l]) for l in range(L)]

    n_ada = ada_w.shape[2]
    bias = lax.dynamic_slice(ada_b, (0, me * n_ada), (L, n_ada))
    mod_cols = jnp.concatenate([_ada_fwd(c_all, ada_w[l], bias[l][None], f"ada{l}") for l in range(L)], axis=0)
    mod_g = _all_gather(mod_cols, "gather_mod").reshape(N_DEV, L, N_DEV * B, n_ada)
    mod_mine = lax.dynamic_slice(mod_g, (0, 0, me * B, 0), (N_DEV, L, B, n_ada))
    mod_mine = jnp.transpose(mod_mine, (1, 2, 0, 3)).reshape(L, B, N_DEV * n_ada)
    mods = [[mod_mine[l][:, None, k * D:(k + 1) * D] for k in range(9)] for l in range(L)]

    h = x.reshape(T, D)
    saved = []
    for l in range(L):
        h, sv = _layer_fwd(h, mods[l], Ws[l], sms[l], B, f"l{l}")
        saved.append(sv)
    loss_row, dh = _loss_head(h, loss_target.reshape(T, D), "loss")
    loss = lax.psum(loss_row[0, 0], ("x", "y", "c"))
    bigs, smalls, dmods = [None] * L, [None] * L, [None] * L
    for l in reversed(range(L)):
        dh, bigs[l], smalls[l], dmods[l] = _layer_bwd(dh, saved[l], mods[l], Ws[l], sms[l], B, f"l{l}")
    grad_x = dh.reshape(B, S, D)

    def full_grad(n, l):
        if n in ("ffn1_w1", "ffn1_w3", "ffn2_w1", "ffn2_w3"):
            both = bigs[l][n[:4] + "_w13"]
            F_ = both.shape[1] // 2
            return both[:, :F_] if n.endswith("w1") else both[:, F_:]
        if n == "w_in":
            return _unpad_w_in(bigs[l]["w_in"])
        return bigs[l][n]

    g = dict()
    for (cols, dim), keys in groups.items():
        tag = f"{'c' if cols else 'r'}{dim}"
        buf = jnp.concatenate([_shards_from_full(full_grad(n, l), cols) for n, l in keys], axis=2 if cols else 1)
        summed = _sum0(_exchange_pieces(buf, f"scatter_grads_{tag}"), f"sum_grads_{tag}", tr=64)
        off = 0
        for n, l in keys:
            k = w[n].shape[2] if cols else w[n].shape[1]
            g.setdefault(n, [None] * L)[l] = summed[:, off:off + k] if cols else summed[off:off + k, :]
            off += k
    g = {n: jnp.stack(ts) for n, ts in g.items()}

    small_keys = [(n, l) for l in range(L) for n in SMALL]
    small_flat = _pack([smalls[l][n] for n, l in small_keys], F32, LANES)
    small_sum = _sum0(_all_gather(small_flat, "gather_small_grads"), "sum_small_grads")
    small_full = {}
    for (n, l), t in zip(small_keys, _unpack(small_sum.reshape(-1), [smalls[l][n].shape for n, l in small_keys])):
        small_full.setdefault(n, [None] * L)[l] = t
    for n, ts in small_full.items():
        g[n] = jnp.stack(ts)
    n_conv = gdn_conv.shape[2]
    g["gdn_conv"] = lax.dynamic_slice(g["gdn_conv"], (0, 0, me * n_conv), (L, GDN_CONV, n_conv))

    dmod_mine = jnp.stack([jnp.concatenate([d[:, 0] for d in dmods[l]], axis=1) for l in range(L)])
    dmod_in = jnp.pad(dmod_mine.reshape(L * B, -1), ((0, SUBLANES - L * B), (0, 0)))
    dmod_g = _all_gather(dmod_in, "gather_dmod")[:, :L * B].reshape(N_DEV, L, B, -1)
    dmod_all = jnp.transpose(dmod_g, (1, 0, 2, 3)).reshape(L, N_DEV * B, -1)
    g["ada_b"] = _sum0(dmod_all.reshape(L, N_DEV * B, -1, LANES).transpose(1, 0, 2, 3).reshape(N_DEV * B, -1, LANES),
                       "sum_ada_b").reshape(L, -1)
    dmod_cols = lax.dynamic_slice(dmod_all, (0, 0, me * n_ada), (L, N_DEV * B, n_ada))
    g["ada_w"] = jnp.stack([_ada_bwd(c_all, dmod_cols[l], f"ada{l}_bwd") for l in range(L)])

    upd = {n: _adamw(w[n], g[n], m[n], v[n], f"adamw_{n}") for n in WEIGHTS}
    return (loss, grad_x, *[g[n] for n in WEIGHTS], *[upd[n][0] for n in WEIGHTS],
            *[upd[n][1] for n in WEIGHTS], *[upd[n][2] for n in WEIGHTS])
```

```python
import functools
import math

import jax
import jax.numpy as jnp
import numpy as np
from jax import lax
from jax.experimental import pallas as pl
from jax.experimental.pallas import tpu as pltpu

F32 = jnp.float32
BF16 = jnp.bfloat16

LANES = 128
SUBLANES = 8
VMEM_LIMIT = 56 * 1024 * 1024

N_DEV = 8
EPS = 1e-6
FFN_RES = 0.5
GDN_HEADS = 8
GDN_DIM = 128
GDN_CONV = 4
GDN_CHUNK = 128
S5_GROUP = 16
S5_STATE = 64
S5_MAX_RE = -1e-4
S5_TCHUNK = 512
DIL_PAIRS = ((128, 1), (512, 4), (2048, 16))
DIL_HPG = 4
DIL_DIM = 64
DIL_BLK = 128
ALIBI_MAX = 8.0
ADAM_LR, ADAM_B1, ADAM_B2, ADAM_EPS, ADAM_WD, ADAM_STEP = 0.001, 0.9, 0.999, 1e-08, 0.01, 10

HI = lax.Precision.HIGHEST
HI3 = lax.Precision.HIGH


def _cparams(sem=None, **kw):
    return pltpu.CompilerParams(dimension_semantics=sem, vmem_limit_bytes=VMEM_LIMIT, **kw)


def _pick(n, cands):
    for c in cands:
        if n % c == 0:
            return c
    return n


def _mm(a, b, *, ta=False, tb=False, out_dtype=F32, name):
    M, K = (a.shape[1], a.shape[0]) if ta else a.shape
    N = b.shape[0] if tb else b.shape[1]
    assert (b.shape[1] if tb else b.shape[0]) == K, (a.shape, b.shape, ta, tb)
    tm = _pick(M, (1024, 512, 256, 128))
    tn = _pick(N, (1024, 768, 512, 384, 256, 128))
    tk = _pick(K, (2048, 1536, 1408, 1024, 768, 512, 256, 128))
    nk = K // tk

    def body(a_ref, b_ref, o_ref, acc_ref):
        k = pl.program_id(2)

        @pl.when(k == 0)
        def _():
            acc_ref[...] = jnp.zeros_like(acc_ref)

        dn = (((0 if ta else 1,), (1 if tb else 0,)), ((), ()))
        acc_ref[...] += lax.dot_general(a_ref[...], b_ref[...], dn, preferred_element_type=F32)

        @pl.when(k == nk - 1)
        def _():
            o_ref[...] = acc_ref[...].astype(o_ref.dtype)

    a_spec = pl.BlockSpec((tk, tm), lambda i, j, k: (k, i)) if ta else pl.BlockSpec((tm, tk), lambda i, j, k: (i, k))
    b_spec = pl.BlockSpec((tn, tk), lambda i, j, k: (j, k)) if tb else pl.BlockSpec((tk, tn), lambda i, j, k: (k, j))
    return pl.pallas_call(
        body, name=name,
        out_shape=jax.ShapeDtypeStruct((M, N), out_dtype),
        grid=(M // tm, N // tn, nk),
        in_specs=[a_spec, b_spec],
        out_specs=pl.BlockSpec((tm, tn), lambda i, j, k: (i, j)),
        scratch_shapes=[pltpu.VMEM((tm, tn), F32)],
        compiler_params=_cparams(("parallel", "parallel", "arbitrary")),
    )(a, b)


def _norm_arg(a):
    return a if isinstance(a, tuple) else (a, None, 0)


def _ew_specs(rows, exs, ws, tr, tpe):
    specs = []
    for arr, cw, off in rows:
        if cw is None:
            specs.append(pl.BlockSpec((tr, arr.shape[1]), lambda j, i: (i, 0)))
        else:
            specs.append(pl.BlockSpec((tr, cw), lambda j, i, off=off: (i, j + off)))
    for arr, cw, off in exs:
        if cw is None:
            specs.append(pl.BlockSpec((1, 1, arr.shape[2]), lambda j, i: (i // tpe, 0, 0)))
        else:
            specs.append(pl.BlockSpec((1, 1, cw), lambda j, i, off=off: (i // tpe, 0, j + off)))
    for arr, cw, off in ws:
        if cw is None:
            specs.append(pl.BlockSpec((1, arr.shape[1]), lambda j, i: (0, 0)))
        else:
            specs.append(pl.BlockSpec((1, cw), lambda j, i, off=off: (0, j + off)))
    return specs


def _ew_fwd(fn, rows, exs, ws, outs, *, ncb=1, tr=256, seq=None, name):
    rows, exs, ws = [list(map(_norm_arg, g)) for g in (rows, exs, ws)]
    T = rows[0][0].shape[0]
    seq = seq or T
    tr = min(tr, seq)
    tpe = seq // tr
    nr, ne, nw = len(rows), len(exs), len(ws)

    def body(*refs):
        ins = [r[...].astype(F32) for r in refs[:nr]]
        ins += [r[0].astype(F32) for r in refs[nr:nr + ne]]
        ins += [r[...].astype(F32) for r in refs[nr + ne:nr + ne + nw]]
        res = fn(*ins)
        if not isinstance(res, (tuple, list)):
            res = (res,)
        for o_ref, v in zip(refs[nr + ne + nw:], res):
            o_ref[...] = v.astype(o_ref.dtype)

    out_shape, out_specs = [], []
    for width, dtype, blocked in outs:
        out_shape.append(jax.ShapeDtypeStruct((T, width), dtype))
        if blocked:
            out_specs.append(pl.BlockSpec((tr, width // ncb), lambda j, i: (i, j)))
        else:
            out_specs.append(pl.BlockSpec((tr, width), lambda j, i: (i, 0)))
    res = pl.pallas_call(
        body, name=name, out_shape=out_shape, grid=(ncb, T // tr),
        in_specs=_ew_specs(rows, exs, ws, tr, tpe), out_specs=out_specs,
        compiler_params=_cparams(("parallel", "parallel")),
    )(*[a[0] for a in rows + exs + ws])
    return res


def _ew_bwd(fn, rows, exs, ws, douts, need, *, ncb=1, tr=256, seq=None, row_dtypes=None, name):
    rows, exs, ws, douts = [list(map(_norm_arg, g)) for g in (rows, exs, ws, douts)]
    T = rows[0][0].shape[0]
    seq = seq or T
    tr = min(tr, seq)
    tpe = seq // tr
    nrt = T // tr
    nr, ne, nw, nd = len(rows), len(exs), len(ws), len(douts)
    nin = nr + ne + nw
    args = rows + exs + ws
    row_dtypes = row_dtypes or [F32] * nr
    for k, (arr, cw, off) in enumerate(exs):
        assert not (need[nr + k] and cw is None and ncb > 1)

    def body(*refs):
        j, i = pl.program_id(0), pl.program_id(1)
        ins = [r[...].astype(F32) for r in refs[:nr]]
        ins += [r[0].astype(F32) for r in refs[nr:nr + ne]]
        ins += [r[...].astype(F32) for r in refs[nr + ne:nin]]
        cts = [r[...].astype(F32) for r in refs[nin:nin + nd]]
        res, vjp = jax.vjp(fn, *ins)
        if isinstance(res, (tuple, list)):
            grads = vjp(tuple(cts))
        else:
            grads = vjp(cts[0])
        o = nin + nd
        for k in range(nin):
            if not need[k]:
                continue
            o_ref, g = refs[o], grads[k]
            o += 1
            if k < nr:
                o_ref[...] = g.astype(o_ref.dtype)
            elif k < nr + ne:
                first = (i % tpe) == 0

                @pl.when(first)
                def _(o_ref=o_ref, g=g):
                    o_ref[0] = g

                @pl.when(jnp.logical_not(first))
                def _(o_ref=o_ref, g=g):
                    o_ref[0] += g
            else:
                blocked = args[k][1] is not None
                first = (i == 0) if blocked else jnp.logical_and(i == 0, j == 0)

                @pl.when(first)
                def _(o_ref=o_ref, g=g):
                    o_ref[...] = g

                @pl.when(jnp.logical_not(first))
                def _(o_ref=o_ref, g=g):
                    o_ref[...] += g

    in_specs = _ew_specs(rows, exs, ws, tr, tpe) + _ew_specs(douts, [], [], tr, tpe)
    out_shape, out_specs = [], []
    all_specs = _ew_specs(rows, exs, ws, tr, tpe)
    for k in range(nin):
        if not need[k]:
            continue
        arr, cw, off = args[k]
        if k < nr and cw is not None:
            out_shape.append(jax.ShapeDtypeStruct((T, ncb * cw), row_dtypes[k]))
            out_specs.append(pl.BlockSpec((tr, cw), lambda j, i: (i, j)))
        elif k >= nr and cw is not None:
            assert off == 0 and arr.shape[-1] == ncb * cw
            out_shape.append(jax.ShapeDtypeStruct(arr.shape, F32))
            out_specs.append(all_specs[k])
        else:
            out_shape.append(jax.ShapeDtypeStruct(arr.shape, row_dtypes[k] if k < nr else F32))
            out_specs.append(all_specs[k])
    res = pl.pallas_call(
        body, name=name, out_shape=out_shape, grid=(ncb, nrt),
        in_specs=in_specs, out_specs=out_specs,
        compiler_params=_cparams(("arbitrary", "arbitrary")),
    )(*[a[0] for a in args + douts])
    return res


def _rms(x, g):
    return x * lax.rsqrt(jnp.mean(x * x, axis=-1, keepdims=True) + EPS) * g


def _f_rmsmod(x, sh, sc, g):
    return _rms(x, g) * (1.0 + sc) + sh


def _f_swiglu(a, b):
    return a * jax.nn.sigmoid(a) * b


def _f_res(res, x, y, gate):
    return x + res * gate * y


def _f_res_rmsmod(res, x, y, gate, sh, sc, g):
    x1 = x + res * gate * y
    return x1, _rms(x1, g) * (1.0 + sc) + sh


def _ffn_fwd(h, w13, w2, tag):
    F = w2.shape[0]
    cw = _pick(F, (512, 256, 128))
    ncb = F // cw
    ab = _mm(h, w13, name=f"{tag}_up")
    (s,) = _ew_fwd(_f_swiglu, [(ab, cw, 0), (ab, cw, ncb)], [], [], [(F, BF16, True)], ncb=ncb, name=f"{tag}_act")
    y = _mm(s, w2, name=f"{tag}_down")
    return y, (h, ab, s)


def _ffn_bwd(dy, saved, w13, w2, tag):
    h, ab, s = saved
    F = w2.shape[0]
    cw = _pick(F, (512, 256, 128))
    ncb = F // cw
    ds = _mm(dy, w2, tb=True, name=f"{tag}_down_dx")
    dw2 = _mm(s, dy, ta=True, out_dtype=BF16, name=f"{tag}_down_dw")
    da, db = _ew_bwd(_f_swiglu, [(ab, cw, 0), (ab, cw, ncb)], [], [], [(ds, cw, 0)], [True, True], ncb=ncb,
                     row_dtypes=[BF16, BF16], name=f"{tag}_act_bwd")
    dab = jnp.concatenate([da, db], axis=1)
    dw13 = _mm(h, dab, ta=True, out_dtype=BF16, name=f"{tag}_up_dw")
    dh = _mm(dab, w13, tb=True, name=f"{tag}_up_dx")
    return dh, dw13, dw2


def _shift_down(x, j):
    if j == 0:
        return x
    row = lax.broadcasted_iota(jnp.int32, x.shape, 0)
    return jnp.where(row >= j, pltpu.roll(x, j, 0), 0.0)


def _shift_up(x, j):
    if j == 0:
        return x
    n = x.shape[0]
    row = lax.broadcasted_iota(jnp.int32, x.shape, 0)
    return jnp.where(row < n - j, pltpu.roll(x, n - j, 0), 0.0)


def _gdn_post_conv(kind, y):
    s = y * jax.nn.sigmoid(y)
    if kind == "v":
        return s
    n = lax.rsqrt(jnp.sum(s * s, axis=-1, keepdims=True) + EPS)
    return s * n * (GDN_DIM ** -0.5 if kind == "q" else 1.0)


def _conv_taps(w_ref):
    return [w_ref[k:k + 1, :] for k in range(GDN_CONV)]


def _gdn_conv(x, w):
    y = w[GDN_CONV - 1] * x
    for k in range(GDN_CONV - 1):
        y = y + w[k] * _shift_down(x, GDN_CONV - 1 - k)
    return y


def _gdn_pre_fwd(proj, conv_w, kind, first_block, B, name):
    T = proj.shape[0]
    S = T // B
    nh = GDN_HEADS

    def body(x_ref, w_ref, o_ref):
        y = _gdn_conv(x_ref[...], _conv_taps(w_ref))
        o_ref[...] = _gdn_post_conv(kind, y)

    return pl.pallas_call(
        body, name=name, out_shape=jax.ShapeDtypeStruct((T, nh * GDN_DIM), F32), grid=(nh, B),
        in_specs=[pl.BlockSpec((S, GDN_DIM), lambda c, b: (b, c + first_block)),
                  pl.BlockSpec((GDN_CONV, GDN_DIM), lambda c, b: (0, c))],
        out_specs=pl.BlockSpec((S, GDN_DIM), lambda c, b: (b, c)),
        compiler_params=_cparams(("parallel", "parallel")),
    )(proj, conv_w)


def _gdn_pre_bwd(proj, conv_w, dout, kind, first_block, B, name):
    T = proj.shape[0]
    S = T // B
    nh = GDN_HEADS

    def body(x_ref, w_ref, d_ref, dx_ref, dw_ref):
        b = pl.program_id(1)
        x, w = x_ref[...], _conv_taps(w_ref)
        y = _gdn_conv(x, w)
        _, vjp = jax.vjp(functools.partial(_gdn_post_conv, kind), y)
        (dy,) = vjp(d_ref[...])
        dx = w[GDN_CONV - 1] * dy

        @pl.when(b == 0)
        def _():
            dw_ref[...] = jnp.zeros_like(dw_ref)

        for k in range(GDN_CONV):
            j = GDN_CONV - 1 - k
            if j:
                dx = dx + w[k] * _shift_up(dy, j)
            dw_ref[k:k + 1, :] += jnp.sum(dy * _shift_down(x, j), axis=0, keepdims=True)
        dx_ref[...] = dx

    return pl.pallas_call(
        body, name=name,
        out_shape=[jax.ShapeDtypeStruct((T, nh * GDN_DIM), F32), jax.ShapeDtypeStruct((GDN_CONV, nh * GDN_DIM), F32)],
        grid=(nh, B),
        in_specs=[pl.BlockSpec((S, GDN_DIM), lambda c, b: (b, c + first_block)),
                  pl.BlockSpec((GDN_CONV, GDN_DIM), lambda c, b: (0, c)),
                  pl.BlockSpec((S, GDN_DIM), lambda c, b: (b, c))],
        out_specs=[pl.BlockSpec((S, GDN_DIM), lambda c, b: (b, c)),
                   pl.BlockSpec((GDN_CONV, GDN_DIM), lambda c, b: (0, c))],
        compiler_params=_cparams(("arbitrary", "arbitrary")),
    )(proj, conv_w, dout)


def _softplus(x):
    return jnp.maximum(x, 0.0) + jnp.log(1.0 + jnp.exp(-jnp.abs(x)))


def _f_gdn_gates(ba, a_log, dt_bias):
    n = ba.shape[0]
    lane = lax.broadcasted_iota(jnp.int32, ba.shape, 1)
    beta = jax.nn.sigmoid(ba)
    g = -jnp.exp(a_log) * _softplus(ba + dt_bias)
    ri = lax.broadcasted_iota(jnp.int32, (n, n), 0)
    ci = lax.broadcasted_iota(jnp.int32, (n, n), 1)
    tri = jnp.where((ri // GDN_CHUNK == ci // GDN_CHUNK) & (ci <= ri), 1.0, 0.0)
    gc = jnp.dot(tri, g, precision=HI, preferred_element_type=F32)
    return jnp.where(lane < GDN_HEADS, beta, gc)


def _bmm(a, b, ca, cb):
    return lax.dot_general(a, b, (((ca,), (cb,)), ((0,), (0,))), precision=HI3, preferred_element_type=F32)


GDN_INV_LEAF = 16


def _unit_lower_inverse(low, ri, ci):
    C = low.shape[1]
    b = GDN_INV_LEAF
    p = jnp.where(ri // b == ci // b, low, 0.0)
    x = jnp.where(ci == ri, 1.0, 0.0) - p
    for _ in range(int(math.log2(b)) - 1):
        p = _bmm(p, p, 2, 1)
        x = x + _bmm(x, p, 2, 1)
    while b < C:
        off = jnp.where(jnp.logical_and(ri // (2 * b) == ci // (2 * b), ri // b != ci // b), low, 0.0)
        x = x - _bmm(_bmm(x, off, 2, 1), x, 2, 1)
        b *= 2
    return x


def _gdn_chunk_fn(q, k, v, gc, beta, h):
    N, C, d = q.shape
    ri = lax.broadcasted_iota(jnp.int32, (N, C, C), 1)
    ci = lax.broadcasted_iota(jnp.int32, (N, C, C), 2)
    kb = k * beta
    vb = v * beta
    gi = jnp.broadcast_to(gc, (N, C, C))
    gj = jnp.swapaxes(gi, 1, 2)
    decay = jnp.exp(jnp.where(ci <= ri, gi - gj, -1e30))
    low = jnp.where(ci < ri, _bmm(kb, k, 2, 2) * decay, 0.0)
    ainv = _unit_lower_inverse(low, ri, ci)
    eg = jnp.exp(gc)
    u = _bmm(ainv, vb, 2, 1)
    w = _bmm(ainv, kb * eg, 2, 1)
    attn = _bmm(q, k, 2, 2) * decay
    v_new = u - _bmm(w, h, 2, 1)
    o = _bmm(q * eg, h, 2, 1) + _bmm(attn, v_new, 2, 1)
    rc = lax.broadcasted_iota(jnp.int32, (N, C, 1), 1)
    g_last = jnp.sum(jnp.where(rc == C - 1, gc, 0.0), axis=1, keepdims=True)
    h_new = h * jnp.exp(g_last) + _bmm(k * jnp.exp(g_last - gc), v_new, 1, 1)
    return o, h_new


def _gdn_heads(x):
    return jnp.stack([x[:, h * GDN_DIM:(h + 1) * GDN_DIM] for h in range(GDN_HEADS)], axis=0)


def _gdn_gate_cols(G, first_lane):
    lane = lax.broadcasted_iota(jnp.int32, G.shape, 1)
    return jnp.stack([jnp.sum(jnp.where(lane == first_lane + h, G, 0.0), axis=1, keepdims=True)
                      for h in range(GDN_HEADS)], axis=0)


def _gdn_chunk_specs(nc, rev):
    C, W = GDN_CHUNK, GDN_HEADS * GDN_DIM

    def at(n):
        return nc - 1 - n if rev else n

    row = lambda b, n: (b * nc + at(n), 0)
    return [pl.BlockSpec((C, W), row)] * 3 + [pl.BlockSpec((C, LANES), row)]


def _gdn_scan_fwd(q, k, v, G, B, name):
    T, W = q.shape
    C = GDN_CHUNK
    nc = T // B // C

    def body(q_ref, k_ref, v_ref, g_ref, o_ref, hs_ref, h_ref):
        @pl.when(pl.program_id(1) == 0)
        def _():
            h_ref[...] = jnp.zeros_like(h_ref)

        G_ = g_ref[...]
        h = h_ref[...]
        hs_ref[0, 0] = h
        o, hn = _gdn_chunk_fn(_gdn_heads(q_ref[...]), _gdn_heads(k_ref[...]), _gdn_heads(v_ref[...]),
                              _gdn_gate_cols(G_, GDN_HEADS), _gdn_gate_cols(G_, 0), h)
        h_ref[...] = hn
        for hd in range(GDN_HEADS):
            o_ref[:, hd * GDN_DIM:(hd + 1) * GDN_DIM] = o[hd]

    return pl.pallas_call(
        body, name=name,
        out_shape=[jax.ShapeDtypeStruct((T, W), F32), jax.ShapeDtypeStruct((B, nc, GDN_HEADS, GDN_DIM, GDN_DIM), F32)],
        grid=(B, nc), in_specs=_gdn_chunk_specs(nc, False),
        out_specs=[pl.BlockSpec((C, W), lambda b, n: (b * nc + n, 0)),
                   pl.BlockSpec((1, 1, GDN_HEADS, GDN_DIM, GDN_DIM), lambda b, n: (b, n, 0, 0, 0))],
        scratch_shapes=[pltpu.VMEM((GDN_HEADS, GDN_DIM, GDN_DIM), F32)],
        compiler_params=_cparams(("parallel", "arbitrary")),
    )(q, k, v, G)


def _gdn_scan_bwd(q, k, v, G, hs, do, B, name):
    T, W = q.shape
    C = GDN_CHUNK
    nc = T // B // C

    def body(q_ref, k_ref, v_ref, g_ref, hs_ref, do_ref, dq_ref, dk_ref, dv_ref, dg_ref, dh_ref):
        @pl.when(pl.program_id(1) == 0)
        def _():
            dh_ref[...] = jnp.zeros_like(dh_ref)

        G_ = g_ref[...]
        args = (_gdn_heads(q_ref[...]), _gdn_heads(k_ref[...]), _gdn_heads(v_ref[...]),
                _gdn_gate_cols(G_, GDN_HEADS), _gdn_gate_cols(G_, 0), hs_ref[0, 0])
        _, vjp = jax.vjp(_gdn_chunk_fn, *args)
        dq, dk, dv, dgc, dbeta, dh = vjp((_gdn_heads(do_ref[...]), dh_ref[...]))
        dh_ref[...] = dh
        lane = lax.broadcasted_iota(jnp.int32, G_.shape, 1)
        dG = jnp.zeros_like(G_)
        for hd in range(GDN_HEADS):
            sl = slice(hd * GDN_DIM, (hd + 1) * GDN_DIM)
            dq_ref[:, sl] = dq[hd]
            dk_ref[:, sl] = dk[hd]
            dv_ref[:, sl] = dv[hd]
            dG = dG + jnp.where(lane == hd, dbeta[hd], 0.0) + jnp.where(lane == GDN_HEADS + hd, dgc[hd], 0.0)
        dg_ref[...] = dG

    rrow = lambda b, n: (b * nc + nc - 1 - n, 0)
    return pl.pallas_call(
        body, name=name,
        out_shape=[jax.ShapeDtypeStruct((T, W), F32)] * 3 + [jax.ShapeDtypeStruct((T, LANES), F32)],
        grid=(B, nc),
        in_specs=_gdn_chunk_specs(nc, True) + [
            pl.BlockSpec((1, 1, GDN_HEADS, GDN_DIM, GDN_DIM), lambda b, n: (b, nc - 1 - n, 0, 0, 0)),
            pl.BlockSpec((C, W), rrow)],
        out_specs=[pl.BlockSpec((C, W), rrow)] * 3 + [pl.BlockSpec((C, LANES), rrow)],
        scratch_shapes=[pltpu.VMEM((GDN_HEADS, GDN_DIM, GDN_DIM), F32)],
        compiler_params=_cparams(("parallel", "arbitrary")),
    )(q, k, v, G, hs, do)


def _f_gdn_out(o, z, w):
    return _rms(o, w) * z * jax.nn.sigmoid(z)


def _gdn_fwd(qkv, z, ba, conv_w, gate_params, out_norm, B, tag):
    a_log, dt_bias = gate_params
    W = GDN_HEADS * GDN_DIM
    qn, kn, vn = [_gdn_pre_fwd(qkv, conv_w[:, i * W:(i + 1) * W], kd, i * GDN_HEADS, B, f"{tag}_pre_{kd}")
                  for i, kd in enumerate("qkv")]
    (G,) = _ew_fwd(_f_gdn_gates, [ba], [], [a_log, dt_bias], [(LANES, F32, False)], name=f"{tag}_gates")
    o, hs = _gdn_scan_fwd(qn, kn, vn, G, B, f"{tag}_scan")
    (y,) = _ew_fwd(_f_gdn_out, [(o, GDN_DIM, 0), (z, GDN_DIM, 0)], [], [out_norm], [(W, BF16, True)],
                   ncb=GDN_HEADS, name=f"{tag}_out")
    return y, (qkv, z, ba, qn, kn, vn, G, hs, o)


def _gdn_bwd(dy, saved, conv_w, gate_params, out_norm, B, tag):
    qkv, z, ba, qn, kn, vn, G, hs, o = saved
    a_log, dt_bias = gate_params
    W = GDN_HEADS * GDN_DIM
    do, dz, d_out_norm = _ew_bwd(_f_gdn_out, [(o, GDN_DIM, 0), (z, GDN_DIM, 0)], [], [out_norm],
                                 [(dy, GDN_DIM, 0)], [True] * 3, ncb=GDN_HEADS, name=f"{tag}_out_bwd")
    dq, dk, dv, dG = _gdn_scan_bwd(qn, kn, vn, G, hs, do, B, f"{tag}_scan_bwd")
    dba, d_a_log, d_dt_bias = _ew_bwd(_f_gdn_gates, [ba], [], [a_log, dt_bias], [dG], [True] * 3,
                                      name=f"{tag}_gates_bwd")
    dxs, dws = [], []
    for i, (kd, d) in enumerate(zip("qkv", (dq, dk, dv))):
        dx, dw = _gdn_pre_bwd(qkv, conv_w[:, i * W:(i + 1) * W], d, kd, i * GDN_HEADS, B, f"{tag}_pre_{kd}_bwd")
        dxs.append(dx)
        dws.append(dw)
    return (jnp.concatenate(dxs, axis=1), dz, dba,
            dict(conv=jnp.concatenate(dws, axis=1), a_log=d_a_log, dt_bias=d_dt_bias, out_norm=d_out_norm))


S5_GPB = LANES // S5_GROUP
S5_SLANES = S5_GPB * S5_STATE


def _cmul(ar, ai, br, bi):
    return ar * br - ai * bi, ar * bi + ai * br


def _s5_prep_fn(a_re, a_im, ls, b_re, b_im):
    lr = jnp.minimum(a_re, S5_MAX_RE)
    li = a_im
    step = jnp.exp(ls)
    mag = jnp.exp(lr * step)
    lbr, lbi = mag * jnp.cos(li * step), mag * jnp.sin(li * step)
    den = lr * lr + li * li
    cr = ((lbr - 1.0) * lr + lbi * li) / den
    ci = (lbi * lr - (lbr - 1.0) * li) / den
    bbr = cr[:, None, :] * b_re - ci[:, None, :] * b_im
    bbi = cr[:, None, :] * b_im + ci[:, None, :] * b_re
    return lbr, lbi, bbr, bbi


def _s5_prep_fwd(args, name):
    G, I, P = args[3].shape
    shp = [jax.ShapeDtypeStruct((G, P), F32)] * 2 + [jax.ShapeDtypeStruct((G, I, P), F32)] * 2

    def body(*refs):
        for o_ref, v in zip(refs[5:], _s5_prep_fn(*[r[...] for r in refs[:5]])):
            o_ref[...] = v

    return pl.pallas_call(body, name=name, out_shape=shp, compiler_params=_cparams())(*args)


def _s5_prep_bwd(args, cts, name):
    shp = [jax.ShapeDtypeStruct(a.shape, F32) for a in args]

    def body(*refs):
        _, vjp = jax.vjp(_s5_prep_fn, *[r[...] for r in refs[:5]])
        for o_ref, v in zip(refs[9:], vjp(tuple(r[...] for r in refs[5:9]))):
            o_ref[...] = v

    return pl.pallas_call(body, name=name, out_shape=shp, compiler_params=_cparams())(*args, *cts)


def _s5_blockdiag_in(bb):
    G, I, P = bb.shape
    nb = G // S5_GPB
    return jnp.einsum("jgip,gh->jgihp", bb.reshape(nb, S5_GPB, I, P), jnp.eye(S5_GPB, dtype=bb.dtype)).reshape(
        nb, S5_GPB * I, S5_GPB * P)


def _s5_blockdiag_in_t(d):
    nb = d.shape[0]
    d = d.reshape(nb, S5_GPB, S5_GROUP, S5_GPB, S5_STATE)
    return jnp.einsum("jgihp,gh->jgip", d, jnp.eye(S5_GPB, dtype=d.dtype)).reshape(nb * S5_GPB, S5_GROUP, S5_STATE)


def _s5_blockdiag_out(c):
    G, I, P = c.shape
    nb = G // S5_GPB
    return jnp.einsum("jgip,gh->jgphi", c.reshape(nb, S5_GPB, I, P), jnp.eye(S5_GPB, dtype=c.dtype)).reshape(
        nb, S5_GPB * P, S5_GPB * I)


def _s5_blockdiag_out_t(d):
    nb = d.shape[0]
    d = d.reshape(nb, S5_GPB, S5_STATE, S5_GPB, S5_GROUP)
    return jnp.einsum("jgphi,gh->jgip", d, jnp.eye(S5_GPB, dtype=d.dtype)).reshape(nb * S5_GPB, S5_GROUP, S5_STATE)


def _s5_powers(lr, li, n):
    out = []
    for _ in range(int(math.log2(n))):
        out.append((lr, li))
        lr, li = _cmul(lr, li, lr, li)
    return out


def _s5_local_scan(sr, si, powers, up):
    shift = _shift_up if up else _shift_down
    for k, (pr, pi) in enumerate(powers):
        d = 1 << k
        tr_, ti_ = _cmul(pr, pi, shift(sr, d), shift(si, d))
        sr, si = sr + tr_, si + ti_
    return sr, si


def _dot_hi(a, b, ca=1, cb=0):
    return lax.dot_general(a, b, (((ca,), (cb,)), ((), ())), precision=HI3, preferred_element_type=F32)


def _s5_specs(nt, rev):
    tc = S5_TCHUNK

    def at(t):
        return nt - 1 - t if rev else t

    return [
        pl.BlockSpec((tc, LANES), lambda j, b, t: (b * nt + at(t), j)),
        pl.BlockSpec((1, S5_SLANES), lambda j, b, t: (0, j)),
        pl.BlockSpec((1, S5_SLANES), lambda j, b, t: (0, j)),
        pl.BlockSpec((1, LANES, S5_SLANES), lambda j, b, t: (j, 0, 0)),
        pl.BlockSpec((1, LANES, S5_SLANES), lambda j, b, t: (j, 0, 0)),
        pl.BlockSpec((1, S5_SLANES, LANES), lambda j, b, t: (j, 0, 0)),
        pl.BlockSpec((1, S5_SLANES, LANES), lambda j, b, t: (j, 0, 0)),
        pl.BlockSpec((1, LANES), lambda j, b, t: (0, j)),
    ]


def _s5_chunk_states(u, lr, li, b_re, b_im, cr, ci, powers):
    bur, bui = _dot_hi(u, b_re), _dot_hi(u, b_im)
    row = lax.broadcasted_iota(jnp.int32, bur.shape, 0)
    inr, ini = _cmul(lr, li, cr, ci)
    bur = bur + jnp.where(row == 0, inr, 0.0)
    bui = bui + jnp.where(row == 0, ini, 0.0)
    return _s5_local_scan(bur, bui, powers, False)


def _s5_scan_fwd(u, lam_re, lam_im, Bre, Bim, Cre, Cim, dskip, B, name):
    T, Wd = u.shape
    nb = Wd // LANES
    tc = S5_TCHUNK
    nt = T // B // tc
    L = nb * S5_SLANES

    def body(u_ref, lr_ref, li_ref, br_ref, bi_ref, cr_ref, ci_ref, d_ref, y_ref, csr_ref, csi_ref, car_ref, cai_ref):
        @pl.when(pl.program_id(2) == 0)
        def _():
            car_ref[...] = jnp.zeros_like(car_ref)
            cai_ref[...] = jnp.zeros_like(cai_ref)

        csr_ref[0, 0] = car_ref[...]
        csi_ref[0, 0] = cai_ref[...]
        u_ = u_ref[...]
        lr, li = lr_ref[...], li_ref[...]
        sr, si = _s5_chunk_states(u_, lr, li, br_ref[0], bi_ref[0], car_ref[0:1, :], cai_ref[0:1, :],
                                  _s5_powers(lr, li, tc))
        y_ref[...] = _dot_hi(sr, cr_ref[0]) - _dot_hi(si, ci_ref[0]) + d_ref[...] * u_
        row = lax.broadcasted_iota(jnp.int32, sr.shape, 0)
        car_ref[0:1, :] = jnp.sum(jnp.where(row == tc - 1, sr, 0.0), axis=0, keepdims=True)
        cai_ref[0:1, :] = jnp.sum(jnp.where(row == tc - 1, si, 0.0), axis=0, keepdims=True)

    cs_shape = jax.ShapeDtypeStruct((B, nt, SUBLANES, L), F32)
    cs_spec = pl.BlockSpec((1, 1, SUBLANES, S5_SLANES), lambda j, b, t: (b, t, 0, j))
    return pl.pallas_call(
        body, name=name, out_shape=[jax.ShapeDtypeStruct((T, Wd), F32), cs_shape, cs_shape],
        grid=(nb, B, nt), in_specs=_s5_specs(nt, False),
        out_specs=[pl.BlockSpec((tc, LANES), lambda j, b, t: (b * nt + t, j)), cs_spec, cs_spec],
        scratch_shapes=[pltpu.VMEM((SUBLANES, S5_SLANES), F32)] * 2,
        compiler_params=_cparams(("parallel", "parallel", "arbitrary")),
    )(u, lam_re, lam_im, Bre, Bim, Cre, Cim, dskip)


def _s5_scan_bwd(u, lam_re, lam_im, Bre, Bim, Cre, Cim, dskip, csr, csi, dy, B, name):
    T, Wd = u.shape
    nb = Wd // LANES
    tc = S5_TCHUNK
    nt = T // B // tc
    L = nb * S5_SLANES

    def body(u_ref, lr_ref, li_ref, br_ref, bi_ref, cr_ref, ci_ref, d_ref, csr_ref, csi_ref, dy_ref,
             du_ref, dlr_ref, dli_ref, dbr_ref, dbi_ref, dcr_ref, dci_ref, dd_ref, gr_ref, gi_ref):
        first = jnp.logical_and(pl.program_id(1) == 0, pl.program_id(2) == 0)

        @pl.when(pl.program_id(2) == 0)
        def _():
            gr_ref[...] = jnp.zeros_like(gr_ref)
            gi_ref[...] = jnp.zeros_like(gi_ref)

        @pl.when(first)
        def _():
            for r in (dlr_ref, dli_ref, dbr_ref, dbi_ref, dcr_ref, dci_ref, dd_ref):
                r[...] = jnp.zeros_like(r)

        u_, dy_ = u_ref[...], dy_ref[...]
        lr, li = lr_ref[...], li_ref[...]
        powers = _s5_powers(lr, li, tc)
        c_in_r, c_in_i = csr_ref[0, 0, 0:1, :], csi_ref[0, 0, 0:1, :]
        sr, si = _s5_chunk_states(u_, lr, li, br_ref[0], bi_ref[0], c_in_r, c_in_i, powers)
        dcr_ref[0] += _dot_hi(sr, dy_, 0, 0)
        dci_ref[0] -= _dot_hi(si, dy_, 0, 0)
        dd_ref[...] += jnp.sum(dy_ * u_, axis=0, keepdims=True)
        row = lax.broadcasted_iota(jnp.int32, sr.shape, 0)
        gr = _dot_hi(dy_, cr_ref[0], 1, 1)
        gi = -_dot_hi(dy_, ci_ref[0], 1, 1)
        inr, ini = _cmul(lr, -li, gr_ref[0:1, :], gi_ref[0:1, :])
        gr = gr + jnp.where(row == tc - 1, inr, 0.0)
        gi = gi + jnp.where(row == tc - 1, ini, 0.0)
        gr, gi = _s5_local_scan(gr, gi, [(pr, -pi) for pr, pi in powers], True)
        gr_ref[0:1, :] = jnp.sum(jnp.where(row == 0, gr, 0.0), axis=0, keepdims=True)
        gi_ref[0:1, :] = jnp.sum(jnp.where(row == 0, gi, 0.0), axis=0, keepdims=True)
        pr_ = _shift_down(sr, 1) + jnp.where(row == 0, c_in_r, 0.0)
        pi_ = _shift_down(si, 1) + jnp.where(row == 0, c_in_i, 0.0)
        dlr_ref[...] += jnp.sum(gr * pr_ + gi * pi_, axis=0, keepdims=True)
        dli_ref[...] += jnp.sum(gi * pr_ - gr * pi_, axis=0, keepdims=True)
        dbr_ref[0] += _dot_hi(u_, gr, 0, 0)
        dbi_ref[0] += _dot_hi(u_, gi, 0, 0)
        du_ref[...] = dy_ * d_ref[...] + _dot_hi(gr, br_ref[0], 1, 1) + _dot_hi(gi, bi_ref[0], 1, 1)

    cs_spec = pl.BlockSpec((1, 1, SUBLANES, S5_SLANES), lambda j, b, t: (b, nt - 1 - t, 0, j))
    rrow = pl.BlockSpec((tc, LANES), lambda j, b, t: (b * nt + nt - 1 - t, j))
    lam_spec = pl.BlockSpec((1, S5_SLANES), lambda j, b, t: (0, j))
    b_spec = pl.BlockSpec((1, LANES, S5_SLANES), lambda j, b, t: (j, 0, 0))
    c_spec = pl.BlockSpec((1, S5_SLANES, LANES), lambda j, b, t: (j, 0, 0))
    return pl.pallas_call(
        body, name=name,
        out_shape=[jax.ShapeDtypeStruct((T, Wd), F32)] + [jax.ShapeDtypeStruct((1, L), F32)] * 2
        + [jax.ShapeDtypeStruct((nb, LANES, S5_SLANES), F32)] * 2
        + [jax.ShapeDtypeStruct((nb, S5_SLANES, LANES), F32)] * 2 + [jax.ShapeDtypeStruct((1, Wd), F32)],
        grid=(nb, B, nt), in_specs=_s5_specs(nt, True) + [cs_spec, cs_spec, rrow],
        out_specs=[rrow, lam_spec, lam_spec, b_spec, b_spec, c_spec, c_spec,
                   pl.BlockSpec((1, LANES), lambda j, b, t: (0, j))],
        scratch_shapes=[pltpu.VMEM((SUBLANES, S5_SLANES), F32)] * 2,
        compiler_params=_cparams(("arbitrary", "arbitrary", "arbitrary")),
    )(u, lam_re, lam_im, Bre, Bim, Cre, Cim, dskip, csr, csi, dy)


def _f_gelu(y):
    return 0.5 * y * (1.0 + jnp.tanh(math.sqrt(2.0 / math.pi) * (y + 0.044715 * (y * y * y))))


def _f_glu(pv, pg, bv, bg):
    return (pv + bv) * jax.nn.sigmoid(pg + bg)


def _s5_params(p):
    prep_in = (p["a_re"], p["a_im"], p["log_step"][:, None], jnp.swapaxes(p["b_re"], 1, 2), jnp.swapaxes(p["b_im"], 1, 2))
    return prep_in


def _s5_fwd(u, p, glu_w, B, tag):
    Wd = u.shape[1]
    prep_in = _s5_params(p)
    lbr, lbi, bbr, bbi = _s5_prep_fwd(prep_in, f"{tag}_prep")
    ops = (lbr.reshape(1, -1), lbi.reshape(1, -1), _s5_blockdiag_in(bbr), _s5_blockdiag_in(bbi),
           _s5_blockdiag_out(p["c_re"]), _s5_blockdiag_out(p["c_im"]), p["d"][None])
    y, csr, csi = _s5_scan_fwd(u, *ops, B, f"{tag}_scan")
    (yg,) = _ew_fwd(_f_gelu, [y], [], [], [(Wd, BF16, False)], name=f"{tag}_gelu")
    pj = _mm(yg, glu_w, name=f"{tag}_glu")
    bv, bg = p["glu_b"][None, :Wd], p["glu_b"][None, Wd:]
    (out,) = _ew_fwd(_f_glu, [(pj, Wd, 0), (pj, Wd, 1)], [], [bv, bg], [(Wd, BF16, False)], name=f"{tag}_gate")
    return out, (u, prep_in, ops, csr, csi, y, yg, pj, bv, bg)


def _s5_bwd(dout, saved, p, glu_w, B, tag):
    u, prep_in, ops, csr, csi, y, yg, pj, bv, bg = saved
    Wd = u.shape[1]
    dpv, dpg, dbv, dbg = _ew_bwd(_f_glu, [(pj, Wd, 0), (pj, Wd, 1)], [], [bv, bg], [dout], [True] * 4,
                                 row_dtypes=[BF16, BF16], name=f"{tag}_gate_bwd")
    dpj = jnp.concatenate([dpv, dpg], axis=1)
    d_glu_w = _mm(yg, dpj, ta=True, out_dtype=BF16, name=f"{tag}_glu_dw")
    dyg = _mm(dpj, glu_w, tb=True, name=f"{tag}_glu_dx")
    (dy,) = _ew_bwd(_f_gelu, [y], [], [], [dyg], [True], name=f"{tag}_gelu_bwd")
    du, dlr, dli, dBr, dBi, dCr, dCi, dd = _s5_scan_bwd(u, *ops, csr, csi, dy, B, f"{tag}_scan_bwd")
    G = p["a_re"].shape[0]
    cts = (dlr.reshape(G, S5_STATE), dli.reshape(G, S5_STATE), _s5_blockdiag_in_t(dBr), _s5_blockdiag_in_t(dBi))
    da_re, da_im, dls, db_re, db_im = _s5_prep_bwd(prep_in, cts, f"{tag}_prep_bwd")
    small = dict(a_re=da_re, a_im=da_im, log_step=dls[:, 0], b_re=jnp.swapaxes(db_re, 1, 2),
                 b_im=jnp.swapaxes(db_im, 1, 2), c_re=_s5_blockdiag_out_t(dCr), c_im=_s5_blockdiag_out_t(dCi),
                 d=dd[0], glu_b=jnp.concatenate([dbv[0], dbg[0]]))
    return du, d_glu_w, small


DIL_GW = DIL_HPG * DIL_DIM


def _f_qknorm(scale, x, w):
    n = x.shape[1]
    ri = lax.broadcasted_iota(jnp.int32, (n, n), 0)
    ci = lax.broadcasted_iota(jnp.int32, (n, n), 1)
    seg = jnp.where(ri // DIL_DIM == ci // DIL_DIM, 1.0 / DIL_DIM, 0.0)
    ms = jnp.dot(x * x, seg, precision=HI, preferred_element_type=F32)
    return x * lax.rsqrt(ms + EPS) * (w * scale)


def _dil_to_blocks(x, B):
    T = x.shape[0]
    S = T // B
    parts = []
    for gi, (_, dil) in enumerate(DIL_PAIRS):
        xg = x[:, gi * DIL_GW:(gi + 1) * DIL_GW].reshape(B, S // dil, dil, DIL_HPG, DIL_DIM)
        parts.append(xg.transpose(0, 2, 3, 1, 4).reshape(-1, DIL_DIM))
    return jnp.concatenate(parts, axis=0)


def _dil_from_blocks(y, B):
    n = y.shape[0] // len(DIL_PAIRS)
    T = n // DIL_HPG
    S = T // B
    parts = []
    for gi, (_, dil) in enumerate(DIL_PAIRS):
        yg = y[gi * n:(gi + 1) * n].reshape(B, dil, DIL_HPG, S // dil, DIL_DIM)
        parts.append(yg.transpose(0, 3, 1, 2, 4).reshape(T, DIL_GW))
    return jnp.concatenate(parts, axis=1)


DIL_BPS = 8


def _dil_block_fn(slope, has_prev, q, kp, kc, vp, vc):
    G, n, _ = q.shape
    qi = lax.broadcasted_iota(jnp.int32, (G, n, n), 1)
    kj = lax.broadcasted_iota(jnp.int32, (G, n, n), 2)
    dist = (qi - kj).astype(F32)
    sc = _bmm(q, kc, 2, 2) - slope * dist
    sp = _bmm(q, kp, 2, 2) - slope * (dist + n)
    sc = jnp.where(qi >= kj, sc, -1e30)
    sp = jnp.where(jnp.logical_and(kj >= qi, has_prev > 0.5), sp, -1e30)
    m = lax.stop_gradient(jnp.maximum(jnp.max(sc, axis=2, keepdims=True), jnp.max(sp, axis=2, keepdims=True)))
    pc = jnp.exp(sc - m)
    pp = jnp.exp(sp - m)
    l = jnp.sum(pc, axis=2, keepdims=True) + jnp.sum(pp, axis=2, keepdims=True)
    o = (_bmm(pp, vp, 2, 1) + _bmm(pc, vc, 2, 1)) / l
    return o, jnp.broadcast_to(m + jnp.log(l), o.shape)


def _dil_tables(B, S):
    nh = len(DIL_PAIRS) * DIL_HPG
    slopes, has_prev = [], []
    for gi, (_, dil) in enumerate(DIL_PAIRS):
        nbk = S // dil // DIL_BLK
        for seq in range(B * dil * DIL_HPG):
            head = gi * DIL_HPG + seq % DIL_HPG
            for n in range(nbk):
                slopes.append(dil * 2.0 ** (-ALIBI_MAX * (head + 1) / nh))
                has_prev.append(1.0 if n > 0 else 0.0)
    shape = (len(slopes), 1, 1)
    return jnp.asarray(np.array(slopes, np.float32).reshape(shape)), jnp.asarray(np.array(has_prev, np.float32).reshape(shape))


def _dil_blocks3(t):
    return t.reshape(-1, DIL_BLK, DIL_DIM)


def _dil_prev(t3):
    return jnp.concatenate([jnp.zeros_like(t3[:1]), t3[:-1]], axis=0)


def _dil_attn_fwd(qb, kb, vb, B, S, name):
    q3, k3, v3 = _dil_blocks3(qb), _dil_blocks3(kb), _dil_blocks3(vb)
    nbt = q3.shape[0]
    slope, has_prev = _dil_tables(B, S)

    def body(s_ref, h_ref, q_ref, kp_ref, kc_ref, vp_ref, vc_ref, o_ref, l_ref):
        o, l = _dil_block_fn(s_ref[...], h_ref[...], q_ref[...], kp_ref[...], kc_ref[...], vp_ref[...], vc_ref[...])
        o_ref[...] = o
        l_ref[...] = l

    blk = pl.BlockSpec((DIL_BPS, DIL_BLK, DIL_DIM), lambda m: (m, 0, 0))
    tab = pl.BlockSpec((DIL_BPS, 1, 1), lambda m: (m, 0, 0))
    o, l = pl.pallas_call(
        body, name=name, out_shape=[jax.ShapeDtypeStruct(q3.shape, F32)] * 2, grid=(nbt // DIL_BPS,),
        in_specs=[tab, tab] + [blk] * 5, out_specs=[blk, blk], compiler_params=_cparams(("parallel",)),
    )(slope, has_prev, q3, _dil_prev(k3), k3, _dil_prev(v3), v3)
    return o.reshape(qb.shape), l.reshape(qb.shape)


def _dil_attn_bwd(qb, kb, vb, do, dl, B, S, name):
    q3, k3, v3 = _dil_blocks3(qb), _dil_blocks3(kb), _dil_blocks3(vb)
    nbt = q3.shape[0]
    slope, has_prev = _dil_tables(B, S)

    def body(s_ref, h_ref, q_ref, kp_ref, kc_ref, vp_ref, vc_ref, do_ref, dl_ref, *outs):
        _, vjp = jax.vjp(functools.partial(_dil_block_fn, s_ref[...], h_ref[...]),
                         q_ref[...], kp_ref[...], kc_ref[...], vp_ref[...], vc_ref[...])
        for o_ref, g in zip(outs, vjp((do_ref[...], dl_ref[...]))):
            o_ref[...] = g

    blk = pl.BlockSpec((DIL_BPS, DIL_BLK, DIL_DIM), lambda m: (m, 0, 0))
    tab = pl.BlockSpec((DIL_BPS, 1, 1), lambda m: (m, 0, 0))
    outs = pl.pallas_call(
        body, name=name, out_shape=[jax.ShapeDtypeStruct(q3.shape, F32)] * 5, grid=(nbt // DIL_BPS,),
        in_specs=[tab, tab] + [blk] * 7, out_specs=[blk] * 5, compiler_params=_cparams(("parallel",)),
    )(slope, has_prev, q3, _dil_prev(k3), k3, _dil_prev(v3), v3, _dil_blocks3(do), _dil_blocks3(dl))
    return [t.reshape(qb.shape) for t in outs]


def _f_dil_merge(o0, o1, o2, l0, l1, l2):
    m = lax.stop_gradient(jnp.maximum(jnp.maximum(l0, l1), l2))
    e0, e1, e2 = jnp.exp(l0 - m), jnp.exp(l1 - m), jnp.exp(l2 - m)
    return (e0 * o0 + e1 * o1 + e2 * o2) / (e0 + e1 + e2)


def _dil_fwd(qkv, q_norm, k_norm, B, tag):
    T = qkv.shape[0]
    S = T // B
    Wd = qkv.shape[1] // 3
    nblk = Wd // LANES
    (qn,) = _ew_fwd(functools.partial(_f_qknorm, DIL_DIM ** -0.5), [(qkv, LANES, 0)], [], [q_norm],
                    [(Wd, F32, True)], ncb=nblk, name=f"{tag}_qnorm")
    (kn,) = _ew_fwd(functools.partial(_f_qknorm, 1.0), [(qkv, LANES, nblk)], [], [k_norm],
                    [(Wd, F32, True)], ncb=nblk, name=f"{tag}_knorm")
    qb, kb, vb = _dil_to_blocks(qn, B), _dil_to_blocks(kn, B), _dil_to_blocks(qkv[:, 2 * Wd:], B)
    ob, lb = _dil_attn_fwd(qb, kb, vb, B, S, f"{tag}_attn")
    o, l = _dil_from_blocks(ob, B), _dil_from_blocks(lb, B)
    gw = DIL_GW
    rows = [(o, gw, 0), (o, gw, 1), (o, gw, 2), (l, gw, 0), (l, gw, 1), (l, gw, 2)]
    (y,) = _ew_fwd(_f_dil_merge, rows, [], [], [(gw, BF16, False)], name=f"{tag}_merge")
    return y, (qkv, qb, kb, vb, o, l)


def _dil_bwd(dy, saved, q_norm, k_norm, B, tag):
    qkv, qb, kb, vb, o, l = saved
    T = qkv.shape[0]
    S = T // B
    Wd = qkv.shape[1] // 3
    nblk = Wd // LANES
    gw = DIL_GW
    rows = [(o, gw, 0), (o, gw, 1), (o, gw, 2), (l, gw, 0), (l, gw, 1), (l, gw, 2)]
    g = _ew_bwd(_f_dil_merge, rows, [], [], [dy], [True] * 6, name=f"{tag}_merge_bwd")
    do = _dil_to_blocks(jnp.concatenate(g[:3], axis=1), B)
    dl = _dil_to_blocks(jnp.concatenate(g[3:], axis=1), B)
    dq, dkp, dkc, dvp, dvc = _dil_attn_bwd(qb, kb, vb, do, dl, B, S, f"{tag}_attn_bwd")
    nxt = lambda t: jnp.concatenate([t[DIL_BLK:], jnp.zeros((DIL_BLK, DIL_DIM), t.dtype)], axis=0)
    dqn = _dil_from_blocks(dq, B)
    dkn = _dil_from_blocks(dkc + nxt(dkp), B)
    dv = _dil_from_blocks(dvc + nxt(dvp), B)
    dq_raw, dqw = _ew_bwd(functools.partial(_f_qknorm, DIL_DIM ** -0.5), [(qkv, LANES, 0)], [], [q_norm],
                          [(dqn, LANES, 0)], [True, True], ncb=nblk, name=f"{tag}_qnorm_bwd")
    dk_raw, dkw = _ew_bwd(functools.partial(_f_qknorm, 1.0), [(qkv, LANES, nblk)], [], [k_norm],
                          [(dkn, LANES, 0)], [True, True], ncb=nblk, name=f"{tag}_knorm_bwd")
    return jnp.concatenate([dq_raw, dk_raw, dv], axis=1), dqw, dkw


MESH_ID = pl.DeviceIdType.MESH
ANY = pl.BlockSpec(memory_space=pl.ANY)


def _my_place():
    return lax.axis_index("x"), lax.axis_index("y"), lax.axis_index("c")


def _flat_index(px, py, pc):
    return 4 * px + 2 * py + pc


def _all_gather(x, name):
    R_, C = x.shape

    def body(x_ref, out_ref, send_sems, recv_sems, local_sem):
        mx, my, mc = _my_place()
        me, sibling = (mx, my, mc), (mx, my, 1 - mc)
        chips = [(1 - mx, my), (mx, 1 - my), (1 - mx, 1 - my)]

        def rows(p):
            return out_ref.at[_flat_index(*p)]

        def copy(k, block, to, src=None):
            return pltpu.make_async_remote_copy(
                src_ref=rows(block) if src is None else src, dst_ref=rows(block),
                send_sem=send_sems.at[k], recv_sem=recv_sems.at[k], device_id=to, device_id_type=MESH_ID)

        mine = pltpu.make_async_copy(x_ref, rows(me), local_sem)
        mine.start()
        first = [copy(0, me, sibling, src=x_ref)]
        first += [copy(1 + j, me, (*chip, mc), src=x_ref) for j, chip in enumerate(chips)]
        for cp in first:
            cp.start()
        passed = [copy(4 + j, (*chip, mc), sibling) for j, chip in enumerate(chips)]
        for j, chip in enumerate(chips):
            copy(1 + j, (*chip, mc), me).wait_recv()
            passed[j].start()
        copy(0, sibling, me).wait_recv()
        for j, chip in enumerate(chips):
            copy(4 + j, (*chip, 1 - mc), me).wait_recv()
        for cp in first + passed:
            cp.wait_send()
        mine.wait()

    return pl.pallas_call(
        body, name=name, out_shape=jax.ShapeDtypeStruct((N_DEV, R_, C), x.dtype),
        in_specs=[ANY], out_specs=ANY,
        scratch_shapes=[pltpu.SemaphoreType.DMA((7,)), pltpu.SemaphoreType.DMA((7,)), pltpu.SemaphoreType.DMA],
        compiler_params=pltpu.CompilerParams(has_side_effects=True),
    )(x)


def _exchange_sibling(x, name):
    nchip, _, R_, C = x.shape

    def body(x_ref, out_ref, send_sems, recv_sems):
        mx, my, mc = _my_place()
        sibling = (mx, my, 1 - mc)
        copies = [pltpu.make_async_remote_copy(
            src_ref=x_ref.at[k, 1 - mc], dst_ref=out_ref.at[k], send_sem=send_sems.at[k], recv_sem=recv_sems.at[k],
            device_id=sibling, device_id_type=MESH_ID) for k in range(nchip)]
        for cp in copies:
            cp.start()
        for cp in copies:
            cp.wait_recv()
        for cp in copies:
            cp.wait_send()

    return pl.pallas_call(
        body, name=name, out_shape=jax.ShapeDtypeStruct((nchip, R_, C), x.dtype), in_specs=[ANY], out_specs=ANY,
        scratch_shapes=[pltpu.SemaphoreType.DMA((nchip,)), pltpu.SemaphoreType.DMA((nchip,))],
        compiler_params=pltpu.CompilerParams(has_side_effects=True),
    )(x)


def _exchange_chips(x, name):
    nchip, R_, C = x.shape

    def body(x_ref, out_ref, send_sems, recv_sems, local_sem):
        mx, my, mc = _my_place()
        mk = 2 * mx + my
        chips = [(1 - mx, my), (mx, 1 - my), (1 - mx, 1 - my)]
        mine = pltpu.make_async_copy(x_ref.at[mk], out_ref.at[mk], local_sem)
        mine.start()
        copies = [pltpu.make_async_remote_copy(
            src_ref=x_ref.at[2 * px + py], dst_ref=out_ref.at[mk], send_sem=send_sems.at[j], recv_sem=recv_sems.at[j],
            device_id=(px, py, mc), device_id_type=MESH_ID) for j, (px, py) in enumerate(chips)]
        for cp in copies:
            cp.start()
        for j, (px, py) in enumerate(chips):
            pltpu.make_async_remote_copy(
                src_ref=x_ref.at[mk], dst_ref=out_ref.at[2 * px + py], send_sem=send_sems.at[j],
                recv_sem=recv_sems.at[j], device_id=(px, py, mc), device_id_type=MESH_ID).wait_recv()
        for cp in copies:
            cp.wait_send()
        mine.wait()

    return pl.pallas_call(
        body, name=name, out_shape=jax.ShapeDtypeStruct(x.shape, x.dtype), in_specs=[ANY], out_specs=ANY,
        scratch_shapes=[pltpu.SemaphoreType.DMA((3,)), pltpu.SemaphoreType.DMA((3,)), pltpu.SemaphoreType.DMA],
        compiler_params=pltpu.CompilerParams(has_side_effects=True),
    )(x)


def _add_pieces(a, b, name, tr=64):
    n, R_, C = a.shape
    tr = _pick(R_, (tr, 32, 16, 8))

    def body(a_ref, b_ref, o_ref):
        o_ref[...] = (a_ref[...].astype(F32) + b_ref[...].astype(F32)).astype(o_ref.dtype)

    spec = pl.BlockSpec((1, tr, C), lambda k, i: (k, i, 0))
    return pl.pallas_call(
        body, name=name, out_shape=jax.ShapeDtypeStruct(a.shape, a.dtype), grid=(n, R_ // tr),
        in_specs=[spec, spec], out_specs=spec, compiler_params=_cparams(("parallel", "parallel")),
    )(a, b)


def _reduce_to_owner(buf, mc, tag):
    _, R_, C = buf.shape
    buf4 = buf.reshape(N_DEV // 2, 2, R_, C)
    from_sibling = _exchange_sibling(buf4, f"scatter_sib_{tag}")
    mine = lax.dynamic_index_in_dim(buf4, mc, axis=1, keepdims=False)
    chip_sums = _add_pieces(mine, from_sibling, f"add_sib_{tag}")
    return _sum0(_exchange_chips(chip_sums, f"scatter_chips_{tag}"), f"sum_grads_{tag}", tr=64)


def _sum0(x, name, tr=256):
    n, R_, C = x.shape
    tr = _pick(R_, (tr, 128, 64, 32, 16, 8))

    def body(x_ref, o_ref):
        acc = x_ref[0].astype(F32)
        for k in range(1, n):
            acc = acc + x_ref[k].astype(F32)
        o_ref[...] = acc

    return pl.pallas_call(
        body, name=name, out_shape=jax.ShapeDtypeStruct((R_, C), F32), grid=(R_ // tr,),
        in_specs=[pl.BlockSpec((n, tr, C), lambda i: (0, i, 0))], out_specs=pl.BlockSpec((tr, C), lambda i: (i, 0)),
        compiler_params=_cparams(("parallel",)),
    )(x)


PACK_ROWS = 256


def _pack(arrs, dtype, width):
    flat = jnp.concatenate([a.astype(dtype).reshape(-1) for a in arrs])
    quantum = width * PACK_ROWS
    pad = (-flat.shape[0]) % quantum
    if pad:
        flat = jnp.concatenate([flat, jnp.zeros((pad,), dtype)])
    return flat.reshape(-1, width)


def _unpack(flat, shapes):
    out, off = [], 0
    for s in shapes:
        n = int(np.prod(s))
        out.append(flat[..., off:off + n].reshape(flat.shape[:-1] + tuple(s)))
        off += n
    return out


def _ada_fwd(c_all, ada_w, bias, name):
    M, D = c_all.shape
    n = ada_w.shape[1]
    tn = _pick(n, (768, 512, 256, 128))

    def body(c_ref, w_ref, b_ref, o_ref):
        c_ = c_ref[...]
        a = (c_ * jax.nn.sigmoid(c_)).astype(BF16)
        o_ref[...] = jnp.dot(a, w_ref[...].astype(BF16), preferred_element_type=F32) + b_ref[...]

    return pl.pallas_call(
        body, name=name, out_shape=jax.ShapeDtypeStruct((M, n), F32), grid=(n // tn,),
        in_specs=[pl.BlockSpec((M, D), lambda j: (0, 0)), pl.BlockSpec((D, tn), lambda j: (0, j)),
                  pl.BlockSpec((1, tn), lambda j: (0, j))],
        out_specs=pl.BlockSpec((M, tn), lambda j: (0, j)), compiler_params=_cparams(("parallel",)),
    )(c_all, ada_w, bias)


def _ada_bwd(c_all, dmod, name):
    M, D = c_all.shape
    n = dmod.shape[1]
    tn = _pick(n, (768, 512, 256, 128))

    def body(c_ref, d_ref, o_ref):
        c_ = c_ref[...]
        a = (c_ * jax.nn.sigmoid(c_)).astype(BF16)
        o_ref[...] = lax.dot_general(a, d_ref[...].astype(BF16), (((0,), (0,)), ((), ())), preferred_element_type=F32)

    return pl.pallas_call(
        body, name=name, out_shape=jax.ShapeDtypeStruct((D, n), F32), grid=(n // tn,),
        in_specs=[pl.BlockSpec((M, D), lambda j: (0, 0)), pl.BlockSpec((M, tn), lambda j: (0, j))],
        out_specs=pl.BlockSpec((D, tn), lambda j: (0, j)), compiler_params=_cparams(("parallel",)),
    )(c_all, dmod)


def _loss_head(y, target, name, tr=256):
    T, D = y.shape

    def body(y_ref, t_ref, l_ref, d_ref):
        e = y_ref[...] - t_ref[...]
        d_ref[...] = e * (1.0 / D)
        part = jnp.sum(jnp.sum(e * e, axis=1, keepdims=True), axis=0, keepdims=True) * (0.5 / D)

        @pl.when(pl.program_id(0) == 0)
        def _():
            l_ref[...] = jnp.zeros_like(l_ref)

        l_ref[...] += jnp.broadcast_to(part, l_ref.shape)

    row = pl.BlockSpec((tr, D), lambda i: (i, 0))
    return pl.pallas_call(
        body, name=name, out_shape=[jax.ShapeDtypeStruct((1, LANES), F32), jax.ShapeDtypeStruct((T, D), F32)],
        grid=(T // tr,), in_specs=[row, row], out_specs=[pl.BlockSpec((1, LANES), lambda i: (0, 0)), row],
        compiler_params=_cparams(("arbitrary",)),
    )(y, target)


def _adamw(w, g, m, v, name):
    shape = w.shape
    C = shape[-1]
    R_ = int(np.prod(shape[:-1]))
    w2, g2, m2, v2 = [a.reshape(R_, C) for a in (w, g, m, v)]
    tr = _pick(R_, (256, 128, 64, 32, 16, 8)) if R_ > 8 else R_
    c1 = 1.0 / (1.0 - ADAM_B1 ** ADAM_STEP)
    c2 = 1.0 / (1.0 - ADAM_B2 ** ADAM_STEP)

    def body(w_ref, g_ref, m_ref, v_ref, d_ref, nm_ref, nv_ref):
        g_ = g_ref[...]
        nm = ADAM_B1 * m_ref[...] + (1.0 - ADAM_B1) * g_
        nv = ADAM_B2 * v_ref[...] + (1.0 - ADAM_B2) * (g_ * g_)
        d_ref[...] = -ADAM_LR * ((nm * c1) / (jnp.sqrt(nv * c2) + ADAM_EPS) + ADAM_WD * w_ref[...])
        nm_ref[...] = nm
        nv_ref[...] = nv

    spec = pl.BlockSpec((tr, C), lambda i: (i, 0))
    outs = pl.pallas_call(
        body, name=name, out_shape=[jax.ShapeDtypeStruct((R_, C), F32)] * 3, grid=(R_ // tr,),
        in_specs=[spec] * 4, out_specs=[spec] * 3, compiler_params=_cparams(("parallel",)),
    )(w2, g2, m2, v2)
    return [o.reshape(shape) for o in outs]


GDN_W = GDN_HEADS * GDN_DIM
IN_SPLITS = (3 * GDN_W, GDN_W, GDN_HEADS, GDN_HEADS, 768, 3 * 768, None)
IN_PAD_GATES = LANES - 2 * GDN_HEADS


def _f_merge(pa, pb, pc, ga, gb, gc):
    return jax.nn.sigmoid(ga) * pa + jax.nn.sigmoid(gb) * pb + jax.nn.sigmoid(gc) * pc


def _f_id_rmsmod(x, sh, sc, g):
    return x, _rms(x, g) * (1.0 + sc) + sh


def _in_layout(D):
    widths = [3 * GDN_W, GDN_W, LANES, 768, 3 * 768, 3 * D]
    offs = np.concatenate([[0], np.cumsum(widths)]).tolist()
    total = -(-offs[-1] // 768) * 768
    return offs, total


def _pad_w_in(w_in):
    D = w_in.shape[0]
    offs, total = _in_layout(D)
    cut = 4 * GDN_W + 2 * GDN_HEADS
    return jnp.concatenate([w_in[:, :cut], jnp.zeros((D, IN_PAD_GATES), w_in.dtype), w_in[:, cut:],
                            jnp.zeros((D, total - offs[-1]), w_in.dtype)], axis=1)


def _unpad_w_in(d):
    D = d.shape[0]
    offs, _ = _in_layout(D)
    cut = 4 * GDN_W + 2 * GDN_HEADS
    return jnp.concatenate([d[:, :cut], d[:, cut + IN_PAD_GATES:offs[-1]]], axis=1)


def _layer_fwd(x0, mod, W, sm, B, tag):
    T, D = x0.shape
    S = T // B
    sh1, sc1, g1, sh2, sc2, g2, sh3, sc3, g3 = mod
    n1, nm, n3 = sm["norm_ffn1"][None], sm["norm_mix"][None], sm["norm_ffn2"][None]
    (h1,) = _ew_fwd(_f_rmsmod, [x0], [sh1, sc1], [n1], [(D, BF16, False)], seq=S, name=f"{tag}_norm1")
    y1, sv1 = _ffn_fwd(h1, W["ffn1_w13"], W["ffn1_w2"], f"{tag}_ffn1")
    x1, h2 = _ew_fwd(functools.partial(_f_res_rmsmod, FFN_RES), [x0, y1], [g1, sh2, sc2], [nm],
                     [(D, F32, False), (D, BF16, False)], seq=S, name=f"{tag}_res1")
    P = _mm(h2, W["w_in"], name=f"{tag}_in")
    offs, _ = _in_layout(D)
    qkv_a, z, ba, u, qkv_c, gl = [P[:, offs[i]:offs[i + 1]] for i in range(6)]
    lane8 = lambda v: jnp.pad(v, (GDN_HEADS, LANES - 2 * GDN_HEADS))[None]
    gate_params = (lane8(sm["gdn_a_log"]), lane8(sm["gdn_dt_bias"]))
    out_norm = sm["gdn_out_norm"][None]
    y_a, sva = _gdn_fwd(qkv_a, z, ba, sm["gdn_conv"], gate_params, out_norm, B, f"{tag}_gdn")
    s5p = {k[3:]: v for k, v in sm.items() if k.startswith("s5_")}
    y_b, svb = _s5_fwd(u, s5p, W["s5_glu_w"], B, f"{tag}_s5")
    qn2, kn2 = jnp.tile(sm["dil_q_norm"], 2)[None], jnp.tile(sm["dil_k_norm"], 2)[None]
    y_c, svc = _dil_fwd(qkv_c, qn2, kn2, B, f"{tag}_dil")
    pa = _mm(y_a, W["w_branch_a"], name=f"{tag}_pa")
    pb = _mm(y_b, W["w_branch_b"], name=f"{tag}_pb")
    pc = _mm(y_c, W["w_branch_c"], name=f"{tag}_pc")
    mrows = [pa, pb, pc, (gl, D, 0), (gl, D, 1), (gl, D, 2)]
    (merged,) = _ew_fwd(_f_merge, mrows, [], [], [(D, BF16, False)], tr=128, name=f"{tag}_merge")
    mo = _mm(merged, W["w_out"], name=f"{tag}_out")
    x2, h3 = _ew_fwd(functools.partial(_f_res_rmsmod, 1.0), [x1, mo], [g2, sh3, sc3], [n3],
                     [(D, F32, False), (D, BF16, False)], seq=S, name=f"{tag}_res2")
    y3, sv3 = _ffn_fwd(h3, W["ffn2_w13"], W["ffn2_w2"], f"{tag}_ffn2")
    (x3,) = _ew_fwd(functools.partial(_f_res, FFN_RES), [x2, y3], [g3], [], [(D, F32, False)], seq=S,
                    name=f"{tag}_res3")
    saved = dict(x0=x0, x1=x1, x2=x2, y1=y1, y3=y3, mo=mo, h2=h2, sv1=sv1, sv3=sv3, sva=sva, svb=svb, svc=svc,
                 y_a=y_a, y_b=y_b, y_c=y_c, mrows=mrows, merged=merged, gate_params=gate_params, out_norm=out_norm,
                 s5p=s5p, qn2=qn2, kn2=kn2)
    return x3, saved


def _layer_bwd(dx3, sv, mod, W, sm, B, tag):
    T, D = dx3.shape
    S = T // B
    sh1, sc1, g1, sh2, sc2, g2, sh3, sc3, g3 = mod
    n1, nm, n3 = sm["norm_ffn1"][None], sm["norm_mix"][None], sm["norm_ffn2"][None]
    big, small = {}, {}
    dx2, dy3, dg3 = _ew_bwd(functools.partial(_f_res, FFN_RES), [sv["x2"], sv["y3"]], [g3], [], [dx3], [True] * 3,
                            seq=S, row_dtypes=[F32, BF16], name=f"{tag}_res3_bwd")
    dh3, big["ffn2_w13"], big["ffn2_w2"] = _ffn_bwd(dy3, sv["sv3"], W["ffn2_w13"], W["ffn2_w2"], f"{tag}_ffn2")
    dx1, dmo, dg2, dsh3, dsc3, dn3 = _ew_bwd(
        functools.partial(_f_res_rmsmod, 1.0), [sv["x1"], sv["mo"]], [g2, sh3, sc3], [n3], [dx2, dh3], [True] * 6,
        seq=S, row_dtypes=[F32, BF16], name=f"{tag}_res2_bwd")
    big["w_out"] = _mm(sv["merged"], dmo, ta=True, out_dtype=BF16, name=f"{tag}_out_dw")
    dmerged = _mm(dmo, W["w_out"], tb=True, name=f"{tag}_out_dx")
    dpa, dpb, dpc, dga, dgb, dgc = _ew_bwd(_f_merge, sv["mrows"], [], [], [dmerged], [True] * 6, tr=128,
                                           row_dtypes=[BF16] * 6, name=f"{tag}_merge_bwd")
    big["w_branch_a"] = _mm(sv["y_a"], dpa, ta=True, out_dtype=BF16, name=f"{tag}_pa_dw")
    big["w_branch_b"] = _mm(sv["y_b"], dpb, ta=True, out_dtype=BF16, name=f"{tag}_pb_dw")
    big["w_branch_c"] = _mm(sv["y_c"], dpc, ta=True, out_dtype=BF16, name=f"{tag}_pc_dw")
    dy_a = _mm(dpa, W["w_branch_a"], tb=True, name=f"{tag}_pa_dx")
    dy_b = _mm(dpb, W["w_branch_b"], tb=True, name=f"{tag}_pb_dx")
    dy_c = _mm(dpc, W["w_branch_c"], tb=True, name=f"{tag}_pc_dx")
    dqkv_a, dz, dba, gdn_small = _gdn_bwd(dy_a, sv["sva"], sm["gdn_conv"], sv["gate_params"], sv["out_norm"], B,
                                          f"{tag}_gdn")
    du, big["s5_glu_w"], s5_small = _s5_bwd(dy_b, sv["svb"], sv["s5p"], W["s5_glu_w"], B, f"{tag}_s5")
    dqkv_c, dqw, dkw = _dil_bwd(dy_c, sv["svc"], sv["qn2"], sv["kn2"], B, f"{tag}_dil")
    offs, total = _in_layout(D)
    dP = jnp.concatenate([t.astype(BF16) for t in (dqkv_a, dz, dba, du, dqkv_c, dga, dgb, dgc)]
                         + [jnp.zeros((T, total - offs[-1]), BF16)], axis=1)
    big["w_in"] = _mm(sv["h2"], dP, ta=True, out_dtype=BF16, name=f"{tag}_in_dw")
    dh2 = _mm(dP, W["w_in"], tb=True, name=f"{tag}_in_dx")
    dx0a, dy1, dg1, dsh2, dsc2, dnm = _ew_bwd(
        functools.partial(_f_res_rmsmod, FFN_RES), [sv["x0"], sv["y1"]], [g1, sh2, sc2], [nm], [dx1, dh2], [True] * 6,
        seq=S, row_dtypes=[F32, BF16], name=f"{tag}_res1_bwd")
    dh1, big["ffn1_w13"], big["ffn1_w2"] = _ffn_bwd(dy1, sv["sv1"], W["ffn1_w13"], W["ffn1_w2"], f"{tag}_ffn1")
    dx0, dsh1, dsc1, dn1 = _ew_bwd(_f_id_rmsmod, [sv["x0"]], [sh1, sc1], [n1], [dx0a, dh1], [True] * 4, seq=S,
                                   name=f"{tag}_norm1_bwd")
    half = DIL_DIM
    small.update(norm_ffn1=dn1[0], norm_mix=dnm[0], norm_ffn2=dn3[0], gdn_conv=gdn_small["conv"],
                 gdn_a_log=gdn_small["a_log"][0, GDN_HEADS:2 * GDN_HEADS],
                 gdn_dt_bias=gdn_small["dt_bias"][0, GDN_HEADS:2 * GDN_HEADS], gdn_out_norm=gdn_small["out_norm"][0],
                 dil_q_norm=dqw[0, :half] + dqw[0, half:], dil_k_norm=dkw[0, :half] + dkw[0, half:])
    small.update({"s5_" + k: v for k, v in s5_small.items()})
    dmod = [dsh1, dsc1, dg1, dsh2, dsc2, dg2, dsh3, dsc3, dg3]
    return dx0, big, small, dmod


WEIGHTS = ['ada_w', 'ada_b', 'norm_ffn1', 'ffn1_w1', 'ffn1_w3', 'ffn1_w2', 'norm_mix', 'w_in', 'gdn_conv', 'gdn_a_log',
           'gdn_dt_bias', 'gdn_out_norm', 's5_a_re', 's5_a_im', 's5_b_re', 's5_b_im', 's5_c_re', 's5_c_im', 's5_d',
           's5_log_step', 's5_glu_w', 's5_glu_b', 'dil_q_norm', 'dil_k_norm', 'w_branch_a', 'w_branch_b', 'w_branch_c',
           'w_out', 'norm_ffn2', 'ffn2_w1', 'ffn2_w3', 'ffn2_w2']
BIG = dict(ffn1_w1=True, ffn1_w3=True, ffn1_w2=False, w_in=True, s5_glu_w=True, w_branch_a=True, w_branch_b=True,
           w_branch_c=True, w_out=False, ffn2_w1=True, ffn2_w3=True, ffn2_w2=False)
SMALL = ['norm_ffn1', 'norm_mix', 'norm_ffn2', 'gdn_conv', 'gdn_a_log', 'gdn_dt_bias', 'gdn_out_norm', 's5_a_re',
         's5_a_im', 's5_b_re', 's5_b_im', 's5_c_re', 's5_c_im', 's5_d', 's5_log_step', 's5_glu_b', 'dil_q_norm',
         'dil_k_norm']


def _full_from_shards(g, cols):
    n, r, c = g.shape
    return jnp.transpose(g, (1, 0, 2)).reshape(r, n * c) if cols else g.reshape(n * r, c)


def _shards_from_full(w, cols):
    if cols:
        r, nc = w.shape
        return jnp.transpose(w.reshape(r, N_DEV, nc // N_DEV), (1, 0, 2))
    nr, c = w.shape
    return w.reshape(N_DEV, nr // N_DEV, c)


def kernel(x, c, ada_w, ada_b, norm_ffn1, ffn1_w1, ffn1_w3, ffn1_w2, norm_mix, w_in, gdn_conv, gdn_a_log, gdn_dt_bias, gdn_out_norm, s5_a_re, s5_a_im, s5_b_re, s5_b_im, s5_c_re, s5_c_im, s5_d, s5_log_step, s5_glu_w, s5_glu_b, dil_q_norm, dil_k_norm, w_branch_a, w_branch_b, w_branch_c, w_out, norm_ffn2, ffn2_w1, ffn2_w3, ffn2_w2, loss_target, m_ada_w, m_ada_b, m_norm_ffn1, m_ffn1_w1, m_ffn1_w3, m_ffn1_w2, m_norm_mix, m_w_in, m_gdn_conv, m_gdn_a_log, m_gdn_dt_bias, m_gdn_out_norm, m_s5_a_re, m_s5_a_im, m_s5_b_re, m_s5_b_im, m_s5_c_re, m_s5_c_im, m_s5_d, m_s5_log_step, m_s5_glu_w, m_s5_glu_b, m_dil_q_norm, m_dil_k_norm, m_w_branch_a, m_w_branch_b, m_w_branch_c, m_w_out, m_norm_ffn2, m_ffn2_w1, m_ffn2_w3, m_ffn2_w2, v_ada_w, v_ada_b, v_norm_ffn1, v_ffn1_w1, v_ffn1_w3, v_ffn1_w2, v_norm_mix, v_w_in, v_gdn_conv, v_gdn_a_log, v_gdn_dt_bias, v_gdn_out_norm, v_s5_a_re, v_s5_a_im, v_s5_b_re, v_s5_b_im, v_s5_c_re, v_s5_c_im, v_s5_d, v_s5_log_step, v_s5_glu_w, v_s5_glu_b, v_dil_q_norm, v_dil_k_norm, v_w_branch_a, v_w_branch_b, v_w_branch_c, v_w_out, v_norm_ffn2, v_ffn2_w1, v_ffn2_w3, v_ffn2_w2):
    env = dict(locals())
    w = {n: env[n] for n in WEIGHTS}
    m = {n: env["m_" + n] for n in WEIGHTS}
    v = {n: env["v_" + n] for n in WEIGHTS}
    L = ada_w.shape[0]
    B, S, D = x.shape
    T = B * S
    me = _flat_index(*_my_place())

    big_keys = [(n, l) for l in range(L) for n in BIG]
    groups = {}
    for n, l in big_keys:
        r, cc = w[n].shape[1:]
        groups.setdefault((BIG[n], r if BIG[n] else cc), []).append((n, l))
    shards = {}
    for (cols, dim), keys in groups.items():
        buf = jnp.concatenate([w[n][l].astype(BF16) for n, l in keys], axis=1 if cols else 0)
        got = _all_gather(buf, f"gather_weights_{'c' if cols else 'r'}{dim}")
        off = 0
        for n, l in keys:
            k = w[n].shape[2] if cols else w[n].shape[1]
            shards[(n, l)] = got[:, :, off:off + k] if cols else got[:, off:off + k, :]
            off += k
    small_in = _pack([jnp.pad(c, ((0, SUBLANES - B), (0, 0))), gdn_conv], F32, LANES)
    c_g, conv_g = _unpack(_all_gather(small_in, "gather_cond").reshape(N_DEV, -1),
                          [(SUBLANES, D), gdn_conv.shape])
    c_all = c_g[:, :B].reshape(N_DEV * B, D)
    conv_full = jnp.transpose(conv_g, (1, 2, 0, 3)).reshape(L, GDN_CONV, -1)
    Ws = []
    for l in range(L):
        full = {n: _full_from_shards(shards[(n, l)], BIG[n]) for n in BIG}
        Wl = {n: full[n] for n in ("s5_glu_w", "w_branch_a", "w_branch_b", "w_branch_c", "w_out", "ffn1_w2", "ffn2_w2")}
        Wl["ffn1_w13"] = jnp.concatenate([full["ffn1_w1"], full["ffn1_w3"]], axis=1)
        Wl["ffn2_w13"] = jnp.concatenate([full["ffn2_w1"], full["ffn2_w3"]], axis=1)
        Wl["w_in"] = _pad_w_in(full["w_in"])
        Ws.append(Wl)
    sms = [dict({n: w[n][l] for n in SMALL}, gdn_conv=conv_full[l]) for l in range(L)]

    n_ada = ada_w.shape[2]
    bias = lax.dynamic_slice(ada_b, (0, me * n_ada), (L, n_ada))
    mod_cols = jnp.concatenate([_ada_fwd(c_all, ada_w[l], bias[l][None], f"ada{l}") for l in range(L)], axis=0)
    mod_g = _all_gather(mod_cols, "gather_mod").reshape(N_DEV, L, N_DEV * B, n_ada)
    mod_mine = lax.dynamic_slice(mod_g, (0, 0, me * B, 0), (N_DEV, L, B, n_ada))
    mod_mine = jnp.transpose(mod_mine, (1, 2, 0, 3)).reshape(L, B, N_DEV * n_ada)
    mods = [[mod_mine[l][:, None, k * D:(k + 1) * D] for k in range(9)] for l in range(L)]

    h = x.reshape(T, D)
    saved = []
    for l in range(L):
        h, sv = _layer_fwd(h, mods[l], Ws[l], sms[l], B, f"l{l}")
        saved.append(sv)
    loss_row, dh = _loss_head(h, loss_target.reshape(T, D), "loss")
    loss = lax.psum(loss_row[0, 0], ("x", "y", "c"))
    bigs, smalls, dmods = [None] * L, [None] * L, [None] * L
    for l in reversed(range(L)):
        dh, bigs[l], smalls[l], dmods[l] = _layer_bwd(dh, saved[l], mods[l], Ws[l], sms[l], B, f"l{l}")
    grad_x = dh.reshape(B, S, D)

    def full_grad(n, l):
        if n in ("ffn1_w1", "ffn1_w3", "ffn2_w1", "ffn2_w3"):
            both = bigs[l][n[:4] + "_w13"]
            F_ = both.shape[1] // 2
            return both[:, :F_] if n.endswith("w1") else both[:, F_:]
        if n == "w_in":
            return _unpad_w_in(bigs[l]["w_in"])
        return bigs[l][n]

    g = dict()
    for (cols, dim), keys in groups.items():
        tag = f"{'c' if cols else 'r'}{dim}"
        buf = jnp.concatenate([_shards_from_full(full_grad(n, l), cols) for n, l in keys], axis=2 if cols else 1)
        summed = _reduce_to_owner(buf, lax.axis_index("c"), tag)
        off = 0
        for n, l in keys:
            k = w[n].shape[2] if cols else w[n].shape[1]
            g.setdefault(n, [None] * L)[l] = summed[:, off:off + k] if cols else summed[off:off + k, :]
            off += k
    g = {n: jnp.stack(ts) for n, ts in g.items()}

    small_keys = [(n, l) for l in range(L) for n in SMALL]
    small_flat = _pack([smalls[l][n] for n, l in small_keys], F32, LANES)
    small_sum = _sum0(_all_gather(small_flat, "gather_small_grads"), "sum_small_grads")
    small_full = {}
    for (n, l), t in zip(small_keys, _unpack(small_sum.reshape(-1), [smalls[l][n].shape for n, l in small_keys])):
        small_full.setdefault(n, [None] * L)[l] = t
    for n, ts in small_full.items():
        g[n] = jnp.stack(ts)
    n_conv = gdn_conv.shape[2]
    g["gdn_conv"] = lax.dynamic_slice(g["gdn_conv"], (0, 0, me * n_conv), (L, GDN_CONV, n_conv))

    dmod_mine = jnp.stack([jnp.concatenate([d[:, 0] for d in dmods[l]], axis=1) for l in range(L)])
    dmod_in = jnp.pad(dmod_mine.reshape(L * B, -1), ((0, SUBLANES - L * B), (0, 0)))
    dmod_g = _all_gather(dmod_in, "gather_dmod")[:, :L * B].reshape(N_DEV, L, B, -1)
    dmod_all = jnp.transpose(dmod_g, (1, 0, 2, 3)).reshape(L, N_DEV * B, -1)
    g["ada_b"] = _sum0(dmod_all.reshape(L, N_DEV * B, -1, LANES).transpose(1, 0, 2, 3).reshape(N_DEV * B, -1, LANES),
                       "sum_ada_b").reshape(L, -1)
    dmod_cols = lax.dynamic_slice(dmod_all, (0, 0, me * n_ada), (L, N_DEV * B, n_ada))
    g["ada_w"] = jnp.stack([_ada_bwd(c_all, dmod_cols[l], f"ada{l}_bwd") for l in range(L)])

    upd = {n: _adamw(w[n], g[n], m[n], v[n], f"adamw_{n}") for n in WEIGHTS}
    return (loss, grad_x, *[g[n] for n in WEIGHTS], *[upd[n][0] for n in WEIGHTS],
            *[upd[n][1] for n in WEIGHTS], *[upd[n][2] for n in WEIGHTS])
```

```python
import functools
import math

import jax
import jax.numpy as jnp
import numpy as np
from jax import lax
from jax.experimental import pallas as pl
from jax.experimental.pallas import tpu as pltpu

F32 = jnp.float32
BF16 = jnp.bfloat16

LANES = 128
SUBLANES = 8
VMEM_LIMIT = 56 * 1024 * 1024

N_DEV = 8
EPS = 1e-6
FFN_RES = 0.5
GDN_HEADS = 8
GDN_DIM = 128
GDN_CONV = 4
GDN_CHUNK = 128
S5_GROUP = 16
S5_STATE = 64
S5_MAX_RE = -1e-4
S5_TCHUNK = 512
DIL_PAIRS = ((128, 1), (512, 4), (2048, 16))
DIL_HPG = 4
DIL_DIM = 64
DIL_BLK = 128
ALIBI_MAX = 8.0
ADAM_LR, ADAM_B1, ADAM_B2, ADAM_EPS, ADAM_WD, ADAM_STEP = 0.001, 0.9, 0.999, 1e-08, 0.01, 10

HI = lax.Precision.HIGHEST
HI3 = lax.Precision.HIGH


def _cparams(sem=None, **kw):
    return pltpu.CompilerParams(dimension_semantics=sem, vmem_limit_bytes=VMEM_LIMIT, **kw)


def _pick(n, cands):
    for c in cands:
        if n % c == 0:
            return c
    return n


def _mm(a, b, *, ta=False, tb=False, out_dtype=F32, name):
    M, K = (a.shape[1], a.shape[0]) if ta else a.shape
    N = b.shape[0] if tb else b.shape[1]
    assert (b.shape[1] if tb else b.shape[0]) == K, (a.shape, b.shape, ta, tb)
    tm = _pick(M, (1024, 512, 256, 128))
    tn = _pick(N, (1024, 768, 512, 384, 256, 128))
    tk = _pick(K, (2048, 1536, 1408, 1024, 768, 512, 256, 128))
    nk = K // tk

    def body(a_ref, b_ref, o_ref, acc_ref):
        k = pl.program_id(2)

        @pl.when(k == 0)
        def _():
            acc_ref[...] = jnp.zeros_like(acc_ref)

        dn = (((0 if ta else 1,), (1 if tb else 0,)), ((), ()))
        acc_ref[...] += lax.dot_general(a_ref[...], b_ref[...], dn, preferred_element_type=F32)

        @pl.when(k == nk - 1)
        def _():
            o_ref[...] = acc_ref[...].astype(o_ref.dtype)

    a_spec = pl.BlockSpec((tk, tm), lambda i, j, k: (k, i)) if ta else pl.BlockSpec((tm, tk), lambda i, j, k: (i, k))
    b_spec = pl.BlockSpec((tn, tk), lambda i, j, k: (j, k)) if tb else pl.BlockSpec((tk, tn), lambda i, j, k: (k, j))
    return pl.pallas_call(
        body, name=name,
        out_shape=jax.ShapeDtypeStruct((M, N), out_dtype),
        grid=(M // tm, N // tn, nk),
        in_specs=[a_spec, b_spec],
        out_specs=pl.BlockSpec((tm, tn), lambda i, j, k: (i, j)),
        scratch_shapes=[pltpu.VMEM((tm, tn), F32)],
        compiler_params=_cparams(("parallel", "parallel", "arbitrary")),
    )(a, b)


def _norm_arg(a):
    return a if isinstance(a, tuple) else (a, None, 0)


def _ew_specs(rows, exs, ws, tr, tpe):
    specs = []
    for arr, cw, off in rows:
        if cw is None:
            specs.append(pl.BlockSpec((tr, arr.shape[1]), lambda j, i: (i, 0)))
        else:
            specs.append(pl.BlockSpec((tr, cw), lambda j, i, off=off: (i, j + off)))
    for arr, cw, off in exs:
        if cw is None:
            specs.append(pl.BlockSpec((1, 1, arr.shape[2]), lambda j, i: (i // tpe, 0, 0)))
        else:
            specs.append(pl.BlockSpec((1, 1, cw), lambda j, i, off=off: (i // tpe, 0, j + off)))
    for arr, cw, off in ws:
        if cw is None:
            specs.append(pl.BlockSpec((1, arr.shape[1]), lambda j, i: (0, 0)))
        else:
            specs.append(pl.BlockSpec((1, cw), lambda j, i, off=off: (0, j + off)))
    return specs


def _ew_fwd(fn, rows, exs, ws, outs, *, ncb=1, tr=256, seq=None, name):
    rows, exs, ws = [list(map(_norm_arg, g)) for g in (rows, exs, ws)]
    T = rows[0][0].shape[0]
    seq = seq or T
    tr = min(tr, seq)
    tpe = seq // tr
    nr, ne, nw = len(rows), len(exs), len(ws)

    def body(*refs):
        ins = [r[...].astype(F32) for r in refs[:nr]]
        ins += [r[0].astype(F32) for r in refs[nr:nr + ne]]
        ins += [r[...].astype(F32) for r in refs[nr + ne:nr + ne + nw]]
        res = fn(*ins)
        if not isinstance(res, (tuple, list)):
            res = (res,)
        for o_ref, v in zip(refs[nr + ne + nw:], res):
            o_ref[...] = v.astype(o_ref.dtype)

    out_shape, out_specs = [], []
    for width, dtype, blocked in outs:
        out_shape.append(jax.ShapeDtypeStruct((T, width), dtype))
        if blocked:
            out_specs.append(pl.BlockSpec((tr, width // ncb), lambda j, i: (i, j)))
        else:
            out_specs.append(pl.BlockSpec((tr, width), lambda j, i: (i, 0)))
    res = pl.pallas_call(
        body, name=name, out_shape=out_shape, grid=(ncb, T // tr),
        in_specs=_ew_specs(rows, exs, ws, tr, tpe), out_specs=out_specs,
        compiler_params=_cparams(("parallel", "parallel")),
    )(*[a[0] for a in rows + exs + ws])
    return res


def _ew_bwd(fn, rows, exs, ws, douts, need, *, ncb=1, tr=256, seq=None, row_dtypes=None, name):
    rows, exs, ws, douts = [list(map(_norm_arg, g)) for g in (rows, exs, ws, douts)]
    T = rows[0][0].shape[0]
    seq = seq or T
    tr = min(tr, seq)
    tpe = seq // tr
    nrt = T // tr
    nr, ne, nw, nd = len(rows), len(exs), len(ws), len(douts)
    nin = nr + ne + nw
    args = rows + exs + ws
    row_dtypes = row_dtypes or [F32] * nr
    for k, (arr, cw, off) in enumerate(exs):
        assert not (need[nr + k] and cw is None and ncb > 1)

    def body(*refs):
        j, i = pl.program_id(0), pl.program_id(1)
        ins = [r[...].astype(F32) for r in refs[:nr]]
        ins += [r[0].astype(F32) for r in refs[nr:nr + ne]]
        ins += [r[...].astype(F32) for r in refs[nr + ne:nin]]
        cts = [r[...].astype(F32) for r in refs[nin:nin + nd]]
        res, vjp = jax.vjp(fn, *ins)
        if isinstance(res, (tuple, list)):
            grads = vjp(tuple(cts))
        else:
            grads = vjp(cts[0])
        o = nin + nd
        for k in range(nin):
            if not need[k]:
                continue
            o_ref, g = refs[o], grads[k]
            o += 1
            if k < nr:
                o_ref[...] = g.astype(o_ref.dtype)
            elif k < nr + ne:
                first = (i % tpe) == 0

                @pl.when(first)
                def _(o_ref=o_ref, g=g):
                    o_ref[0] = g

                @pl.when(jnp.logical_not(first))
                def _(o_ref=o_ref, g=g):
                    o_ref[0] += g
            else:
                blocked = args[k][1] is not None
                first = (i == 0) if blocked else jnp.logical_and(i == 0, j == 0)

                @pl.when(first)
                def _(o_ref=o_ref, g=g):
                    o_ref[...] = g

                @pl.when(jnp.logical_not(first))
                def _(o_ref=o_ref, g=g):
                    o_ref[...] += g

    in_specs = _ew_specs(rows, exs, ws, tr, tpe) + _ew_specs(douts, [], [], tr, tpe)
    out_shape, out_specs = [], []
    all_specs = _ew_specs(rows, exs, ws, tr, tpe)
    for k in range(nin):
        if not need[k]:
            continue
        arr, cw, off = args[k]
        if k < nr and cw is not None:
            out_shape.append(jax.ShapeDtypeStruct((T, ncb * cw), row_dtypes[k]))
            out_specs.append(pl.BlockSpec((tr, cw), lambda j, i: (i, j)))
        elif k >= nr and cw is not None:
            assert off == 0 and arr.shape[-1] == ncb * cw
            out_shape.append(jax.ShapeDtypeStruct(arr.shape, F32))
            out_specs.append(all_specs[k])
        else:
            out_shape.append(jax.ShapeDtypeStruct(arr.shape, row_dtypes[k] if k < nr else F32))
            out_specs.append(all_specs[k])
    res = pl.pallas_call(
        body, name=name, out_shape=out_shape, grid=(ncb, nrt),
        in_specs=in_specs, out_specs=out_specs,
        compiler_params=_cparams(("arbitrary", "arbitrary")),
    )(*[a[0] for a in args + douts])
    return res


def _rms(x, g):
    return x * lax.rsqrt(jnp.mean(x * x, axis=-1, keepdims=True) + EPS) * g


def _f_rmsmod(x, sh, sc, g):
    return _rms(x, g) * (1.0 + sc) + sh


def _f_swiglu(a, b):
    return a * jax.nn.sigmoid(a) * b


def _f_res(res, x, y, gate):
    return x + res * gate * y


def _f_res_rmsmod(res, x, y, gate, sh, sc, g):
    x1 = x + res * gate * y
    return x1, _rms(x1, g) * (1.0 + sc) + sh


def _mm_sh(step, ins, in_specs, out_shape, out_spec, acc_shape, grid, name):
    nk = grid[2]
    n = len(ins)

    def body(*refs):
        o_ref, acc_ref = refs[n], refs[n + 1]
        k = pl.program_id(2)

        @pl.when(k == 0)
        def _():
            acc_ref[...] = jnp.zeros_like(acc_ref)

        step(k, acc_ref, *refs[:n])

        @pl.when(k == nk - 1)
        def _():
            if len(o_ref.shape) == 3:
                o_ref[0] = acc_ref[...].astype(o_ref.dtype)
            else:
                o_ref[...] = acc_ref[...].astype(o_ref.dtype)

    return pl.pallas_call(
        body, name=name, out_shape=out_shape, grid=grid, in_specs=in_specs, out_specs=out_spec,
        scratch_shapes=[pltpu.VMEM(acc_shape, F32)],
        compiler_params=_cparams(("parallel", "parallel", "arbitrary")),
    )(*ins)


def _dg(a, b, ca, cb):
    return lax.dot_general(a, b, (((ca,), (cb,)), ((), ())), preferred_element_type=F32)


def _ffn_tiles(T, D):
    return _pick(T, (1024, 512, 256, 128)), _pick(D, (1024, 512, 256, 128)), _pick(D, (2048, 1024, 512, 256, 128)), \
        _pick(T, (2048, 1024, 512, 256, 128))


def _ffn_up(h, ws, name):
    n, D, c = ws.shape
    T = h.shape[0]
    tm, _, tkd, _ = _ffn_tiles(T, D)

    def step(k, acc, a_ref, b_ref):
        acc[...] += _dg(a_ref[...], b_ref[0], 1, 0)

    return _mm_sh(step, [h, ws],
                  [pl.BlockSpec((tm, tkd), lambda i, j, k: (i, k)), pl.BlockSpec((1, tkd, c), lambda i, j, k: (j, k, 0))],
                  jax.ShapeDtypeStruct((n, T, c), F32), pl.BlockSpec((1, tm, c), lambda i, j, k: (j, i, 0)),
                  (tm, c), (T // tm, n, D // tkd), name)


def _ffn_down(s3, w2s, name):
    n, T, c = s3.shape
    D = w2s.shape[2]
    tm, tn, _, _ = _ffn_tiles(T, D)

    def step(k, acc, a_ref, b_ref):
        acc[...] += _dg(a_ref[0], b_ref[0], 1, 0)

    return _mm_sh(step, [s3, w2s],
                  [pl.BlockSpec((1, tm, c), lambda i, j, k: (k, i, 0)), pl.BlockSpec((1, c, tn), lambda i, j, k: (k, 0, j))],
                  jax.ShapeDtypeStruct((T, D), F32), pl.BlockSpec((tm, tn), lambda i, j, k: (i, j)),
                  (tm, tn), (T // tm, D // tn, n), name)


def _ffn_down_dx(dy, w2s, name):
    n, c, D = w2s.shape
    T = dy.shape[0]
    tm, _, tkd, _ = _ffn_tiles(T, D)

    def step(k, acc, a_ref, b_ref):
        acc[...] += _dg(a_ref[...], b_ref[0], 1, 1)

    return _mm_sh(step, [dy, w2s],
                  [pl.BlockSpec((tm, tkd), lambda i, j, k: (i, k)), pl.BlockSpec((1, c, tkd), lambda i, j, k: (j, 0, k))],
                  jax.ShapeDtypeStruct((n, T, c), F32), pl.BlockSpec((1, tm, c), lambda i, j, k: (j, i, 0)),
                  (tm, c), (T // tm, n, D // tkd), name)


def _ffn_down_dw(s3, dy, name):
    n, T, c = s3.shape
    D = dy.shape[1]
    _, tn, _, tkt = _ffn_tiles(T, D)

    def step(k, acc, a_ref, b_ref):
        acc[...] += _dg(a_ref[0], b_ref[...], 0, 0)

    return _mm_sh(step, [s3, dy],
                  [pl.BlockSpec((1, tkt, c), lambda i, j, k: (i, k, 0)), pl.BlockSpec((tkt, tn), lambda i, j, k: (k, j))],
                  jax.ShapeDtypeStruct((n, c, D), BF16), pl.BlockSpec((1, c, tn), lambda i, j, k: (i, 0, j)),
                  (c, tn), (n, D // tn, T // tkt), name)


def _ffn_up_dw(h, d3, name):
    n, T, c = d3.shape
    D = h.shape[1]
    _, tn, _, tkt = _ffn_tiles(T, D)

    def step(k, acc, a_ref, b_ref):
        acc[...] += _dg(a_ref[...], b_ref[0], 0, 0)

    return _mm_sh(step, [h, d3],
                  [pl.BlockSpec((tkt, tn), lambda i, j, k: (k, i)), pl.BlockSpec((1, tkt, c), lambda i, j, k: (j, k, 0))],
                  jax.ShapeDtypeStruct((n, D, c), BF16), pl.BlockSpec((1, tn, c), lambda i, j, k: (j, i, 0)),
                  (tn, c), (D // tn, n, T // tkt), name)


def _ffn_up_dx(da3, db3, w1s, w3s, name):
    n, T, c = da3.shape
    D = w1s.shape[1]
    tm, tn, _, _ = _ffn_tiles(T, D)

    def step(k, acc, da_ref, db_ref, w1_ref, w3_ref):
        @pl.when(k < n)
        def _():
            acc[...] += _dg(da_ref[0], w1_ref[0], 1, 1)

        @pl.when(k >= n)
        def _():
            acc[...] += _dg(db_ref[0], w3_ref[0], 1, 1)

    lo = lambda k: jnp.minimum(k, n - 1)
    hi = lambda k: jnp.maximum(k - n, 0)
    return _mm_sh(step, [da3, db3, w1s, w3s],
                  [pl.BlockSpec((1, tm, c), lambda i, j, k: (lo(k), i, 0)), pl.BlockSpec((1, tm, c), lambda i, j, k: (hi(k), i, 0)),
                   pl.BlockSpec((1, tn, c), lambda i, j, k: (lo(k), j, 0)), pl.BlockSpec((1, tn, c), lambda i, j, k: (hi(k), j, 0))],
                  jax.ShapeDtypeStruct((T, D), F32), pl.BlockSpec((tm, tn), lambda i, j, k: (i, j)),
                  (tm, tn), (T // tm, D // tn, 2 * n), name)


def _ffn_act(a3, b3, name, tr=512):
    n, T, c = a3.shape
    tr = _pick(T, (tr, 256, 128))

    def body(a_ref, b_ref, o_ref):
        o_ref[...] = _f_swiglu(a_ref[...], b_ref[...]).astype(o_ref.dtype)

    spec = pl.BlockSpec((1, tr, c), lambda j, i: (j, i, 0))
    return pl.pallas_call(body, name=name, out_shape=jax.ShapeDtypeStruct((n, T, c), BF16), grid=(n, T // tr),
                          in_specs=[spec, spec], out_specs=spec, compiler_params=_cparams(("parallel", "parallel")))(a3, b3)


def _ffn_act_bwd(a3, b3, ds3, name, tr=512):
    n, T, c = a3.shape
    tr = _pick(T, (tr, 256, 128))

    def body(a_ref, b_ref, d_ref, da_ref, db_ref):
        _, vjp = jax.vjp(_f_swiglu, a_ref[...], b_ref[...])
        da, db = vjp(d_ref[...])
        da_ref[...] = da.astype(da_ref.dtype)
        db_ref[...] = db.astype(db_ref.dtype)

    spec = pl.BlockSpec((1, tr, c), lambda j, i: (j, i, 0))
    return pl.pallas_call(body, name=name, out_shape=[jax.ShapeDtypeStruct((n, T, c), BF16)] * 2, grid=(n, T // tr),
                          in_specs=[spec] * 3, out_specs=[spec, spec],
                          compiler_params=_cparams(("parallel", "parallel")))(a3, b3, ds3)


def _ffn_fwd(h, w1s, w3s, w2s, tag):
    a3 = _ffn_up(h, w1s, f"{tag}_up1")
    b3 = _ffn_up(h, w3s, f"{tag}_up3")
    s3 = _ffn_act(a3, b3, f"{tag}_act")
    return _ffn_down(s3, w2s, f"{tag}_down"), (h, a3, b3, s3)


def _ffn_bwd(dy, saved, w1s, w3s, w2s, tag):
    h, a3, b3, s3 = saved
    ds3 = _ffn_down_dx(dy, w2s, f"{tag}_down_dx")
    dw2s = _ffn_down_dw(s3, dy, f"{tag}_down_dw")
    da3, db3 = _ffn_act_bwd(a3, b3, ds3, f"{tag}_act_bwd")
    dw1s = _ffn_up_dw(h, da3, f"{tag}_up1_dw")
    dw3s = _ffn_up_dw(h, db3, f"{tag}_up3_dw")
    dh = _ffn_up_dx(da3, db3, w1s, w3s, f"{tag}_up_dx")
    return dh, dw1s, dw3s, dw2s


def _shift_down(x, j):
    if j == 0:
        return x
    row = lax.broadcasted_iota(jnp.int32, x.shape, 0)
    return jnp.where(row >= j, pltpu.roll(x, j, 0), 0.0)


def _shift_up(x, j):
    if j == 0:
        return x
    n = x.shape[0]
    row = lax.broadcasted_iota(jnp.int32, x.shape, 0)
    return jnp.where(row < n - j, pltpu.roll(x, n - j, 0), 0.0)


def _gdn_post_conv(kind, y):
    s = y * jax.nn.sigmoid(y)
    if kind == "v":
        return s
    n = lax.rsqrt(jnp.sum(s * s, axis=-1, keepdims=True) + EPS)
    return s * n * (GDN_DIM ** -0.5 if kind == "q" else 1.0)


def _conv_taps(w_ref):
    return [w_ref[k:k + 1, :] for k in range(GDN_CONV)]


def _gdn_conv(x, w):
    y = w[GDN_CONV - 1] * x
    for k in range(GDN_CONV - 1):
        y = y + w[k] * _shift_down(x, GDN_CONV - 1 - k)
    return y


def _gdn_pre_fwd(proj, conv_w, kind, first_block, B, name):
    T = proj.shape[0]
    S = T // B
    nh = GDN_HEADS

    def body(x_ref, w_ref, o_ref):
        y = _gdn_conv(x_ref[...], _conv_taps(w_ref))
        o_ref[...] = _gdn_post_conv(kind, y)

    return pl.pallas_call(
        body, name=name, out_shape=jax.ShapeDtypeStruct((T, nh * GDN_DIM), F32), grid=(nh, B),
        in_specs=[pl.BlockSpec((S, GDN_DIM), lambda c, b: (b, c + first_block)),
                  pl.BlockSpec((GDN_CONV, GDN_DIM), lambda c, b: (0, c))],
        out_specs=pl.BlockSpec((S, GDN_DIM), lambda c, b: (b, c)),
        compiler_params=_cparams(("parallel", "parallel")),
    )(proj, conv_w)


def _gdn_pre_bwd(proj, conv_w, dout, kind, first_block, B, name):
    T = proj.shape[0]
    S = T // B
    nh = GDN_HEADS

    def body(x_ref, w_ref, d_ref, dx_ref, dw_ref):
        b = pl.program_id(1)
        x, w = x_ref[...], _conv_taps(w_ref)
        y = _gdn_conv(x, w)
        _, vjp = jax.vjp(functools.partial(_gdn_post_conv, kind), y)
        (dy,) = vjp(d_ref[...])
        dx = w[GDN_CONV - 1] * dy

        @pl.when(b == 0)
        def _():
            dw_ref[...] = jnp.zeros_like(dw_ref)

        for k in range(GDN_CONV):
            j = GDN_CONV - 1 - k
            if j:
                dx = dx + w[k] * _shift_up(dy, j)
            dw_ref[k:k + 1, :] += jnp.sum(dy * _shift_down(x, j), axis=0, keepdims=True)
        dx_ref[...] = dx

    return pl.pallas_call(
        body, name=name,
        out_shape=[jax.ShapeDtypeStruct((T, nh * GDN_DIM), F32), jax.ShapeDtypeStruct((GDN_CONV, nh * GDN_DIM), F32)],
        grid=(nh, B),
        in_specs=[pl.BlockSpec((S, GDN_DIM), lambda c, b: (b, c + first_block)),
                  pl.BlockSpec((GDN_CONV, GDN_DIM), lambda c, b: (0, c)),
                  pl.BlockSpec((S, GDN_DIM), lambda c, b: (b, c))],
        out_specs=[pl.BlockSpec((S, GDN_DIM), lambda c, b: (b, c)),
                   pl.BlockSpec((GDN_CONV, GDN_DIM), lambda c, b: (0, c))],
        compiler_params=_cparams(("arbitrary", "arbitrary")),
    )(proj, conv_w, dout)


def _softplus(x):
    return jnp.maximum(x, 0.0) + jnp.log(1.0 + jnp.exp(-jnp.abs(x)))


def _f_gdn_gates(ba, a_log, dt_bias):
    n = ba.shape[0]
    lane = lax.broadcasted_iota(jnp.int32, ba.shape, 1)
    beta = jax.nn.sigmoid(ba)
    g = -jnp.exp(a_log) * _softplus(ba + dt_bias)
    ri = lax.broadcasted_iota(jnp.int32, (n, n), 0)
    ci = lax.broadcasted_iota(jnp.int32, (n, n), 1)
    tri = jnp.where((ri // GDN_CHUNK == ci // GDN_CHUNK) & (ci <= ri), 1.0, 0.0)
    gc = jnp.dot(tri, g, precision=HI, preferred_element_type=F32)
    return jnp.where(lane < GDN_HEADS, beta, gc)


def _bmm(a, b, ca, cb):
    return lax.dot_general(a, b, (((ca,), (cb,)), ((0,), (0,))), precision=HI3, preferred_element_type=F32)


GDN_INV_LEAF = 16


def _unit_lower_inverse(low, ri, ci):
    C = low.shape[1]
    b = GDN_INV_LEAF
    p = jnp.where(ri // b == ci // b, low, 0.0)
    x = jnp.where(ci == ri, 1.0, 0.0) - p
    for _ in range(int(math.log2(b)) - 1):
        p = _bmm(p, p, 2, 1)
        x = x + _bmm(x, p, 2, 1)
    while b < C:
        off = jnp.where(jnp.logical_and(ri // (2 * b) == ci // (2 * b), ri // b != ci // b), low, 0.0)
        x = x - _bmm(_bmm(x, off, 2, 1), x, 2, 1)
        b *= 2
    return x


def _gdn_chunk_fn(q, k, v, gc, beta, h):
    N, C, d = q.shape
    ri = lax.broadcasted_iota(jnp.int32, (N, C, C), 1)
    ci = lax.broadcasted_iota(jnp.int32, (N, C, C), 2)
    kb = k * beta
    vb = v * beta
    gi = jnp.broadcast_to(gc, (N, C, C))
    gj = jnp.swapaxes(gi, 1, 2)
    decay = jnp.exp(jnp.where(ci <= ri, gi - gj, -1e30))
    low = jnp.where(ci < ri, _bmm(kb, k, 2, 2) * decay, 0.0)
    ainv = _unit_lower_inverse(low, ri, ci)
    eg = jnp.exp(gc)
    u = _bmm(ainv, vb, 2, 1)
    w = _bmm(ainv, kb * eg, 2, 1)
    attn = _bmm(q, k, 2, 2) * decay
    v_new = u - _bmm(w, h, 2, 1)
    o = _bmm(q * eg, h, 2, 1) + _bmm(attn, v_new, 2, 1)
    rc = lax.broadcasted_iota(jnp.int32, (N, C, 1), 1)
    g_last = jnp.sum(jnp.where(rc == C - 1, gc, 0.0), axis=1, keepdims=True)
    h_new = h * jnp.exp(g_last) + _bmm(k * jnp.exp(g_last - gc), v_new, 1, 1)
    return o, h_new


def _gdn_heads(x):
    return jnp.stack([x[:, h * GDN_DIM:(h + 1) * GDN_DIM] for h in range(GDN_HEADS)], axis=0)


def _gdn_gate_cols(G, first_lane):
    lane = lax.broadcasted_iota(jnp.int32, G.shape, 1)
    return jnp.stack([jnp.sum(jnp.where(lane == first_lane + h, G, 0.0), axis=1, keepdims=True)
                      for h in range(GDN_HEADS)], axis=0)


def _gdn_chunk_specs(nc, rev):
    C, W = GDN_CHUNK, GDN_HEADS * GDN_DIM

    def at(n):
        return nc - 1 - n if rev else n

    row = lambda b, n: (b * nc + at(n), 0)
    return [pl.BlockSpec((C, W), row)] * 3 + [pl.BlockSpec((C, LANES), row)]


def _gdn_scan_fwd(q, k, v, G, B, name):
    T, W = q.shape
    C = GDN_CHUNK
    nc = T // B // C

    def body(q_ref, k_ref, v_ref, g_ref, o_ref, hs_ref, h_ref):
        @pl.when(pl.program_id(1) == 0)
        def _():
            h_ref[...] = jnp.zeros_like(h_ref)

        G_ = g_ref[...]
        h = h_ref[...]
        hs_ref[0, 0] = h
        o, hn = _gdn_chunk_fn(_gdn_heads(q_ref[...]), _gdn_heads(k_ref[...]), _gdn_heads(v_ref[...]),
                              _gdn_gate_cols(G_, GDN_HEADS), _gdn_gate_cols(G_, 0), h)
        h_ref[...] = hn
        for hd in range(GDN_HEADS):
            o_ref[:, hd * GDN_DIM:(hd + 1) * GDN_DIM] = o[hd]

    return pl.pallas_call(
        body, name=name,
        out_shape=[jax.ShapeDtypeStruct((T, W), F32), jax.ShapeDtypeStruct((B, nc, GDN_HEADS, GDN_DIM, GDN_DIM), F32)],
        grid=(B, nc), in_specs=_gdn_chunk_specs(nc, False),
        out_specs=[pl.BlockSpec((C, W), lambda b, n: (b * nc + n, 0)),
                   pl.BlockSpec((1, 1, GDN_HEADS, GDN_DIM, GDN_DIM), lambda b, n: (b, n, 0, 0, 0))],
        scratch_shapes=[pltpu.VMEM((GDN_HEADS, GDN_DIM, GDN_DIM), F32)],
        compiler_params=_cparams(("parallel", "arbitrary")),
    )(q, k, v, G)


def _gdn_scan_bwd(q, k, v, G, hs, do, B, name):
    T, W = q.shape
    C = GDN_CHUNK
    nc = T // B // C

    def body(q_ref, k_ref, v_ref, g_ref, hs_ref, do_ref, dq_ref, dk_ref, dv_ref, dg_ref, dh_ref):
        @pl.when(pl.program_id(1) == 0)
        def _():
            dh_ref[...] = jnp.zeros_like(dh_ref)

        G_ = g_ref[...]
        args = (_gdn_heads(q_ref[...]), _gdn_heads(k_ref[...]), _gdn_heads(v_ref[...]),
                _gdn_gate_cols(G_, GDN_HEADS), _gdn_gate_cols(G_, 0), hs_ref[0, 0])
        _, vjp = jax.vjp(_gdn_chunk_fn, *args)
        dq, dk, dv, dgc, dbeta, dh = vjp((_gdn_heads(do_ref[...]), dh_ref[...]))
        dh_ref[...] = dh
        lane = lax.broadcasted_iota(jnp.int32, G_.shape, 1)
        dG = jnp.zeros_like(G_)
        for hd in range(GDN_HEADS):
            sl = slice(hd * GDN_DIM, (hd + 1) * GDN_DIM)
            dq_ref[:, sl] = dq[hd]
            dk_ref[:, sl] = dk[hd]
            dv_ref[:, sl] = dv[hd]
            dG = dG + jnp.where(lane == hd, dbeta[hd], 0.0) + jnp.where(lane == GDN_HEADS + hd, dgc[hd], 0.0)
        dg_ref[...] = dG

    rrow = lambda b, n: (b * nc + nc - 1 - n, 0)
    return pl.pallas_call(
        body, name=name,
        out_shape=[jax.ShapeDtypeStruct((T, W), F32)] * 3 + [jax.ShapeDtypeStruct((T, LANES), F32)],
        grid=(B, nc),
        in_specs=_gdn_chunk_specs(nc, True) + [
            pl.BlockSpec((1, 1, GDN_HEADS, GDN_DIM, GDN_DIM), lambda b, n: (b, nc - 1 - n, 0, 0, 0)),
            pl.BlockSpec((C, W), rrow)],
        out_specs=[pl.BlockSpec((C, W), rrow)] * 3 + [pl.BlockSpec((C, LANES), rrow)],
        scratch_shapes=[pltpu.VMEM((GDN_HEADS, GDN_DIM, GDN_DIM), F32)],
        compiler_params=_cparams(("parallel", "arbitrary")),
    )(q, k, v, G, hs, do)


def _f_gdn_out(o, z, w):
    return _rms(o, w) * z * jax.nn.sigmoid(z)


def _gdn_fwd(qkv, z, ba, conv_w, gate_params, out_norm, B, tag):
    a_log, dt_bias = gate_params
    W = GDN_HEADS * GDN_DIM
    qn, kn, vn = [_gdn_pre_fwd(qkv[0], conv_w[:, i * W:(i + 1) * W], kd, qkv[1] + i * GDN_HEADS, B, f"{tag}_pre_{kd}")
                  for i, kd in enumerate("qkv")]
    (G,) = _ew_fwd(_f_gdn_gates, [(ba[0], LANES, ba[1])], [], [a_log, dt_bias], [(LANES, F32, False)],
                   name=f"{tag}_gates")
    o, hs = _gdn_scan_fwd(qn, kn, vn, G, B, f"{tag}_scan")
    (y,) = _ew_fwd(_f_gdn_out, [(o, GDN_DIM, 0), (z[0], GDN_DIM, z[1])], [], [out_norm], [(W, BF16, True)],
                   ncb=GDN_HEADS, name=f"{tag}_out")
    return y, (qkv, z, ba, qn, kn, vn, G, hs, o)


def _gdn_bwd(dy, saved, conv_w, gate_params, out_norm, B, tag):
    qkv, z, ba, qn, kn, vn, G, hs, o = saved
    a_log, dt_bias = gate_params
    W = GDN_HEADS * GDN_DIM
    do, dz, d_out_norm = _ew_bwd(_f_gdn_out, [(o, GDN_DIM, 0), (z[0], GDN_DIM, z[1])], [], [out_norm],
                                 [(dy, GDN_DIM, 0)], [True] * 3, ncb=GDN_HEADS, name=f"{tag}_out_bwd")
    dq, dk, dv, dG = _gdn_scan_bwd(qn, kn, vn, G, hs, do, B, f"{tag}_scan_bwd")
    dba, d_a_log, d_dt_bias = _ew_bwd(_f_gdn_gates, [(ba[0], LANES, ba[1])], [], [a_log, dt_bias], [dG], [True] * 3,
                                      name=f"{tag}_gates_bwd")
    dxs, dws = [], []
    for i, (kd, d) in enumerate(zip("qkv", (dq, dk, dv))):
        dx, dw = _gdn_pre_bwd(qkv[0], conv_w[:, i * W:(i + 1) * W], d, kd, qkv[1] + i * GDN_HEADS, B,
                              f"{tag}_pre_{kd}_bwd")
        dxs.append(dx)
        dws.append(dw)
    return (jnp.concatenate(dxs, axis=1), dz, dba,
            dict(conv=jnp.concatenate(dws, axis=1), a_log=d_a_log, dt_bias=d_dt_bias, out_norm=d_out_norm))


S5_GPB = LANES // S5_GROUP
S5_SLANES = S5_GPB * S5_STATE


def _cmul(ar, ai, br, bi):
    return ar * br - ai * bi, ar * bi + ai * br


def _s5_prep_fn(a_re, a_im, ls, b_re, b_im):
    lr = jnp.minimum(a_re, S5_MAX_RE)
    li = a_im
    step = jnp.exp(ls)
    mag = jnp.exp(lr * step)
    lbr, lbi = mag * jnp.cos(li * step), mag * jnp.sin(li * step)
    den = lr * lr + li * li
    cr = ((lbr - 1.0) * lr + lbi * li) / den
    ci = (lbi * lr - (lbr - 1.0) * li) / den
    bbr = cr[:, None, :] * b_re - ci[:, None, :] * b_im
    bbi = cr[:, None, :] * b_im + ci[:, None, :] * b_re
    return lbr, lbi, bbr, bbi


def _s5_prep_fwd(args, name):
    G, I, P = args[3].shape
    shp = [jax.ShapeDtypeStruct((G, P), F32)] * 2 + [jax.ShapeDtypeStruct((G, I, P), F32)] * 2

    def body(*refs):
        for o_ref, v in zip(refs[5:], _s5_prep_fn(*[r[...] for r in refs[:5]])):
            o_ref[...] = v

    return pl.pallas_call(body, name=name, out_shape=shp, compiler_params=_cparams())(*args)


def _s5_prep_bwd(args, cts, name):
    shp = [jax.ShapeDtypeStruct(a.shape, F32) for a in args]

    def body(*refs):
        _, vjp = jax.vjp(_s5_prep_fn, *[r[...] for r in refs[:5]])
        for o_ref, v in zip(refs[9:], vjp(tuple(r[...] for r in refs[5:9]))):
            o_ref[...] = v

    return pl.pallas_call(body, name=name, out_shape=shp, compiler_params=_cparams())(*args, *cts)


def _s5_blockdiag_in(bb):
    G, I, P = bb.shape
    nb = G // S5_GPB
    return jnp.einsum("jgip,gh->jgihp", bb.reshape(nb, S5_GPB, I, P), jnp.eye(S5_GPB, dtype=bb.dtype)).reshape(
        nb, S5_GPB * I, S5_GPB * P)


def _s5_blockdiag_in_t(d):
    nb = d.shape[0]
    d = d.reshape(nb, S5_GPB, S5_GROUP, S5_GPB, S5_STATE)
    return jnp.einsum("jgihp,gh->jgip", d, jnp.eye(S5_GPB, dtype=d.dtype)).reshape(nb * S5_GPB, S5_GROUP, S5_STATE)


def _s5_blockdiag_out(c):
    G, I, P = c.shape
    nb = G // S5_GPB
    return jnp.einsum("jgip,gh->jgphi", c.reshape(nb, S5_GPB, I, P), jnp.eye(S5_GPB, dtype=c.dtype)).reshape(
        nb, S5_GPB * P, S5_GPB * I)


def _s5_blockdiag_out_t(d):
    nb = d.shape[0]
    d = d.reshape(nb, S5_GPB, S5_STATE, S5_GPB, S5_GROUP)
    return jnp.einsum("jgphi,gh->jgip", d, jnp.eye(S5_GPB, dtype=d.dtype)).reshape(nb * S5_GPB, S5_GROUP, S5_STATE)


def _s5_powers(lr, li, n):
    out = []
    for _ in range(int(math.log2(n))):
        out.append((lr, li))
        lr, li = _cmul(lr, li, lr, li)
    return out


def _s5_local_scan(sr, si, powers, up):
    shift = _shift_up if up else _shift_down
    for k, (pr, pi) in enumerate(powers):
        d = 1 << k
        tr_, ti_ = _cmul(pr, pi, shift(sr, d), shift(si, d))
        sr, si = sr + tr_, si + ti_
    return sr, si


def _dot_hi(a, b, ca=1, cb=0):
    return lax.dot_general(a, b, (((ca,), (cb,)), ((), ())), precision=HI3, preferred_element_type=F32)


def _s5_specs(nt, rev, ublk):
    tc = S5_TCHUNK

    def at(t):
        return nt - 1 - t if rev else t

    return [
        pl.BlockSpec((tc, LANES), lambda j, b, t: (b * nt + at(t), j + ublk)),
        pl.BlockSpec((1, S5_SLANES), lambda j, b, t: (0, j)),
        pl.BlockSpec((1, S5_SLANES), lambda j, b, t: (0, j)),
        pl.BlockSpec((1, LANES, S5_SLANES), lambda j, b, t: (j, 0, 0)),
        pl.BlockSpec((1, LANES, S5_SLANES), lambda j, b, t: (j, 0, 0)),
        pl.BlockSpec((1, S5_SLANES, LANES), lambda j, b, t: (j, 0, 0)),
        pl.BlockSpec((1, S5_SLANES, LANES), lambda j, b, t: (j, 0, 0)),
        pl.BlockSpec((1, LANES), lambda j, b, t: (0, j)),
    ]


def _s5_chunk_states(u, lr, li, b_re, b_im, cr, ci, powers):
    bur, bui = _dot_hi(u, b_re), _dot_hi(u, b_im)
    row = lax.broadcasted_iota(jnp.int32, bur.shape, 0)
    inr, ini = _cmul(lr, li, cr, ci)
    bur = bur + jnp.where(row == 0, inr, 0.0)
    bui = bui + jnp.where(row == 0, ini, 0.0)
    return _s5_local_scan(bur, bui, powers, False)


def _s5_scan_fwd(u, lam_re, lam_im, Bre, Bim, Cre, Cim, dskip, B, name):
    u, ublk = u
    T = u.shape[0]
    nb = Bre.shape[0]
    Wd = nb * LANES
    tc = S5_TCHUNK
    nt = T // B // tc
    L = nb * S5_SLANES

    def body(u_ref, lr_ref, li_ref, br_ref, bi_ref, cr_ref, ci_ref, d_ref, y_ref, csr_ref, csi_ref, car_ref, cai_ref):
        @pl.when(pl.program_id(2) == 0)
        def _():
            car_ref[...] = jnp.zeros_like(car_ref)
            cai_ref[...] = jnp.zeros_like(cai_ref)

        csr_ref[0, 0] = car_ref[...]
        csi_ref[0, 0] = cai_ref[...]
        u_ = u_ref[...]
        lr, li = lr_ref[...], li_ref[...]
        sr, si = _s5_chunk_states(u_, lr, li, br_ref[0], bi_ref[0], car_ref[0:1, :], cai_ref[0:1, :],
                                  _s5_powers(lr, li, tc))
        y_ref[...] = _dot_hi(sr, cr_ref[0]) - _dot_hi(si, ci_ref[0]) + d_ref[...] * u_
        row = lax.broadcasted_iota(jnp.int32, sr.shape, 0)
        car_ref[0:1, :] = jnp.sum(jnp.where(row == tc - 1, sr, 0.0), axis=0, keepdims=True)
        cai_ref[0:1, :] = jnp.sum(jnp.where(row == tc - 1, si, 0.0), axis=0, keepdims=True)

    cs_shape = jax.ShapeDtypeStruct((B, nt, SUBLANES, L), F32)
    cs_spec = pl.BlockSpec((1, 1, SUBLANES, S5_SLANES), lambda j, b, t: (b, t, 0, j))
    return pl.pallas_call(
        body, name=name, out_shape=[jax.ShapeDtypeStruct((T, Wd), F32), cs_shape, cs_shape],
        grid=(nb, B, nt), in_specs=_s5_specs(nt, False, ublk),
        out_specs=[pl.BlockSpec((tc, LANES), lambda j, b, t: (b * nt + t, j)), cs_spec, cs_spec],
        scratch_shapes=[pltpu.VMEM((SUBLANES, S5_SLANES), F32)] * 2,
        compiler_params=_cparams(("parallel", "parallel", "arbitrary")),
    )(u, lam_re, lam_im, Bre, Bim, Cre, Cim, dskip)


def _s5_scan_bwd(u, lam_re, lam_im, Bre, Bim, Cre, Cim, dskip, csr, csi, dy, B, name):
    u, ublk = u
    T = u.shape[0]
    nb = Bre.shape[0]
    Wd = nb * LANES
    tc = S5_TCHUNK
    nt = T // B // tc
    L = nb * S5_SLANES

    def body(u_ref, lr_ref, li_ref, br_ref, bi_ref, cr_ref, ci_ref, d_ref, csr_ref, csi_ref, dy_ref,
             du_ref, dlr_ref, dli_ref, dbr_ref, dbi_ref, dcr_ref, dci_ref, dd_ref, gr_ref, gi_ref):
        first = jnp.logical_and(pl.program_id(1) == 0, pl.program_id(2) == 0)

        @pl.when(pl.program_id(2) == 0)
        def _():
            gr_ref[...] = jnp.zeros_like(gr_ref)
            gi_ref[...] = jnp.zeros_like(gi_ref)

        @pl.when(first)
        def _():
            for r in (dlr_ref, dli_ref, dbr_ref, dbi_ref, dcr_ref, dci_ref, dd_ref):
                r[...] = jnp.zeros_like(r)

        u_, dy_ = u_ref[...], dy_ref[...]
        lr, li = lr_ref[...], li_ref[...]
        powers = _s5_powers(lr, li, tc)
        c_in_r, c_in_i = csr_ref[0, 0, 0:1, :], csi_ref[0, 0, 0:1, :]
        sr, si = _s5_chunk_states(u_, lr, li, br_ref[0], bi_ref[0], c_in_r, c_in_i, powers)
        dcr_ref[0] += _dot_hi(sr, dy_, 0, 0)
        dci_ref[0] -= _dot_hi(si, dy_, 0, 0)
        dd_ref[...] += jnp.sum(dy_ * u_, axis=0, keepdims=True)
        row = lax.broadcasted_iota(jnp.int32, sr.shape, 0)
        gr = _dot_hi(dy_, cr_ref[0], 1, 1)
        gi = -_dot_hi(dy_, ci_ref[0], 1, 1)
        inr, ini = _cmul(lr, -li, gr_ref[0:1, :], gi_ref[0:1, :])
        gr = gr + jnp.where(row == tc - 1, inr, 0.0)
        gi = gi + jnp.where(row == tc - 1, ini, 0.0)
        gr, gi = _s5_local_scan(gr, gi, [(pr, -pi) for pr, pi in powers], True)
        gr_ref[0:1, :] = jnp.sum(jnp.where(row == 0, gr, 0.0), axis=0, keepdims=True)
        gi_ref[0:1, :] = jnp.sum(jnp.where(row == 0, gi, 0.0), axis=0, keepdims=True)
        pr_ = _shift_down(sr, 1) + jnp.where(row == 0, c_in_r, 0.0)
        pi_ = _shift_down(si, 1) + jnp.where(row == 0, c_in_i, 0.0)
        dlr_ref[...] += jnp.sum(gr * pr_ + gi * pi_, axis=0, keepdims=True)
        dli_ref[...] += jnp.sum(gi * pr_ - gr * pi_, axis=0, keepdims=True)
        dbr_ref[0] += _dot_hi(u_, gr, 0, 0)
        dbi_ref[0] += _dot_hi(u_, gi, 0, 0)
        du_ref[...] = dy_ * d_ref[...] + _dot_hi(gr, br_ref[0], 1, 1) + _dot_hi(gi, bi_ref[0], 1, 1)

    cs_spec = pl.BlockSpec((1, 1, SUBLANES, S5_SLANES), lambda j, b, t: (b, nt - 1 - t, 0, j))
    rrow = pl.BlockSpec((tc, LANES), lambda j, b, t: (b * nt + nt - 1 - t, j))
    lam_spec = pl.BlockSpec((1, S5_SLANES), lambda j, b, t: (0, j))
    b_spec = pl.BlockSpec((1, LANES, S5_SLANES), lambda j, b, t: (j, 0, 0))
    c_spec = pl.BlockSpec((1, S5_SLANES, LANES), lambda j, b, t: (j, 0, 0))
    return pl.pallas_call(
        body, name=name,
        out_shape=[jax.ShapeDtypeStruct((T, Wd), F32)] + [jax.ShapeDtypeStruct((1, L), F32)] * 2
        + [jax.ShapeDtypeStruct((nb, LANES, S5_SLANES), F32)] * 2
        + [jax.ShapeDtypeStruct((nb, S5_SLANES, LANES), F32)] * 2 + [jax.ShapeDtypeStruct((1, Wd), F32)],
        grid=(nb, B, nt), in_specs=_s5_specs(nt, True, ublk) + [cs_spec, cs_spec, rrow],
        out_specs=[rrow, lam_spec, lam_spec, b_spec, b_spec, c_spec, c_spec,
                   pl.BlockSpec((1, LANES), lambda j, b, t: (0, j))],
        scratch_shapes=[pltpu.VMEM((SUBLANES, S5_SLANES), F32)] * 2,
        compiler_params=_cparams(("arbitrary", "arbitrary", "arbitrary")),
    )(u, lam_re, lam_im, Bre, Bim, Cre, Cim, dskip, csr, csi, dy)


def _f_gelu(y):
    return 0.5 * y * (1.0 + jnp.tanh(math.sqrt(2.0 / math.pi) * (y + 0.044715 * (y * y * y))))


def _f_glu(pv, pg, bv, bg):
    return (pv + bv) * jax.nn.sigmoid(pg + bg)


def _s5_params(p):
    prep_in = (p["a_re"], p["a_im"], p["log_step"][:, None], jnp.swapaxes(p["b_re"], 1, 2), jnp.swapaxes(p["b_im"], 1, 2))
    return prep_in


def _s5_fwd(u, p, glu_w, B, tag):
    Wd = p["d"].shape[0]
    prep_in = _s5_params(p)
    lbr, lbi, bbr, bbi = _s5_prep_fwd(prep_in, f"{tag}_prep")
    ops = (lbr.reshape(1, -1), lbi.reshape(1, -1), _s5_blockdiag_in(bbr), _s5_blockdiag_in(bbi),
           _s5_blockdiag_out(p["c_re"]), _s5_blockdiag_out(p["c_im"]), p["d"][None])
    y, csr, csi = _s5_scan_fwd(u, *ops, B, f"{tag}_scan")
    (yg,) = _ew_fwd(_f_gelu, [y], [], [], [(Wd, BF16, False)], name=f"{tag}_gelu")
    pj = _mm(yg, glu_w, name=f"{tag}_glu")
    bv, bg = p["glu_b"][None, :Wd], p["glu_b"][None, Wd:]
    (out,) = _ew_fwd(_f_glu, [(pj, Wd, 0), (pj, Wd, 1)], [], [bv, bg], [(Wd, BF16, False)], name=f"{tag}_gate")
    return out, (u, prep_in, ops, csr, csi, y, yg, pj, bv, bg)


def _s5_bwd(dout, saved, p, glu_w, B, tag):
    u, prep_in, ops, csr, csi, y, yg, pj, bv, bg = saved
    Wd = p["d"].shape[0]
    dpv, dpg, dbv, dbg = _ew_bwd(_f_glu, [(pj, Wd, 0), (pj, Wd, 1)], [], [bv, bg], [dout], [True] * 4,
                                 row_dtypes=[BF16, BF16], name=f"{tag}_gate_bwd")
    dpj = jnp.concatenate([dpv, dpg], axis=1)
    d_glu_w = _mm(yg, dpj, ta=True, out_dtype=BF16, name=f"{tag}_glu_dw")
    dyg = _mm(dpj, glu_w, tb=True, name=f"{tag}_glu_dx")
    (dy,) = _ew_bwd(_f_gelu, [y], [], [], [dyg], [True], name=f"{tag}_gelu_bwd")
    du, dlr, dli, dBr, dBi, dCr, dCi, dd = _s5_scan_bwd(u, *ops, csr, csi, dy, B, f"{tag}_scan_bwd")
    G = p["a_re"].shape[0]
    cts = (dlr.reshape(G, S5_STATE), dli.reshape(G, S5_STATE), _s5_blockdiag_in_t(dBr), _s5_blockdiag_in_t(dBi))
    da_re, da_im, dls, db_re, db_im = _s5_prep_bwd(prep_in, cts, f"{tag}_prep_bwd")
    small = dict(a_re=da_re, a_im=da_im, log_step=dls[:, 0], b_re=jnp.swapaxes(db_re, 1, 2),
                 b_im=jnp.swapaxes(db_im, 1, 2), c_re=_s5_blockdiag_out_t(dCr), c_im=_s5_blockdiag_out_t(dCi),
                 d=dd[0], glu_b=jnp.concatenate([dbv[0], dbg[0]]))
    return du, d_glu_w, small


DIL_GW = DIL_HPG * DIL_DIM


def _f_qknorm(scale, x, w):
    n = x.shape[1]
    ri = lax.broadcasted_iota(jnp.int32, (n, n), 0)
    ci = lax.broadcasted_iota(jnp.int32, (n, n), 1)
    seg = jnp.where(ri // DIL_DIM == ci // DIL_DIM, 1.0 / DIL_DIM, 0.0)
    ms = jnp.dot(x * x, seg, precision=HI, preferred_element_type=F32)
    return x * lax.rsqrt(ms + EPS) * (w * scale)


def _dil_to_blocks(x, B):
    T = x.shape[0]
    S = T // B
    parts = []
    for gi, (_, dil) in enumerate(DIL_PAIRS):
        xg = x[:, gi * DIL_GW:(gi + 1) * DIL_GW].reshape(B, S // dil, dil, DIL_HPG, DIL_DIM)
        parts.append(xg.transpose(0, 2, 3, 1, 4).reshape(-1, DIL_DIM))
    return jnp.concatenate(parts, axis=0)


def _dil_from_blocks(y, B):
    n = y.shape[0] // len(DIL_PAIRS)
    T = n // DIL_HPG
    S = T // B
    parts = []
    for gi, (_, dil) in enumerate(DIL_PAIRS):
        yg = y[gi * n:(gi + 1) * n].reshape(B, dil, DIL_HPG, S // dil, DIL_DIM)
        parts.append(yg.transpose(0, 3, 1, 2, 4).reshape(T, DIL_GW))
    return jnp.concatenate(parts, axis=1)


DIL_BPS = 8


def _dil_block_fn(slope, has_prev, q, kp, kc, vp, vc):
    G, n, _ = q.shape
    qi = lax.broadcasted_iota(jnp.int32, (G, n, n), 1)
    kj = lax.broadcasted_iota(jnp.int32, (G, n, n), 2)
    dist = (qi - kj).astype(F32)
    sc = _bmm(q, kc, 2, 2) - slope * dist
    sp = _bmm(q, kp, 2, 2) - slope * (dist + n)
    sc = jnp.where(qi >= kj, sc, -1e30)
    sp = jnp.where(jnp.logical_and(kj >= qi, has_prev > 0.5), sp, -1e30)
    m = lax.stop_gradient(jnp.maximum(jnp.max(sc, axis=2, keepdims=True), jnp.max(sp, axis=2, keepdims=True)))
    pc = jnp.exp(sc - m)
    pp = jnp.exp(sp - m)
    l = jnp.sum(pc, axis=2, keepdims=True) + jnp.sum(pp, axis=2, keepdims=True)
    o = (_bmm(pp, vp, 2, 1) + _bmm(pc, vc, 2, 1)) / l
    return o, jnp.broadcast_to(m + jnp.log(l), o.shape)


def _dil_tables(B, S):
    nh = len(DIL_PAIRS) * DIL_HPG
    slopes, has_prev = [], []
    for gi, (_, dil) in enumerate(DIL_PAIRS):
        nbk = S // dil // DIL_BLK
        for seq in range(B * dil * DIL_HPG):
            head = gi * DIL_HPG + seq % DIL_HPG
            for n in range(nbk):
                slopes.append(dil * 2.0 ** (-ALIBI_MAX * (head + 1) / nh))
                has_prev.append(1.0 if n > 0 else 0.0)
    shape = (len(slopes), 1, 1)
    return jnp.asarray(np.array(slopes, np.float32).reshape(shape)), jnp.asarray(np.array(has_prev, np.float32).reshape(shape))


def _dil_blocks3(t):
    return t.reshape(-1, DIL_BLK, DIL_DIM)


def _dil_prev(t3):
    return jnp.concatenate([jnp.zeros_like(t3[:1]), t3[:-1]], axis=0)


def _dil_attn_fwd(qb, kb, vb, B, S, name):
    q3, k3, v3 = _dil_blocks3(qb), _dil_blocks3(kb), _dil_blocks3(vb)
    nbt = q3.shape[0]
    slope, has_prev = _dil_tables(B, S)

    def body(s_ref, h_ref, q_ref, kp_ref, kc_ref, vp_ref, vc_ref, o_ref, l_ref):
        o, l = _dil_block_fn(s_ref[...], h_ref[...], q_ref[...], kp_ref[...], kc_ref[...], vp_ref[...], vc_ref[...])
        o_ref[...] = o
        l_ref[...] = l

    blk = pl.BlockSpec((DIL_BPS, DIL_BLK, DIL_DIM), lambda m: (m, 0, 0))
    tab = pl.BlockSpec((DIL_BPS, 1, 1), lambda m: (m, 0, 0))
    o, l = pl.pallas_call(
        body, name=name, out_shape=[jax.ShapeDtypeStruct(q3.shape, F32)] * 2, grid=(nbt // DIL_BPS,),
        in_specs=[tab, tab] + [blk] * 5, out_specs=[blk, blk], compiler_params=_cparams(("parallel",)),
    )(slope, has_prev, q3, _dil_prev(k3), k3, _dil_prev(v3), v3)
    return o.reshape(qb.shape), l.reshape(qb.shape)


def _dil_attn_bwd(qb, kb, vb, do, dl, B, S, name):
    q3, k3, v3 = _dil_blocks3(qb), _dil_blocks3(kb), _dil_blocks3(vb)
    nbt = q3.shape[0]
    slope, has_prev = _dil_tables(B, S)

    def body(s_ref, h_ref, q_ref, kp_ref, kc_ref, vp_ref, vc_ref, do_ref, dl_ref, *outs):
        _, vjp = jax.vjp(functools.partial(_dil_block_fn, s_ref[...], h_ref[...]),
                         q_ref[...], kp_ref[...], kc_ref[...], vp_ref[...], vc_ref[...])
        for o_ref, g in zip(outs, vjp((do_ref[...], dl_ref[...]))):
            o_ref[...] = g

    blk = pl.BlockSpec((DIL_BPS, DIL_BLK, DIL_DIM), lambda m: (m, 0, 0))
    tab = pl.BlockSpec((DIL_BPS, 1, 1), lambda m: (m, 0, 0))
    outs = pl.pallas_call(
        body, name=name, out_shape=[jax.ShapeDtypeStruct(q3.shape, F32)] * 5, grid=(nbt // DIL_BPS,),
        in_specs=[tab, tab] + [blk] * 7, out_specs=[blk] * 5, compiler_params=_cparams(("parallel",)),
    )(slope, has_prev, q3, _dil_prev(k3), k3, _dil_prev(v3), v3, _dil_blocks3(do), _dil_blocks3(dl))
    return [t.reshape(qb.shape) for t in outs]


def _f_dil_merge(o0, o1, o2, l0, l1, l2):
    m = lax.stop_gradient(jnp.maximum(jnp.maximum(l0, l1), l2))
    e0, e1, e2 = jnp.exp(l0 - m), jnp.exp(l1 - m), jnp.exp(l2 - m)
    return (e0 * o0 + e1 * o1 + e2 * o2) / (e0 + e1 + e2)


def _dil_fwd(qkv, q_norm, k_norm, B, tag):
    qkv, fb = qkv
    T = qkv.shape[0]
    S = T // B
    Wd = len(DIL_PAIRS) * DIL_GW
    nblk = Wd // LANES
    (qn,) = _ew_fwd(functools.partial(_f_qknorm, DIL_DIM ** -0.5), [(qkv, LANES, fb)], [], [q_norm],
                    [(Wd, F32, True)], ncb=nblk, name=f"{tag}_qnorm")
    (kn,) = _ew_fwd(functools.partial(_f_qknorm, 1.0), [(qkv, LANES, fb + nblk)], [], [k_norm],
                    [(Wd, F32, True)], ncb=nblk, name=f"{tag}_knorm")
    v0 = (fb + 2 * nblk) * LANES
    qb, kb, vb = _dil_to_blocks(qn, B), _dil_to_blocks(kn, B), _dil_to_blocks(qkv[:, v0:v0 + Wd], B)
    ob, lb = _dil_attn_fwd(qb, kb, vb, B, S, f"{tag}_attn")
    o, l = _dil_from_blocks(ob, B), _dil_from_blocks(lb, B)
    gw = DIL_GW
    rows = [(o, gw, 0), (o, gw, 1), (o, gw, 2), (l, gw, 0), (l, gw, 1), (l, gw, 2)]
    (y,) = _ew_fwd(_f_dil_merge, rows, [], [], [(gw, BF16, False)], name=f"{tag}_merge")
    return y, (qkv, fb, qb, kb, vb, o, l)


def _dil_bwd(dy, saved, q_norm, k_norm, B, tag):
    qkv, fb, qb, kb, vb, o, l = saved
    T = qkv.shape[0]
    S = T // B
    Wd = len(DIL_PAIRS) * DIL_GW
    nblk = Wd // LANES
    gw = DIL_GW
    rows = [(o, gw, 0), (o, gw, 1), (o, gw, 2), (l, gw, 0), (l, gw, 1), (l, gw, 2)]
    g = _ew_bwd(_f_dil_merge, rows, [], [], [dy], [True] * 6, name=f"{tag}_merge_bwd")
    do = _dil_to_blocks(jnp.concatenate(g[:3], axis=1), B)
    dl = _dil_to_blocks(jnp.concatenate(g[3:], axis=1), B)
    dq, dkp, dkc, dvp, dvc = _dil_attn_bwd(qb, kb, vb, do, dl, B, S, f"{tag}_attn_bwd")
    nxt = lambda t: jnp.concatenate([t[DIL_BLK:], jnp.zeros((DIL_BLK, DIL_DIM), t.dtype)], axis=0)
    dqn = _dil_from_blocks(dq, B)
    dkn = _dil_from_blocks(dkc + nxt(dkp), B)
    dv = _dil_from_blocks(dvc + nxt(dvp), B)
    dq_raw, dqw = _ew_bwd(functools.partial(_f_qknorm, DIL_DIM ** -0.5), [(qkv, LANES, fb)], [], [q_norm],
                          [(dqn, LANES, 0)], [True, True], ncb=nblk, name=f"{tag}_qnorm_bwd")
    dk_raw, dkw = _ew_bwd(functools.partial(_f_qknorm, 1.0), [(qkv, LANES, fb + nblk)], [], [k_norm],
                          [(dkn, LANES, 0)], [True, True], ncb=nblk, name=f"{tag}_knorm_bwd")
    return jnp.concatenate([dq_raw, dk_raw, dv], axis=1), dqw, dkw


MESH_ID = pl.DeviceIdType.MESH
ANY = pl.BlockSpec(memory_space=pl.ANY)


def _my_place():
    return lax.axis_index("x"), lax.axis_index("y"), lax.axis_index("c")


def _flat_index(px, py, pc):
    return 4 * px + 2 * py + pc


def _all_gather(x, name):
    R_, C = x.shape

    def body(x_ref, out_ref, send_sems, recv_sems, local_sem):
        mx, my, mc = _my_place()
        me, sibling = (mx, my, mc), (mx, my, 1 - mc)
        chips = [(1 - mx, my), (mx, 1 - my), (1 - mx, 1 - my)]

        def rows(p):
            return out_ref.at[_flat_index(*p)]

        def copy(k, block, to, src=None):
            return pltpu.make_async_remote_copy(
                src_ref=rows(block) if src is None else src, dst_ref=rows(block),
                send_sem=send_sems.at[k], recv_sem=recv_sems.at[k], device_id=to, device_id_type=MESH_ID)

        mine = pltpu.make_async_copy(x_ref, rows(me), local_sem)
        mine.start()
        first = [copy(0, me, sibling, src=x_ref)]
        first += [copy(1 + j, me, (*chip, mc), src=x_ref) for j, chip in enumerate(chips)]
        for cp in first:
            cp.start()
        passed = [copy(4 + j, (*chip, mc), sibling) for j, chip in enumerate(chips)]
        for j, chip in enumerate(chips):
            copy(1 + j, (*chip, mc), me).wait_recv()
            passed[j].start()
        copy(0, sibling, me).wait_recv()
        for j, chip in enumerate(chips):
            copy(4 + j, (*chip, 1 - mc), me).wait_recv()
        for cp in first + passed:
            cp.wait_send()
        mine.wait()

    return pl.pallas_call(
        body, name=name, out_shape=jax.ShapeDtypeStruct((N_DEV, R_, C), x.dtype),
        in_specs=[ANY], out_specs=ANY,
        scratch_shapes=[pltpu.SemaphoreType.DMA((7,)), pltpu.SemaphoreType.DMA((7,)), pltpu.SemaphoreType.DMA],
        compiler_params=pltpu.CompilerParams(has_side_effects=True),
    )(x)


def _exchange_sibling(x, name):
    nchip, _, R_, C = x.shape

    def body(x_ref, out_ref, send_sems, recv_sems):
        mx, my, mc = _my_place()
        sibling = (mx, my, 1 - mc)
        copies = [pltpu.make_async_remote_copy(
            src_ref=x_ref.at[k, 1 - mc], dst_ref=out_ref.at[k], send_sem=send_sems.at[k], recv_sem=recv_sems.at[k],
            device_id=sibling, device_id_type=MESH_ID) for k in range(nchip)]
        for cp in copies:
            cp.start()
        for cp in copies:
            cp.wait_recv()
        for cp in copies:
            cp.wait_send()

    return pl.pallas_call(
        body, name=name, out_shape=jax.ShapeDtypeStruct((nchip, R_, C), x.dtype), in_specs=[ANY], out_specs=ANY,
        scratch_shapes=[pltpu.SemaphoreType.DMA((nchip,)), pltpu.SemaphoreType.DMA((nchip,))],
        compiler_params=pltpu.CompilerParams(has_side_effects=True),
    )(x)


def _exchange_chips(x, name):
    nchip, R_, C = x.shape

    def body(x_ref, out_ref, send_sems, recv_sems, local_sem):
        mx, my, mc = _my_place()
        mk = 2 * mx + my
        chips = [(1 - mx, my), (mx, 1 - my), (1 - mx, 1 - my)]
        mine = pltpu.make_async_copy(x_ref.at[mk], out_ref.at[mk], local_sem)
        mine.start()
        copies = [pltpu.make_async_remote_copy(
            src_ref=x_ref.at[2 * px + py], dst_ref=out_ref.at[mk], send_sem=send_sems.at[j], recv_sem=recv_sems.at[j],
            device_id=(px, py, mc), device_id_type=MESH_ID) for j, (px, py) in enumerate(chips)]
        for cp in copies:
            cp.start()
        for j, (px, py) in enumerate(chips):
            pltpu.make_async_remote_copy(
                src_ref=x_ref.at[mk], dst_ref=out_ref.at[2 * px + py], send_sem=send_sems.at[j],
                recv_sem=recv_sems.at[j], device_id=(px, py, mc), device_id_type=MESH_ID).wait_recv()
        for cp in copies:
            cp.wait_send()
        mine.wait()

    return pl.pallas_call(
        body, name=name, out_shape=jax.ShapeDtypeStruct(x.shape, x.dtype), in_specs=[ANY], out_specs=ANY,
        scratch_shapes=[pltpu.SemaphoreType.DMA((3,)), pltpu.SemaphoreType.DMA((3,)), pltpu.SemaphoreType.DMA],
        compiler_params=pltpu.CompilerParams(has_side_effects=True),
    )(x)


def _add_pieces(a, b, name, tr=64):
    n, R_, C = a.shape
    tr = _pick(R_, (tr, 32, 16, 8))

    def body(a_ref, b_ref, o_ref):
        o_ref[...] = (a_ref[...].astype(F32) + b_ref[...].astype(F32)).astype(o_ref.dtype)

    spec = pl.BlockSpec((1, tr, C), lambda k, i: (k, i, 0))
    return pl.pallas_call(
        body, name=name, out_shape=jax.ShapeDtypeStruct(a.shape, a.dtype), grid=(n, R_ // tr),
        in_specs=[spec, spec], out_specs=spec, compiler_params=_cparams(("parallel", "parallel")),
    )(a, b)


def _reduce_to_owner(buf, mc, tag):
    _, R_, C = buf.shape
    buf4 = buf.reshape(N_DEV // 2, 2, R_, C)
    from_sibling = _exchange_sibling(buf4, f"scatter_sib_{tag}")
    mine = lax.dynamic_index_in_dim(buf4, mc, axis=1, keepdims=False)
    chip_sums = _add_pieces(mine, from_sibling, f"add_sib_{tag}")
    return _sum0(_exchange_chips(chip_sums, f"scatter_chips_{tag}"), f"sum_grads_{tag}", tr=64)


def _sum0(x, name, tr=256):
    n, R_, C = x.shape
    tr = _pick(R_, (tr, 128, 64, 32, 16, 8))

    def body(x_ref, o_ref):
        acc = x_ref[0].astype(F32)
        for k in range(1, n):
            acc = acc + x_ref[k].astype(F32)
        o_ref[...] = acc

    return pl.pallas_call(
        body, name=name, out_shape=jax.ShapeDtypeStruct((R_, C), F32), grid=(R_ // tr,),
        in_specs=[pl.BlockSpec((n, tr, C), lambda i: (0, i, 0))], out_specs=pl.BlockSpec((tr, C), lambda i: (i, 0)),
        compiler_params=_cparams(("parallel",)),
    )(x)


PACK_ROWS = 256


def _pack(arrs, dtype, width):
    flat = jnp.concatenate([a.astype(dtype).reshape(-1) for a in arrs])
    quantum = width * PACK_ROWS
    pad = (-flat.shape[0]) % quantum
    if pad:
        flat = jnp.concatenate([flat, jnp.zeros((pad,), dtype)])
    return flat.reshape(-1, width)


def _unpack(flat, shapes):
    out, off = [], 0
    for s in shapes:
        n = int(np.prod(s))
        out.append(flat[..., off:off + n].reshape(flat.shape[:-1] + tuple(s)))
        off += n
    return out


def _ada_fwd(c_all, ada_w, bias, name):
    M, D = c_all.shape
    n = ada_w.shape[1]
    tn = _pick(n, (768, 512, 256, 128))

    def body(c_ref, w_ref, b_ref, o_ref):
        c_ = c_ref[...]
        a = (c_ * jax.nn.sigmoid(c_)).astype(BF16)
        o_ref[...] = jnp.dot(a, w_ref[...].astype(BF16), preferred_element_type=F32) + b_ref[...]

    return pl.pallas_call(
        body, name=name, out_shape=jax.ShapeDtypeStruct((M, n), F32), grid=(n // tn,),
        in_specs=[pl.BlockSpec((M, D), lambda j: (0, 0)), pl.BlockSpec((D, tn), lambda j: (0, j)),
                  pl.BlockSpec((1, tn), lambda j: (0, j))],
        out_specs=pl.BlockSpec((M, tn), lambda j: (0, j)), compiler_params=_cparams(("parallel",)),
    )(c_all, ada_w, bias)


def _ada_bwd(c_all, dmod, name):
    M, D = c_all.shape
    n = dmod.shape[1]
    tn = _pick(n, (768, 512, 256, 128))

    def body(c_ref, d_ref, o_ref):
        c_ = c_ref[...]
        a = (c_ * jax.nn.sigmoid(c_)).astype(BF16)
        o_ref[...] = lax.dot_general(a, d_ref[...].astype(BF16), (((0,), (0,)), ((), ())), preferred_element_type=F32)

    return pl.pallas_call(
        body, name=name, out_shape=jax.ShapeDtypeStruct((D, n), F32), grid=(n // tn,),
        in_specs=[pl.BlockSpec((M, D), lambda j: (0, 0)), pl.BlockSpec((M, tn), lambda j: (0, j))],
        out_specs=pl.BlockSpec((D, tn), lambda j: (0, j)), compiler_params=_cparams(("parallel",)),
    )(c_all, dmod)


def _loss_head(y, target, name, tr=256):
    T, D = y.shape

    def body(y_ref, t_ref, l_ref, d_ref):
        e = y_ref[...] - t_ref[...]
        d_ref[...] = e * (1.0 / D)
        part = jnp.sum(jnp.sum(e * e, axis=1, keepdims=True), axis=0, keepdims=True) * (0.5 / D)

        @pl.when(pl.program_id(0) == 0)
        def _():
            l_ref[...] = jnp.zeros_like(l_ref)

        l_ref[...] += jnp.broadcast_to(part, l_ref.shape)

    row = pl.BlockSpec((tr, D), lambda i: (i, 0))
    return pl.pallas_call(
        body, name=name, out_shape=[jax.ShapeDtypeStruct((1, LANES), F32), jax.ShapeDtypeStruct((T, D), F32)],
        grid=(T // tr,), in_specs=[row, row], out_specs=[pl.BlockSpec((1, LANES), lambda i: (0, 0)), row],
        compiler_params=_cparams(("arbitrary",)),
    )(y, target)


def _adamw(w, g, m, v, name):
    shape = w.shape
    C = shape[-1]
    R_ = int(np.prod(shape[:-1]))
    w2, g2, m2, v2 = [a.reshape(R_, C) for a in (w, g, m, v)]
    tr = _pick(R_, (256, 128, 64, 32, 16, 8)) if R_ > 8 else R_
    c1 = 1.0 / (1.0 - ADAM_B1 ** ADAM_STEP)
    c2 = 1.0 / (1.0 - ADAM_B2 ** ADAM_STEP)

    def body(w_ref, g_ref, m_ref, v_ref, d_ref, nm_ref, nv_ref):
        g_ = g_ref[...]
        nm = ADAM_B1 * m_ref[...] + (1.0 - ADAM_B1) * g_
        nv = ADAM_B2 * v_ref[...] + (1.0 - ADAM_B2) * (g_ * g_)
        d_ref[...] = -ADAM_LR * ((nm * c1) / (jnp.sqrt(nv * c2) + ADAM_EPS) + ADAM_WD * w_ref[...])
        nm_ref[...] = nm
        nv_ref[...] = nv

    spec = pl.BlockSpec((tr, C), lambda i: (i, 0))
    outs = pl.pallas_call(
        body, name=name, out_shape=[jax.ShapeDtypeStruct((R_, C), F32)] * 3, grid=(R_ // tr,),
        in_specs=[spec] * 4, out_specs=[spec] * 3, compiler_params=_cparams(("parallel",)),
    )(w2, g2, m2, v2)
    return [o.reshape(shape) for o in outs]


GDN_W = GDN_HEADS * GDN_DIM
IN_SPLITS = (3 * GDN_W, GDN_W, GDN_HEADS, GDN_HEADS, 768, 3 * 768, None)
IN_PAD_GATES = LANES - 2 * GDN_HEADS


def _f_merge(pa, pb, pc, ga, gb, gc):
    return jax.nn.sigmoid(ga) * pa + jax.nn.sigmoid(gb) * pb + jax.nn.sigmoid(gc) * pc


def _f_id_rmsmod(x, sh, sc, g):
    return x, _rms(x, g) * (1.0 + sc) + sh


def _in_layout(D):
    widths = [3 * D, 3 * GDN_W, GDN_W, LANES, 768, 3 * 768]
    offs = np.concatenate([[0], np.cumsum(widths)]).tolist()
    total = -(-offs[-1] // 768) * 768
    return offs, total


def _pad_w_in(w_in):
    D = w_in.shape[0]
    offs, total = _in_layout(D)
    cut = 4 * GDN_W + 2 * GDN_HEADS
    ng = w_in.shape[1] - 3 * D
    return jnp.concatenate([w_in[:, ng:], w_in[:, :cut], jnp.zeros((D, IN_PAD_GATES), w_in.dtype), w_in[:, cut:ng],
                            jnp.zeros((D, total - offs[-1]), w_in.dtype)], axis=1)


def _unpad_w_in(d):
    D = d.shape[0]
    offs, _ = _in_layout(D)
    cut = 4 * GDN_W + 2 * GDN_HEADS
    g0 = 3 * D
    return jnp.concatenate([d[:, g0:g0 + cut], d[:, g0 + cut + IN_PAD_GATES:offs[-1]], d[:, :g0]], axis=1)


def _layer_fwd(x0, mod, W, sm, B, tag):
    T, D = x0.shape
    S = T // B
    sh1, sc1, g1, sh2, sc2, g2, sh3, sc3, g3 = mod
    n1, nm, n3 = sm["norm_ffn1"][None], sm["norm_mix"][None], sm["norm_ffn2"][None]
    (h1,) = _ew_fwd(_f_rmsmod, [x0], [sh1, sc1], [n1], [(D, BF16, False)], seq=S, name=f"{tag}_norm1")
    y1, sv1 = _ffn_fwd(h1, W["ffn1_w1"], W["ffn1_w3"], W["ffn1_w2"], f"{tag}_ffn1")
    x1, h2 = _ew_fwd(functools.partial(_f_res_rmsmod, FFN_RES), [x0, y1], [g1, sh2, sc2], [nm],
                     [(D, F32, False), (D, BF16, False)], seq=S, name=f"{tag}_res1")
    P = _mm(h2, W["w_in"], name=f"{tag}_in")
    offs, _ = _in_layout(D)
    qkv_a, z, ba, u, qkv_c = [(P, off // LANES) for off in offs[1:6]]
    lane8 = lambda v: jnp.pad(v, (GDN_HEADS, LANES - 2 * GDN_HEADS))[None]
    gate_params = (lane8(sm["gdn_a_log"]), lane8(sm["gdn_dt_bias"]))
    out_norm = sm["gdn_out_norm"][None]
    y_a, sva = _gdn_fwd(qkv_a, z, ba, sm["gdn_conv"], gate_params, out_norm, B, f"{tag}_gdn")
    s5p = {k[3:]: v for k, v in sm.items() if k.startswith("s5_")}
    y_b, svb = _s5_fwd(u, s5p, W["s5_glu_w"], B, f"{tag}_s5")
    qn2, kn2 = jnp.tile(sm["dil_q_norm"], 2)[None], jnp.tile(sm["dil_k_norm"], 2)[None]
    y_c, svc = _dil_fwd(qkv_c, qn2, kn2, B, f"{tag}_dil")
    pa = _mm(y_a, W["w_branch_a"], name=f"{tag}_pa")
    pb = _mm(y_b, W["w_branch_b"], name=f"{tag}_pb")
    pc = _mm(y_c, W["w_branch_c"], name=f"{tag}_pc")
    mrows = [pa, pb, pc, (P, D, 0), (P, D, 1), (P, D, 2)]
    (merged,) = _ew_fwd(_f_merge, mrows, [], [], [(D, BF16, False)], tr=128, name=f"{tag}_merge")
    mo = _mm(merged, W["w_out"], name=f"{tag}_out")
    x2, h3 = _ew_fwd(functools.partial(_f_res_rmsmod, 1.0), [x1, mo], [g2, sh3, sc3], [n3],
                     [(D, F32, False), (D, BF16, False)], seq=S, name=f"{tag}_res2")
    y3, sv3 = _ffn_fwd(h3, W["ffn2_w1"], W["ffn2_w3"], W["ffn2_w2"], f"{tag}_ffn2")
    (x3,) = _ew_fwd(functools.partial(_f_res, FFN_RES), [x2, y3], [g3], [], [(D, F32, False)], seq=S,
                    name=f"{tag}_res3")
    saved = dict(x0=x0, x1=x1, x2=x2, y1=y1, y3=y3, mo=mo, h2=h2, sv1=sv1, sv3=sv3, sva=sva, svb=svb, svc=svc,
                 y_a=y_a, y_b=y_b, y_c=y_c, mrows=mrows, merged=merged, gate_params=gate_params, out_norm=out_norm,
                 s5p=s5p, qn2=qn2, kn2=kn2)
    return x3, saved


def _layer_bwd(dx3, sv, mod, W, sm, B, tag):
    T, D = dx3.shape
    S = T // B
    sh1, sc1, g1, sh2, sc2, g2, sh3, sc3, g3 = mod
    n1, nm, n3 = sm["norm_ffn1"][None], sm["norm_mix"][None], sm["norm_ffn2"][None]
    big, small = {}, {}
    dx2, dy3, dg3 = _ew_bwd(functools.partial(_f_res, FFN_RES), [sv["x2"], sv["y3"]], [g3], [], [dx3], [True] * 3,
                            seq=S, row_dtypes=[F32, BF16], name=f"{tag}_res3_bwd")
    dh3, big["ffn2_w1"], big["ffn2_w3"], big["ffn2_w2"] = _ffn_bwd(
        dy3, sv["sv3"], W["ffn2_w1"], W["ffn2_w3"], W["ffn2_w2"], f"{tag}_ffn2")
    dx1, dmo, dg2, dsh3, dsc3, dn3 = _ew_bwd(
        functools.partial(_f_res_rmsmod, 1.0), [sv["x1"], sv["mo"]], [g2, sh3, sc3], [n3], [dx2, dh3], [True] * 6,
        seq=S, row_dtypes=[F32, BF16], name=f"{tag}_res2_bwd")
    big["w_out"] = _mm(sv["merged"], dmo, ta=True, out_dtype=BF16, name=f"{tag}_out_dw")
    dmerged = _mm(dmo, W["w_out"], tb=True, name=f"{tag}_out_dx")
    dpa, dpb, dpc, dga, dgb, dgc = _ew_bwd(_f_merge, sv["mrows"], [], [], [dmerged], [True] * 6, tr=128,
                                           row_dtypes=[BF16] * 6, name=f"{tag}_merge_bwd")
    big["w_branch_a"] = _mm(sv["y_a"], dpa, ta=True, out_dtype=BF16, name=f"{tag}_pa_dw")
    big["w_branch_b"] = _mm(sv["y_b"], dpb, ta=True, out_dtype=BF16, name=f"{tag}_pb_dw")
    big["w_branch_c"] = _mm(sv["y_c"], dpc, ta=True, out_dtype=BF16, name=f"{tag}_pc_dw")
    dy_a = _mm(dpa, W["w_branch_a"], tb=True, name=f"{tag}_pa_dx")
    dy_b = _mm(dpb, W["w_branch_b"], tb=True, name=f"{tag}_pb_dx")
    dy_c = _mm(dpc, W["w_branch_c"], tb=True, name=f"{tag}_pc_dx")
    dqkv_a, dz, dba, gdn_small = _gdn_bwd(dy_a, sv["sva"], sm["gdn_conv"], sv["gate_params"], sv["out_norm"], B,
                                          f"{tag}_gdn")
    du, big["s5_glu_w"], s5_small = _s5_bwd(dy_b, sv["svb"], sv["s5p"], W["s5_glu_w"], B, f"{tag}_s5")
    dqkv_c, dqw, dkw = _dil_bwd(dy_c, sv["svc"], sv["qn2"], sv["kn2"], B, f"{tag}_dil")
    offs, total = _in_layout(D)
    dP = jnp.concatenate([t.astype(BF16) for t in (dga, dgb, dgc, dqkv_a, dz, dba, du, dqkv_c)]
                         + [jnp.zeros((T, total - offs[-1]), BF16)], axis=1)
    big["w_in"] = _mm(sv["h2"], dP, ta=True, out_dtype=BF16, name=f"{tag}_in_dw")
    dh2 = _mm(dP, W["w_in"], tb=True, name=f"{tag}_in_dx")
    dx0a, dy1, dg1, dsh2, dsc2, dnm = _ew_bwd(
        functools.partial(_f_res_rmsmod, FFN_RES), [sv["x0"], sv["y1"]], [g1, sh2, sc2], [nm], [dx1, dh2], [True] * 6,
        seq=S, row_dtypes=[F32, BF16], name=f"{tag}_res1_bwd")
    dh1, big["ffn1_w1"], big["ffn1_w3"], big["ffn1_w2"] = _ffn_bwd(
        dy1, sv["sv1"], W["ffn1_w1"], W["ffn1_w3"], W["ffn1_w2"], f"{tag}_ffn1")
    dx0, dsh1, dsc1, dn1 = _ew_bwd(_f_id_rmsmod, [sv["x0"]], [sh1, sc1], [n1], [dx0a, dh1], [True] * 4, seq=S,
                                   name=f"{tag}_norm1_bwd")
    half = DIL_DIM
    small.update(norm_ffn1=dn1[0], norm_mix=dnm[0], norm_ffn2=dn3[0], gdn_conv=gdn_small["conv"],
                 gdn_a_log=gdn_small["a_log"][0, GDN_HEADS:2 * GDN_HEADS],
                 gdn_dt_bias=gdn_small["dt_bias"][0, GDN_HEADS:2 * GDN_HEADS], gdn_out_norm=gdn_small["out_norm"][0],
                 dil_q_norm=dqw[0, :half] + dqw[0, half:], dil_k_norm=dkw[0, :half] + dkw[0, half:])
    small.update({"s5_" + k: v for k, v in s5_small.items()})
    dmod = [dsh1, dsc1, dg1, dsh2, dsc2, dg2, dsh3, dsc3, dg3]
    return dx0, big, small, dmod


WEIGHTS = ['ada_w', 'ada_b', 'norm_ffn1', 'ffn1_w1', 'ffn1_w3', 'ffn1_w2', 'norm_mix', 'w_in', 'gdn_conv', 'gdn_a_log',
           'gdn_dt_bias', 'gdn_out_norm', 's5_a_re', 's5_a_im', 's5_b_re', 's5_b_im', 's5_c_re', 's5_c_im', 's5_d',
           's5_log_step', 's5_glu_w', 's5_glu_b', 'dil_q_norm', 'dil_k_norm', 'w_branch_a', 'w_branch_b', 'w_branch_c',
           'w_out', 'norm_ffn2', 'ffn2_w1', 'ffn2_w3', 'ffn2_w2']
BIG = dict(ffn1_w1=True, ffn1_w3=True, ffn1_w2=False, w_in=True, s5_glu_w=True, w_branch_a=True, w_branch_b=True,
           w_branch_c=True, w_out=False, ffn2_w1=True, ffn2_w3=True, ffn2_w2=False)
FFN_W = ("ffn1_w1", "ffn1_w3", "ffn1_w2", "ffn2_w1", "ffn2_w3", "ffn2_w2")
SMALL = ['norm_ffn1', 'norm_mix', 'norm_ffn2', 'gdn_conv', 'gdn_a_log', 'gdn_dt_bias', 'gdn_out_norm', 's5_a_re',
         's5_a_im', 's5_b_re', 's5_b_im', 's5_c_re', 's5_c_im', 's5_d', 's5_log_step', 's5_glu_b', 'dil_q_norm',
         'dil_k_norm']


def _full_from_shards(g, cols):
    n, r, c = g.shape
    return jnp.transpose(g, (1, 0, 2)).reshape(r, n * c) if cols else g.reshape(n * r, c)


def _shards_from_full(w, cols):
    if cols:
        r, nc = w.shape
        return jnp.transpose(w.reshape(r, N_DEV, nc // N_DEV), (1, 0, 2))
    nr, c = w.shape
    return w.reshape(N_DEV, nr // N_DEV, c)


def kernel(x, c, ada_w, ada_b, norm_ffn1, ffn1_w1, ffn1_w3, ffn1_w2, norm_mix, w_in, gdn_conv, gdn_a_log, gdn_dt_bias, gdn_out_norm, s5_a_re, s5_a_im, s5_b_re, s5_b_im, s5_c_re, s5_c_im, s5_d, s5_log_step, s5_glu_w, s5_glu_b, dil_q_norm, dil_k_norm, w_branch_a, w_branch_b, w_branch_c, w_out, norm_ffn2, ffn2_w1, ffn2_w3, ffn2_w2, loss_target, m_ada_w, m_ada_b, m_norm_ffn1, m_ffn1_w1, m_ffn1_w3, m_ffn1_w2, m_norm_mix, m_w_in, m_gdn_conv, m_gdn_a_log, m_gdn_dt_bias, m_gdn_out_norm, m_s5_a_re, m_s5_a_im, m_s5_b_re, m_s5_b_im, m_s5_c_re, m_s5_c_im, m_s5_d, m_s5_log_step, m_s5_glu_w, m_s5_glu_b, m_dil_q_norm, m_dil_k_norm, m_w_branch_a, m_w_branch_b, m_w_branch_c, m_w_out, m_norm_ffn2, m_ffn2_w1, m_ffn2_w3, m_ffn2_w2, v_ada_w, v_ada_b, v_norm_ffn1, v_ffn1_w1, v_ffn1_w3, v_ffn1_w2, v_norm_mix, v_w_in, v_gdn_conv, v_gdn_a_log, v_gdn_dt_bias, v_gdn_out_norm, v_s5_a_re, v_s5_a_im, v_s5_b_re, v_s5_b_im, v_s5_c_re, v_s5_c_im, v_s5_d, v_s5_log_step, v_s5_glu_w, v_s5_glu_b, v_dil_q_norm, v_dil_k_norm, v_w_branch_a, v_w_branch_b, v_w_branch_c, v_w_out, v_norm_ffn2, v_ffn2_w1, v_ffn2_w3, v_ffn2_w2):
    env = dict(locals())
    w = {n: env[n] for n in WEIGHTS}
    m = {n: env["m_" + n] for n in WEIGHTS}
    v = {n: env["v_" + n] for n in WEIGHTS}
    L = ada_w.shape[0]
    B, S, D = x.shape
    T = B * S
    me = _flat_index(*_my_place())

    big_keys = [(n, l) for l in range(L) for n in BIG]
    groups = {}
    for n, l in big_keys:
        r, cc = w[n].shape[1:]
        groups.setdefault((BIG[n], r if BIG[n] else cc), []).append((n, l))
    shards = {}
    for (cols, dim), keys in groups.items():
        buf = jnp.concatenate([w[n][l].astype(BF16) for n, l in keys], axis=1 if cols else 0)
        got = _all_gather(buf, f"gather_weights_{'c' if cols else 'r'}{dim}")
        off = 0
        for n, l in keys:
            k = w[n].shape[2] if cols else w[n].shape[1]
            shards[(n, l)] = got[:, :, off:off + k] if cols else got[:, off:off + k, :]
            off += k
    small_in = _pack([jnp.pad(c, ((0, SUBLANES - B), (0, 0))), gdn_conv], F32, LANES)
    c_g, conv_g = _unpack(_all_gather(small_in, "gather_cond").reshape(N_DEV, -1),
                          [(SUBLANES, D), gdn_conv.shape])
    c_all = c_g[:, :B].reshape(N_DEV * B, D)
    conv_full = jnp.transpose(conv_g, (1, 2, 0, 3)).reshape(L, GDN_CONV, -1)
    Ws = []
    for l in range(L):
        Wl = {n: shards[(n, l)] for n in FFN_W}
        Wl.update({n: _full_from_shards(shards[(n, l)], BIG[n])
                   for n in ("s5_glu_w", "w_branch_a", "w_branch_b", "w_branch_c", "w_out")})
        Wl["w_in"] = _pad_w_in(_full_from_shards(shards[("w_in", l)], True))
        Ws.append(Wl)
    sms = [dict({n: w[n][l] for n in SMALL}, gdn_conv=conv_full[l]) for l in range(L)]

    n_ada = ada_w.shape[2]
    bias = lax.dynamic_slice(ada_b, (0, me * n_ada), (L, n_ada))
    mod_cols = jnp.concatenate([_ada_fwd(c_all, ada_w[l], bias[l][None], f"ada{l}") for l in range(L)], axis=0)
    mod_g = _all_gather(mod_cols, "gather_mod").reshape(N_DEV, L, N_DEV * B, n_ada)
    mod_mine = lax.dynamic_slice(mod_g, (0, 0, me * B, 0), (N_DEV, L, B, n_ada))
    mod_mine = jnp.transpose(mod_mine, (1, 2, 0, 3)).reshape(L, B, N_DEV * n_ada)
    mods = [[mod_mine[l][:, None, k * D:(k + 1) * D] for k in range(9)] for l in range(L)]

    h = x.reshape(T, D)
    saved = []
    for l in range(L):
        h, sv = _layer_fwd(h, mods[l], Ws[l], sms[l], B, f"l{l}")
        saved.append(sv)
    loss_row, dh = _loss_head(h, loss_target.reshape(T, D), "loss")
    loss = lax.psum(loss_row[0, 0], ("x", "y", "c"))
    bigs, smalls, dmods = [None] * L, [None] * L, [None] * L
    for l in reversed(range(L)):
        dh, bigs[l], smalls[l], dmods[l] = _layer_bwd(dh, saved[l], mods[l], Ws[l], sms[l], B, f"l{l}")
    grad_x = dh.reshape(B, S, D)

    def grad_pieces(n, l):
        if n in FFN_W:
            return bigs[l][n]
        full = _unpad_w_in(bigs[l]["w_in"]) if n == "w_in" else bigs[l][n]
        return _shards_from_full(full, BIG[n])

    g = dict()
    for (cols, dim), keys in groups.items():
        tag = f"{'c' if cols else 'r'}{dim}"
        buf = jnp.concatenate([grad_pieces(n, l) for n, l in keys], axis=2 if cols else 1)
        summed = _reduce_to_owner(buf, lax.axis_index("c"), tag)
        off = 0
        for n, l in keys:
            k = w[n].shape[2] if cols else w[n].shape[1]
            g.setdefault(n, [None] * L)[l] = summed[:, off:off + k] if cols else summed[off:off + k, :]
            off += k
    g = {n: jnp.stack(ts) for n, ts in g.items()}

    small_keys = [(n, l) for l in range(L) for n in SMALL]
    small_flat = _pack([smalls[l][n] for n, l in small_keys], F32, LANES)
    small_sum = _sum0(_all_gather(small_flat, "gather_small_grads"), "sum_small_grads")
    small_full = {}
    for (n, l), t in zip(small_keys, _unpack(small_sum.reshape(-1), [smalls[l][n].shape for n, l in small_keys])):
        small_full.setdefault(n, [None] * L)[l] = t
    for n, ts in small_full.items():
        g[n] = jnp.stack(ts)
    n_conv = gdn_conv.shape[2]
    g["gdn_conv"] = lax.dynamic_slice(g["gdn_conv"], (0, 0, me * n_conv), (L, GDN_CONV, n_conv))

    dmod_mine = jnp.stack([jnp.concatenate([d[:, 0] for d in dmods[l]], axis=1) for l in range(L)])
    dmod_in = jnp.pad(dmod_mine.reshape(L * B, -1), ((0, SUBLANES - L * B), (0, 0)))
    dmod_g = _all_gather(dmod_in, "gather_dmod")[:, :L * B].reshape(N_DEV, L, B, -1)
    dmod_all = jnp.transpose(dmod_g, (1, 0, 2, 3)).reshape(L, N_DEV * B, -1)
    g["ada_b"] = _sum0(dmod_all.reshape(L, N_DEV * B, -1, LANES).transpose(1, 0, 2, 3).reshape(N_DEV * B, -1, LANES),
                       "sum_ada_b").reshape(L, -1)
    dmod_cols = lax.dynamic_slice(dmod_all, (0, 0, me * n_ada), (L, N_DEV * B, n_ada))
    g["ada_w"] = jnp.stack([_ada_bwd(c_all, dmod_cols[l], f"ada{l}_bwd") for l in range(L)])

    upd = {n: _adamw(w[n], g[n], m[n], v[n], f"adamw_{n}") for n in WEIGHTS}
    return (loss, grad_x, *[g[n] for n in WEIGHTS], *[upd[n][0] for n in WEIGHTS],
            *[upd[n][1] for n in WEIGHTS], *[upd[n][2] for n in WEIGHTS])
```

```python
import functools
import math

import jax
import jax.numpy as jnp
import numpy as np
from jax import lax
from jax.experimental import pallas as pl
from jax.experimental.pallas import tpu as pltpu

F32 = jnp.float32
BF16 = jnp.bfloat16

LANES = 128
SUBLANES = 8
VMEM_LIMIT = 56 * 1024 * 1024

N_DEV = 8
EPS = 1e-6
FFN_RES = 0.5
GDN_HEADS = 8
GDN_DIM = 128
GDN_CONV = 4
GDN_CHUNK = 128
S5_GROUP = 16
S5_STATE = 64
S5_MAX_RE = -1e-4
S5_TCHUNK = 512
DIL_PAIRS = ((128, 1), (512, 4), (2048, 16))
DIL_HPG = 4
DIL_DIM = 64
DIL_BLK = 128
ALIBI_MAX = 8.0
ADAM_LR, ADAM_B1, ADAM_B2, ADAM_EPS, ADAM_WD, ADAM_STEP = 0.001, 0.9, 0.999, 1e-08, 0.01, 10

HI = lax.Precision.HIGHEST
HI3 = lax.Precision.HIGH


def _cparams(sem=None, **kw):
    return pltpu.CompilerParams(dimension_semantics=sem, vmem_limit_bytes=VMEM_LIMIT, **kw)


def _pick(n, cands):
    for c in cands:
        if n % c == 0:
            return c
    return n


def _mm(a, b, *, ta=False, tb=False, out_dtype=F32, name):
    M, K = (a.shape[1], a.shape[0]) if ta else a.shape
    N = b.shape[0] if tb else b.shape[1]
    assert (b.shape[1] if tb else b.shape[0]) == K, (a.shape, b.shape, ta, tb)
    tm = _pick(M, (1024, 512, 256, 128))
    tn = _pick(N, (1024, 768, 512, 384, 256, 128))
    tk = _pick(K, (2048, 1536, 1408, 1024, 768, 512, 256, 128))
    nk = K // tk

    def body(a_ref, b_ref, o_ref, acc_ref):
        k = pl.program_id(2)

        @pl.when(k == 0)
        def _():
            acc_ref[...] = jnp.zeros_like(acc_ref)

        dn = (((0 if ta else 1,), (1 if tb else 0,)), ((), ()))
        acc_ref[...] += lax.dot_general(a_ref[...], b_ref[...], dn, preferred_element_type=F32)

        @pl.when(k == nk - 1)
        def _():
            o_ref[...] = acc_ref[...].astype(o_ref.dtype)

    a_spec = pl.BlockSpec((tk, tm), lambda i, j, k: (k, i)) if ta else pl.BlockSpec((tm, tk), lambda i, j, k: (i, k))
    b_spec = pl.BlockSpec((tn, tk), lambda i, j, k: (j, k)) if tb else pl.BlockSpec((tk, tn), lambda i, j, k: (k, j))
    return pl.pallas_call(
        body, name=name,
        out_shape=jax.ShapeDtypeStruct((M, N), out_dtype),
        grid=(M // tm, N // tn, nk),
        in_specs=[a_spec, b_spec],
        out_specs=pl.BlockSpec((tm, tn), lambda i, j, k: (i, j)),
        scratch_shapes=[pltpu.VMEM((tm, tn), F32)],
        compiler_params=_cparams(("parallel", "parallel", "arbitrary")),
    )(a, b)


def _norm_arg(a):
    return a if isinstance(a, tuple) else (a, None, 0)


def _ew_specs(rows, exs, ws, tr, tpe):
    specs = []
    for arr, cw, off in rows:
        if cw is None:
            specs.append(pl.BlockSpec((tr, arr.shape[1]), lambda j, i: (i, 0)))
        else:
            specs.append(pl.BlockSpec((tr, cw), lambda j, i, off=off: (i, j + off)))
    for arr, cw, off in exs:
        if cw is None:
            specs.append(pl.BlockSpec((1, 1, arr.shape[2]), lambda j, i: (i // tpe, 0, 0)))
        else:
            specs.append(pl.BlockSpec((1, 1, cw), lambda j, i, off=off: (i // tpe, 0, j + off)))
    for arr, cw, off in ws:
        if cw is None:
            specs.append(pl.BlockSpec((1, arr.shape[1]), lambda j, i: (0, 0)))
        else:
            specs.append(pl.BlockSpec((1, cw), lambda j, i, off=off: (0, j + off)))
    return specs


def _ew_fwd(fn, rows, exs, ws, outs, *, ncb=1, tr=256, seq=None, name):
    rows, exs, ws = [list(map(_norm_arg, g)) for g in (rows, exs, ws)]
    T = rows[0][0].shape[0]
    seq = seq or T
    tr = min(tr, seq)
    tpe = seq // tr
    nr, ne, nw = len(rows), len(exs), len(ws)

    def body(*refs):
        ins = [r[...].astype(F32) for r in refs[:nr]]
        ins += [r[0].astype(F32) for r in refs[nr:nr + ne]]
        ins += [r[...].astype(F32) for r in refs[nr + ne:nr + ne + nw]]
        res = fn(*ins)
        if not isinstance(res, (tuple, list)):
            res = (res,)
        for o_ref, v in zip(refs[nr + ne + nw:], res):
            o_ref[...] = v.astype(o_ref.dtype)

    out_shape, out_specs = [], []
    for width, dtype, blocked in outs:
        out_shape.append(jax.ShapeDtypeStruct((T, width), dtype))
        if blocked:
            out_specs.append(pl.BlockSpec((tr, width // ncb), lambda j, i: (i, j)))
        else:
            out_specs.append(pl.BlockSpec((tr, width), lambda j, i: (i, 0)))
    res = pl.pallas_call(
        body, name=name, out_shape=out_shape, grid=(ncb, T // tr),
        in_specs=_ew_specs(rows, exs, ws, tr, tpe), out_specs=out_specs,
        compiler_params=_cparams(("parallel", "parallel")),
    )(*[a[0] for a in rows + exs + ws])
    return res


def _ew_bwd(fn, rows, exs, ws, douts, need, *, ncb=1, tr=256, seq=None, row_dtypes=None, name):
    rows, exs, ws, douts = [list(map(_norm_arg, g)) for g in (rows, exs, ws, douts)]
    T = rows[0][0].shape[0]
    seq = seq or T
    tr = min(tr, seq)
    tpe = seq // tr
    nrt = T // tr
    nr, ne, nw, nd = len(rows), len(exs), len(ws), len(douts)
    nin = nr + ne + nw
    args = rows + exs + ws
    row_dtypes = row_dtypes or [F32] * nr
    for k, (arr, cw, off) in enumerate(exs):
        assert not (need[nr + k] and cw is None and ncb > 1)

    def body(*refs):
        j, i = pl.program_id(0), pl.program_id(1)
        ins = [r[...].astype(F32) for r in refs[:nr]]
        ins += [r[0].astype(F32) for r in refs[nr:nr + ne]]
        ins += [r[...].astype(F32) for r in refs[nr + ne:nin]]
        cts = [r[...].astype(F32) for r in refs[nin:nin + nd]]
        res, vjp = jax.vjp(fn, *ins)
        if isinstance(res, (tuple, list)):
            grads = vjp(tuple(cts))
        else:
            grads = vjp(cts[0])
        o = nin + nd
        for k in range(nin):
            if not need[k]:
                continue
            o_ref, g = refs[o], grads[k]
            o += 1
            if k < nr:
                o_ref[...] = g.astype(o_ref.dtype)
            elif k < nr + ne:
                first = (i % tpe) == 0

                @pl.when(first)
                def _(o_ref=o_ref, g=g):
                    o_ref[0] = g

                @pl.when(jnp.logical_not(first))
                def _(o_ref=o_ref, g=g):
                    o_ref[0] += g
            else:
                blocked = args[k][1] is not None
                first = (i == 0) if blocked else jnp.logical_and(i == 0, j == 0)

                @pl.when(first)
                def _(o_ref=o_ref, g=g):
                    o_ref[...] = g

                @pl.when(jnp.logical_not(first))
                def _(o_ref=o_ref, g=g):
                    o_ref[...] += g

    in_specs = _ew_specs(rows, exs, ws, tr, tpe) + _ew_specs(douts, [], [], tr, tpe)
    out_shape, out_specs = [], []
    all_specs = _ew_specs(rows, exs, ws, tr, tpe)
    for k in range(nin):
        if not need[k]:
            continue
        arr, cw, off = args[k]
        if k < nr and cw is not None:
            out_shape.append(jax.ShapeDtypeStruct((T, ncb * cw), row_dtypes[k]))
            out_specs.append(pl.BlockSpec((tr, cw), lambda j, i: (i, j)))
        elif k >= nr and cw is not None:
            assert off == 0 and arr.shape[-1] == ncb * cw
            out_shape.append(jax.ShapeDtypeStruct(arr.shape, F32))
            out_specs.append(all_specs[k])
        else:
            out_shape.append(jax.ShapeDtypeStruct(arr.shape, row_dtypes[k] if k < nr else F32))
            out_specs.append(all_specs[k])
    res = pl.pallas_call(
        body, name=name, out_shape=out_shape, grid=(ncb, nrt),
        in_specs=in_specs, out_specs=out_specs,
        compiler_params=_cparams(("arbitrary", "arbitrary")),
    )(*[a[0] for a in args + douts])
    return res


def _rms(x, g):
    return x * lax.rsqrt(jnp.mean(x * x, axis=-1, keepdims=True) + EPS) * g


def _f_rmsmod(x, sh, sc, g):
    return _rms(x, g) * (1.0 + sc) + sh


def _f_swiglu(a, b):
    return a * jax.nn.sigmoid(a) * b


def _f_res(res, x, y, gate):
    return x + res * gate * y


def _f_res_rmsmod(res, x, y, gate, sh, sc, g):
    x1 = x + res * gate * y
    return x1, _rms(x1, g) * (1.0 + sc) + sh


def _mm_sh(step, ins, in_specs, out_shape, out_spec, acc_shape, grid, name):
    nk = grid[2]
    n = len(ins)

    def body(*refs):
        o_ref, acc_ref = refs[n], refs[n + 1]
        k = pl.program_id(2)

        @pl.when(k == 0)
        def _():
            acc_ref[...] = jnp.zeros_like(acc_ref)

        step(k, acc_ref, *refs[:n])

        @pl.when(k == nk - 1)
        def _():
            if len(o_ref.shape) == 3:
                o_ref[0] = acc_ref[...].astype(o_ref.dtype)
            else:
                o_ref[...] = acc_ref[...].astype(o_ref.dtype)

    return pl.pallas_call(
        body, name=name, out_shape=out_shape, grid=grid, in_specs=in_specs, out_specs=out_spec,
        scratch_shapes=[pltpu.VMEM(acc_shape, F32)],
        compiler_params=_cparams(("parallel", "parallel", "arbitrary")),
    )(*ins)


def _dg(a, b, ca, cb):
    return lax.dot_general(a, b, (((ca,), (cb,)), ((), ())), preferred_element_type=F32)


def _ffn_tiles(T, D):
    return _pick(T, (1024, 512, 256, 128)), _pick(D, (1024, 512, 256, 128)), _pick(D, (2048, 1024, 512, 256, 128)), \
        _pick(T, (2048, 1024, 512, 256, 128))


def _ffn_up(h, ws, name):
    n, D, c = ws.shape
    T = h.shape[0]
    tm, _, tkd, _ = _ffn_tiles(T, D)

    def step(k, acc, a_ref, b_ref):
        acc[...] += _dg(a_ref[...], b_ref[0], 1, 0)

    return _mm_sh(step, [h, ws],
                  [pl.BlockSpec((tm, tkd), lambda i, j, k: (i, k)), pl.BlockSpec((1, tkd, c), lambda i, j, k: (j, k, 0))],
                  jax.ShapeDtypeStruct((n, T, c), F32), pl.BlockSpec((1, tm, c), lambda i, j, k: (j, i, 0)),
                  (tm, c), (T // tm, n, D // tkd), name)


def _ffn_down(s3, w2s, name):
    n, T, c = s3.shape
    D = w2s.shape[2]
    tm, tn, _, _ = _ffn_tiles(T, D)

    def step(k, acc, a_ref, b_ref):
        acc[...] += _dg(a_ref[0], b_ref[0], 1, 0)

    return _mm_sh(step, [s3, w2s],
                  [pl.BlockSpec((1, tm, c), lambda i, j, k: (k, i, 0)), pl.BlockSpec((1, c, tn), lambda i, j, k: (k, 0, j))],
                  jax.ShapeDtypeStruct((T, D), F32), pl.BlockSpec((tm, tn), lambda i, j, k: (i, j)),
                  (tm, tn), (T // tm, D // tn, n), name)


def _ffn_down_dx(dy, w2s, name):
    n, c, D = w2s.shape
    T = dy.shape[0]
    tm, _, tkd, _ = _ffn_tiles(T, D)

    def step(k, acc, a_ref, b_ref):
        acc[...] += _dg(a_ref[...], b_ref[0], 1, 1)

    return _mm_sh(step, [dy, w2s],
                  [pl.BlockSpec((tm, tkd), lambda i, j, k: (i, k)), pl.BlockSpec((1, c, tkd), lambda i, j, k: (j, 0, k))],
                  jax.ShapeDtypeStruct((n, T, c), F32), pl.BlockSpec((1, tm, c), lambda i, j, k: (j, i, 0)),
                  (tm, c), (T // tm, n, D // tkd), name)


def _ffn_down_dw(s3, dy, name):
    n, T, c = s3.shape
    D = dy.shape[1]
    _, tn, _, tkt = _ffn_tiles(T, D)

    def step(k, acc, a_ref, b_ref):
        acc[...] += _dg(a_ref[0], b_ref[...], 0, 0)

    return _mm_sh(step, [s3, dy],
                  [pl.BlockSpec((1, tkt, c), lambda i, j, k: (i, k, 0)), pl.BlockSpec((tkt, tn), lambda i, j, k: (k, j))],
                  jax.ShapeDtypeStruct((n, c, D), BF16), pl.BlockSpec((1, c, tn), lambda i, j, k: (i, 0, j)),
                  (c, tn), (n, D // tn, T // tkt), name)


def _ffn_up_dw(h, d3, name):
    n, T, c = d3.shape
    D = h.shape[1]
    _, tn, _, tkt = _ffn_tiles(T, D)

    def step(k, acc, a_ref, b_ref):
        acc[...] += _dg(a_ref[...], b_ref[0], 0, 0)

    return _mm_sh(step, [h, d3],
                  [pl.BlockSpec((tkt, tn), lambda i, j, k: (k, i)), pl.BlockSpec((1, tkt, c), lambda i, j, k: (j, k, 0))],
                  jax.ShapeDtypeStruct((n, D, c), BF16), pl.BlockSpec((1, tn, c), lambda i, j, k: (j, i, 0)),
                  (tn, c), (D // tn, n, T // tkt), name)


def _ffn_up_dx(da3, db3, w1s, w3s, name):
    n, T, c = da3.shape
    D = w1s.shape[1]
    tm, tn, _, _ = _ffn_tiles(T, D)

    def step(k, acc, da_ref, db_ref, w1_ref, w3_ref):
        @pl.when(k < n)
        def _():
            acc[...] += _dg(da_ref[0], w1_ref[0], 1, 1)

        @pl.when(k >= n)
        def _():
            acc[...] += _dg(db_ref[0], w3_ref[0], 1, 1)

    lo = lambda k: jnp.minimum(k, n - 1)
    hi = lambda k: jnp.maximum(k - n, 0)
    return _mm_sh(step, [da3, db3, w1s, w3s],
                  [pl.BlockSpec((1, tm, c), lambda i, j, k: (lo(k), i, 0)), pl.BlockSpec((1, tm, c), lambda i, j, k: (hi(k), i, 0)),
                   pl.BlockSpec((1, tn, c), lambda i, j, k: (lo(k), j, 0)), pl.BlockSpec((1, tn, c), lambda i, j, k: (hi(k), j, 0))],
                  jax.ShapeDtypeStruct((T, D), F32), pl.BlockSpec((tm, tn), lambda i, j, k: (i, j)),
                  (tm, tn), (T // tm, D // tn, 2 * n), name)


def _ffn_act(a3, b3, name, tr=512):
    n, T, c = a3.shape
    tr = _pick(T, (tr, 256, 128))

    def body(a_ref, b_ref, o_ref):
        o_ref[...] = _f_swiglu(a_ref[...], b_ref[...]).astype(o_ref.dtype)

    spec = pl.BlockSpec((1, tr, c), lambda j, i: (j, i, 0))
    return pl.pallas_call(body, name=name, out_shape=jax.ShapeDtypeStruct((n, T, c), BF16), grid=(n, T // tr),
                          in_specs=[spec, spec], out_specs=spec, compiler_params=_cparams(("parallel", "parallel")))(a3, b3)


def _ffn_act_bwd(a3, b3, ds3, name, tr=512):
    n, T, c = a3.shape
    tr = _pick(T, (tr, 256, 128))

    def body(a_ref, b_ref, d_ref, da_ref, db_ref):
        _, vjp = jax.vjp(_f_swiglu, a_ref[...], b_ref[...])
        da, db = vjp(d_ref[...])
        da_ref[...] = da.astype(da_ref.dtype)
        db_ref[...] = db.astype(db_ref.dtype)

    spec = pl.BlockSpec((1, tr, c), lambda j, i: (j, i, 0))
    return pl.pallas_call(body, name=name, out_shape=[jax.ShapeDtypeStruct((n, T, c), BF16)] * 2, grid=(n, T // tr),
                          in_specs=[spec] * 3, out_specs=[spec, spec],
                          compiler_params=_cparams(("parallel", "parallel")))(a3, b3, ds3)


def _ffn_fwd(h, w1s, w3s, w2s, tag):
    a3 = _ffn_up(h, w1s, f"{tag}_up1")
    b3 = _ffn_up(h, w3s, f"{tag}_up3")
    s3 = _ffn_act(a3, b3, f"{tag}_act")
    return _ffn_down(s3, w2s, f"{tag}_down"), (h, a3, b3, s3)


def _ffn_bwd(dy, saved, w1s, w3s, w2s, tag):
    h, a3, b3, s3 = saved
    ds3 = _ffn_down_dx(dy, w2s, f"{tag}_down_dx")
    dw2s = _ffn_down_dw(s3, dy, f"{tag}_down_dw")
    da3, db3 = _ffn_act_bwd(a3, b3, ds3, f"{tag}_act_bwd")
    dw1s = _ffn_up_dw(h, da3, f"{tag}_up1_dw")
    dw3s = _ffn_up_dw(h, db3, f"{tag}_up3_dw")
    dh = _ffn_up_dx(da3, db3, w1s, w3s, f"{tag}_up_dx")
    return dh, dw1s, dw3s, dw2s


def _shift_down(x, j):
    if j == 0:
        return x
    row = lax.broadcasted_iota(jnp.int32, x.shape, 0)
    return jnp.where(row >= j, pltpu.roll(x, j, 0), 0.0)


def _shift_up(x, j):
    if j == 0:
        return x
    n = x.shape[0]
    row = lax.broadcasted_iota(jnp.int32, x.shape, 0)
    return jnp.where(row < n - j, pltpu.roll(x, n - j, 0), 0.0)


def _gdn_post_conv(kind, y):
    s = y * jax.nn.sigmoid(y)
    if kind == "v":
        return s
    n = lax.rsqrt(jnp.sum(s * s, axis=-1, keepdims=True) + EPS)
    return s * n * (GDN_DIM ** -0.5 if kind == "q" else 1.0)


def _conv_taps(w_ref):
    return [w_ref[k:k + 1, :] for k in range(GDN_CONV)]


def _gdn_conv(x, w):
    y = w[GDN_CONV - 1] * x
    for k in range(GDN_CONV - 1):
        y = y + w[k] * _shift_down(x, GDN_CONV - 1 - k)
    return y


def _gdn_pre_fwd(proj, conv_w, kind, first_block, B, name):
    T = proj.shape[0]
    S = T // B
    nh = GDN_HEADS

    def body(x_ref, w_ref, o_ref):
        y = _gdn_conv(x_ref[...], _conv_taps(w_ref))
        o_ref[...] = _gdn_post_conv(kind, y)

    return pl.pallas_call(
        body, name=name, out_shape=jax.ShapeDtypeStruct((T, nh * GDN_DIM), F32), grid=(nh, B),
        in_specs=[pl.BlockSpec((S, GDN_DIM), lambda c, b: (b, c + first_block)),
                  pl.BlockSpec((GDN_CONV, GDN_DIM), lambda c, b: (0, c))],
        out_specs=pl.BlockSpec((S, GDN_DIM), lambda c, b: (b, c)),
        compiler_params=_cparams(("parallel", "parallel")),
    )(proj, conv_w)


def _gdn_pre_bwd(proj, conv_w, dout, kind, first_block, B, name):
    T = proj.shape[0]
    S = T // B
    nh = GDN_HEADS

    def body(x_ref, w_ref, d_ref, dx_ref, dw_ref):
        b = pl.program_id(1)
        x, w = x_ref[...], _conv_taps(w_ref)
        y = _gdn_conv(x, w)
        _, vjp = jax.vjp(functools.partial(_gdn_post_conv, kind), y)
        (dy,) = vjp(d_ref[...])
        dx = w[GDN_CONV - 1] * dy

        @pl.when(b == 0)
        def _():
            dw_ref[...] = jnp.zeros_like(dw_ref)

        for k in range(GDN_CONV):
            j = GDN_CONV - 1 - k
            if j:
                dx = dx + w[k] * _shift_up(dy, j)
            dw_ref[k:k + 1, :] += jnp.sum(dy * _shift_down(x, j), axis=0, keepdims=True)
        dx_ref[...] = dx

    return pl.pallas_call(
        body, name=name,
        out_shape=[jax.ShapeDtypeStruct((T, nh * GDN_DIM), F32), jax.ShapeDtypeStruct((GDN_CONV, nh * GDN_DIM), F32)],
        grid=(nh, B),
        in_specs=[pl.BlockSpec((S, GDN_DIM), lambda c, b: (b, c + first_block)),
                  pl.BlockSpec((GDN_CONV, GDN_DIM), lambda c, b: (0, c)),
                  pl.BlockSpec((S, GDN_DIM), lambda c, b: (b, c))],
        out_specs=[pl.BlockSpec((S, GDN_DIM), lambda c, b: (b, c)),
                   pl.BlockSpec((GDN_CONV, GDN_DIM), lambda c, b: (0, c))],
        compiler_params=_cparams(("arbitrary", "arbitrary")),
    )(proj, conv_w, dout)


def _softplus(x):
    return jnp.maximum(x, 0.0) + jnp.log(1.0 + jnp.exp(-jnp.abs(x)))


def _f_gdn_gates(ba, a_log, dt_bias):
    n = ba.shape[0]
    lane = lax.broadcasted_iota(jnp.int32, ba.shape, 1)
    beta = jax.nn.sigmoid(ba)
    g = -jnp.exp(a_log) * _softplus(ba + dt_bias)
    ri = lax.broadcasted_iota(jnp.int32, (n, n), 0)
    ci = lax.broadcasted_iota(jnp.int32, (n, n), 1)
    tri = jnp.where((ri // GDN_CHUNK == ci // GDN_CHUNK) & (ci <= ri), 1.0, 0.0)
    gc = jnp.dot(tri, g, precision=HI, preferred_element_type=F32)
    return jnp.where(lane < GDN_HEADS, beta, gc)


def _bmm(a, b, ca, cb):
    return lax.dot_general(a, b, (((ca,), (cb,)), ((0,), (0,))), precision=HI3, preferred_element_type=F32)


GDN_INV_LEAF = 16


def _unit_lower_inverse(low, ri, ci):
    C = low.shape[1]
    b = GDN_INV_LEAF
    p = jnp.where(ri // b == ci // b, low, 0.0)
    x = jnp.where(ci == ri, 1.0, 0.0) - p
    for _ in range(int(math.log2(b)) - 1):
        p = _bmm(p, p, 2, 1)
        x = x + _bmm(x, p, 2, 1)
    while b < C:
        off = jnp.where(jnp.logical_and(ri // (2 * b) == ci // (2 * b), ri // b != ci // b), low, 0.0)
        x = x - _bmm(_bmm(x, off, 2, 1), x, 2, 1)
        b *= 2
    return x


def _gdn_chunk_fn(q, k, v, gc, beta, h):
    N, C, d = q.shape
    ri = lax.broadcasted_iota(jnp.int32, (N, C, C), 1)
    ci = lax.broadcasted_iota(jnp.int32, (N, C, C), 2)
    kb = k * beta
    vb = v * beta
    gi = jnp.broadcast_to(gc, (N, C, C))
    gj = jnp.swapaxes(gi, 1, 2)
    decay = jnp.exp(jnp.where(ci <= ri, gi - gj, -1e30))
    low = jnp.where(ci < ri, _bmm(kb, k, 2, 2) * decay, 0.0)
    ainv = _unit_lower_inverse(low, ri, ci)
    eg = jnp.exp(gc)
    u = _bmm(ainv, vb, 2, 1)
    w = _bmm(ainv, kb * eg, 2, 1)
    attn = _bmm(q, k, 2, 2) * decay
    v_new = u - _bmm(w, h, 2, 1)
    o = _bmm(q * eg, h, 2, 1) + _bmm(attn, v_new, 2, 1)
    rc = lax.broadcasted_iota(jnp.int32, (N, C, 1), 1)
    g_last = jnp.sum(jnp.where(rc == C - 1, gc, 0.0), axis=1, keepdims=True)
    h_new = h * jnp.exp(g_last) + _bmm(k * jnp.exp(g_last - gc), v_new, 1, 1)
    return o, h_new


def _gdn_heads(x):
    return jnp.stack([x[:, h * GDN_DIM:(h + 1) * GDN_DIM] for h in range(GDN_HEADS)], axis=0)


def _gdn_gate_cols(G, first_lane):
    lane = lax.broadcasted_iota(jnp.int32, G.shape, 1)
    return jnp.stack([jnp.sum(jnp.where(lane == first_lane + h, G, 0.0), axis=1, keepdims=True)
                      for h in range(GDN_HEADS)], axis=0)


def _gdn_chunk_specs(nc, rev):
    C, W = GDN_CHUNK, GDN_HEADS * GDN_DIM

    def at(n):
        return nc - 1 - n if rev else n

    row = lambda b, n: (b * nc + at(n), 0)
    return [pl.BlockSpec((C, W), row)] * 3 + [pl.BlockSpec((C, LANES), row)]


def _gdn_scan_fwd(q, k, v, G, B, name):
    T, W = q.shape
    C = GDN_CHUNK
    nc = T // B // C

    def body(q_ref, k_ref, v_ref, g_ref, o_ref, hs_ref, h_ref):
        @pl.when(pl.program_id(1) == 0)
        def _():
            h_ref[...] = jnp.zeros_like(h_ref)

        G_ = g_ref[...]
        h = h_ref[...]
        hs_ref[0, 0] = h
        o, hn = _gdn_chunk_fn(_gdn_heads(q_ref[...]), _gdn_heads(k_ref[...]), _gdn_heads(v_ref[...]),
                              _gdn_gate_cols(G_, GDN_HEADS), _gdn_gate_cols(G_, 0), h)
        h_ref[...] = hn
        for hd in range(GDN_HEADS):
            o_ref[:, hd * GDN_DIM:(hd + 1) * GDN_DIM] = o[hd]

    return pl.pallas_call(
        body, name=name,
        out_shape=[jax.ShapeDtypeStruct((T, W), F32), jax.ShapeDtypeStruct((B, nc, GDN_HEADS, GDN_DIM, GDN_DIM), F32)],
        grid=(B, nc), in_specs=_gdn_chunk_specs(nc, False),
        out_specs=[pl.BlockSpec((C, W), lambda b, n: (b * nc + n, 0)),
                   pl.BlockSpec((1, 1, GDN_HEADS, GDN_DIM, GDN_DIM), lambda b, n: (b, n, 0, 0, 0))],
        scratch_shapes=[pltpu.VMEM((GDN_HEADS, GDN_DIM, GDN_DIM), F32)],
        compiler_params=_cparams(("parallel", "arbitrary")),
    )(q, k, v, G)


def _gdn_scan_bwd(q, k, v, G, hs, do, B, name):
    T, W = q.shape
    C = GDN_CHUNK
    nc = T // B // C

    def body(q_ref, k_ref, v_ref, g_ref, hs_ref, do_ref, dq_ref, dk_ref, dv_ref, dg_ref, dh_ref):
        @pl.when(pl.program_id(1) == 0)
        def _():
            dh_ref[...] = jnp.zeros_like(dh_ref)

        G_ = g_ref[...]
        args = (_gdn_heads(q_ref[...]), _gdn_heads(k_ref[...]), _gdn_heads(v_ref[...]),
                _gdn_gate_cols(G_, GDN_HEADS), _gdn_gate_cols(G_, 0), hs_ref[0, 0])
        _, vjp = jax.vjp(_gdn_chunk_fn, *args)
        dq, dk, dv, dgc, dbeta, dh = vjp((_gdn_heads(do_ref[...]), dh_ref[...]))
        dh_ref[...] = dh
        lane = lax.broadcasted_iota(jnp.int32, G_.shape, 1)
        dG = jnp.zeros_like(G_)
        for hd in range(GDN_HEADS):
            sl = slice(hd * GDN_DIM, (hd + 1) * GDN_DIM)
            dq_ref[:, sl] = dq[hd]
            dk_ref[:, sl] = dk[hd]
            dv_ref[:, sl] = dv[hd]
            dG = dG + jnp.where(lane == hd, dbeta[hd], 0.0) + jnp.where(lane == GDN_HEADS + hd, dgc[hd], 0.0)
        dg_ref[...] = dG

    rrow = lambda b, n: (b * nc + nc - 1 - n, 0)
    return pl.pallas_call(
        body, name=name,
        out_shape=[jax.ShapeDtypeStruct((T, W), F32)] * 3 + [jax.ShapeDtypeStruct((T, LANES), F32)],
        grid=(B, nc),
        in_specs=_gdn_chunk_specs(nc, True) + [
            pl.BlockSpec((1, 1, GDN_HEADS, GDN_DIM, GDN_DIM), lambda b, n: (b, nc - 1 - n, 0, 0, 0)),
            pl.BlockSpec((C, W), rrow)],
        out_specs=[pl.BlockSpec((C, W), rrow)] * 3 + [pl.BlockSpec((C, LANES), rrow)],
        scratch_shapes=[pltpu.VMEM((GDN_HEADS, GDN_DIM, GDN_DIM), F32)],
        compiler_params=_cparams(("parallel", "arbitrary")),
    )(q, k, v, G, hs, do)


def _f_gdn_out(o, z, w):
    return _rms(o, w) * z * jax.nn.sigmoid(z)


def _gdn_fwd(qkv, z, ba, conv_w, gate_params, out_norm, B, tag):
    a_log, dt_bias = gate_params
    W = GDN_HEADS * GDN_DIM
    qn, kn, vn = [_gdn_pre_fwd(qkv[0], conv_w[:, i * W:(i + 1) * W], kd, qkv[1] + i * GDN_HEADS, B, f"{tag}_pre_{kd}")
                  for i, kd in enumerate("qkv")]
    (G,) = _ew_fwd(_f_gdn_gates, [(ba[0], LANES, ba[1])], [], [a_log, dt_bias], [(LANES, F32, False)],
                   name=f"{tag}_gates")
    o, hs = _gdn_scan_fwd(qn, kn, vn, G, B, f"{tag}_scan")
    (y,) = _ew_fwd(_f_gdn_out, [(o, GDN_DIM, 0), (z[0], GDN_DIM, z[1])], [], [out_norm], [(W, BF16, True)],
                   ncb=GDN_HEADS, name=f"{tag}_out")
    return y, (qkv, z, ba, qn, kn, vn, G, hs, o)


def _gdn_bwd(dy, saved, conv_w, gate_params, out_norm, B, tag):
    qkv, z, ba, qn, kn, vn, G, hs, o = saved
    a_log, dt_bias = gate_params
    W = GDN_HEADS * GDN_DIM
    do, dz, d_out_norm = _ew_bwd(_f_gdn_out, [(o, GDN_DIM, 0), (z[0], GDN_DIM, z[1])], [], [out_norm],
                                 [(dy, GDN_DIM, 0)], [True] * 3, ncb=GDN_HEADS, name=f"{tag}_out_bwd")
    dq, dk, dv, dG = _gdn_scan_bwd(qn, kn, vn, G, hs, do, B, f"{tag}_scan_bwd")
    dba, d_a_log, d_dt_bias = _ew_bwd(_f_gdn_gates, [(ba[0], LANES, ba[1])], [], [a_log, dt_bias], [dG], [True] * 3,
                                      name=f"{tag}_gates_bwd")
    dxs, dws = [], []
    for i, (kd, d) in enumerate(zip("qkv", (dq, dk, dv))):
        dx, dw = _gdn_pre_bwd(qkv[0], conv_w[:, i * W:(i + 1) * W], d, kd, qkv[1] + i * GDN_HEADS, B,
                              f"{tag}_pre_{kd}_bwd")
        dxs.append(dx)
        dws.append(dw)
    return (jnp.concatenate(dxs, axis=1), dz, dba,
            dict(conv=jnp.concatenate(dws, axis=1), a_log=d_a_log, dt_bias=d_dt_bias, out_norm=d_out_norm))


S5_GPB = LANES // S5_GROUP
S5_SLANES = S5_GPB * S5_STATE


def _cmul(ar, ai, br, bi):
    return ar * br - ai * bi, ar * bi + ai * br


def _s5_prep_fn(a_re, a_im, ls, b_re, b_im):
    lr = jnp.minimum(a_re, S5_MAX_RE)
    li = a_im
    step = jnp.exp(ls)
    mag = jnp.exp(lr * step)
    lbr, lbi = mag * jnp.cos(li * step), mag * jnp.sin(li * step)
    den = lr * lr + li * li
    cr = ((lbr - 1.0) * lr + lbi * li) / den
    ci = (lbi * lr - (lbr - 1.0) * li) / den
    bbr = cr[:, None, :] * b_re - ci[:, None, :] * b_im
    bbi = cr[:, None, :] * b_im + ci[:, None, :] * b_re
    return lbr, lbi, bbr, bbi


def _s5_prep_fwd(args, name):
    G, I, P = args[3].shape
    shp = [jax.ShapeDtypeStruct((G, P), F32)] * 2 + [jax.ShapeDtypeStruct((G, I, P), F32)] * 2

    def body(*refs):
        for o_ref, v in zip(refs[5:], _s5_prep_fn(*[r[...] for r in refs[:5]])):
            o_ref[...] = v

    return pl.pallas_call(body, name=name, out_shape=shp, compiler_params=_cparams())(*args)


def _s5_prep_bwd(args, cts, name):
    shp = [jax.ShapeDtypeStruct(a.shape, F32) for a in args]

    def body(*refs):
        _, vjp = jax.vjp(_s5_prep_fn, *[r[...] for r in refs[:5]])
        for o_ref, v in zip(refs[9:], vjp(tuple(r[...] for r in refs[5:9]))):
            o_ref[...] = v

    return pl.pallas_call(body, name=name, out_shape=shp, compiler_params=_cparams())(*args, *cts)


def _s5_blockdiag_in(bb):
    G, I, P = bb.shape
    nb = G // S5_GPB
    return jnp.einsum("jgip,gh->jgihp", bb.reshape(nb, S5_GPB, I, P), jnp.eye(S5_GPB, dtype=bb.dtype)).reshape(
        nb, S5_GPB * I, S5_GPB * P)


def _s5_blockdiag_in_t(d):
    nb = d.shape[0]
    d = d.reshape(nb, S5_GPB, S5_GROUP, S5_GPB, S5_STATE)
    return jnp.einsum("jgihp,gh->jgip", d, jnp.eye(S5_GPB, dtype=d.dtype)).reshape(nb * S5_GPB, S5_GROUP, S5_STATE)


def _s5_blockdiag_out(c):
    G, I, P = c.shape
    nb = G // S5_GPB
    return jnp.einsum("jgip,gh->jgphi", c.reshape(nb, S5_GPB, I, P), jnp.eye(S5_GPB, dtype=c.dtype)).reshape(
        nb, S5_GPB * P, S5_GPB * I)


def _s5_blockdiag_out_t(d):
    nb = d.shape[0]
    d = d.reshape(nb, S5_GPB, S5_STATE, S5_GPB, S5_GROUP)
    return jnp.einsum("jgphi,gh->jgip", d, jnp.eye(S5_GPB, dtype=d.dtype)).reshape(nb * S5_GPB, S5_GROUP, S5_STATE)


def _s5_powers(lr, li, n):
    out = []
    for _ in range(int(math.log2(n))):
        out.append((lr, li))
        lr, li = _cmul(lr, li, lr, li)
    return out


def _s5_local_scan(sr, si, powers, up):
    shift = _shift_up if up else _shift_down
    for k, (pr, pi) in enumerate(powers):
        d = 1 << k
        tr_, ti_ = _cmul(pr, pi, shift(sr, d), shift(si, d))
        sr, si = sr + tr_, si + ti_
    return sr, si


def _dot_hi(a, b, ca=1, cb=0):
    return lax.dot_general(a, b, (((ca,), (cb,)), ((), ())), precision=HI3, preferred_element_type=F32)


def _s5_specs(nt, rev, ublk):
    tc = S5_TCHUNK

    def at(t):
        return nt - 1 - t if rev else t

    return [
        pl.BlockSpec((tc, LANES), lambda j, b, t: (b * nt + at(t), j + ublk)),
        pl.BlockSpec((1, S5_SLANES), lambda j, b, t: (0, j)),
        pl.BlockSpec((1, S5_SLANES), lambda j, b, t: (0, j)),
        pl.BlockSpec((1, LANES, S5_SLANES), lambda j, b, t: (j, 0, 0)),
        pl.BlockSpec((1, LANES, S5_SLANES), lambda j, b, t: (j, 0, 0)),
        pl.BlockSpec((1, S5_SLANES, LANES), lambda j, b, t: (j, 0, 0)),
        pl.BlockSpec((1, S5_SLANES, LANES), lambda j, b, t: (j, 0, 0)),
        pl.BlockSpec((1, LANES), lambda j, b, t: (0, j)),
    ]


def _s5_chunk_states(u, lr, li, b_re, b_im, cr, ci, powers):
    bur, bui = _dot_hi(u, b_re), _dot_hi(u, b_im)
    row = lax.broadcasted_iota(jnp.int32, bur.shape, 0)
    inr, ini = _cmul(lr, li, cr, ci)
    bur = bur + jnp.where(row == 0, inr, 0.0)
    bui = bui + jnp.where(row == 0, ini, 0.0)
    return _s5_local_scan(bur, bui, powers, False)


def _s5_scan_fwd(u, lam_re, lam_im, Bre, Bim, Cre, Cim, dskip, B, name):
    u, ublk = u
    T = u.shape[0]
    nb = Bre.shape[0]
    Wd = nb * LANES
    tc = S5_TCHUNK
    nt = T // B // tc
    L = nb * S5_SLANES

    def body(u_ref, lr_ref, li_ref, br_ref, bi_ref, cr_ref, ci_ref, d_ref, y_ref, csr_ref, csi_ref, car_ref, cai_ref):
        @pl.when(pl.program_id(2) == 0)
        def _():
            car_ref[...] = jnp.zeros_like(car_ref)
            cai_ref[...] = jnp.zeros_like(cai_ref)

        csr_ref[0, 0] = car_ref[...]
        csi_ref[0, 0] = cai_ref[...]
        u_ = u_ref[...]
        lr, li = lr_ref[...], li_ref[...]
        sr, si = _s5_chunk_states(u_, lr, li, br_ref[0], bi_ref[0], car_ref[0:1, :], cai_ref[0:1, :],
                                  _s5_powers(lr, li, tc))
        y_ref[...] = _dot_hi(sr, cr_ref[0]) - _dot_hi(si, ci_ref[0]) + d_ref[...] * u_
        row = lax.broadcasted_iota(jnp.int32, sr.shape, 0)
        car_ref[0:1, :] = jnp.sum(jnp.where(row == tc - 1, sr, 0.0), axis=0, keepdims=True)
        cai_ref[0:1, :] = jnp.sum(jnp.where(row == tc - 1, si, 0.0), axis=0, keepdims=True)

    cs_shape = jax.ShapeDtypeStruct((B, nt, SUBLANES, L), F32)
    cs_spec = pl.BlockSpec((1, 1, SUBLANES, S5_SLANES), lambda j, b, t: (b, t, 0, j))
    return pl.pallas_call(
        body, name=name, out_shape=[jax.ShapeDtypeStruct((T, Wd), F32), cs_shape, cs_shape],
        grid=(nb, B, nt), in_specs=_s5_specs(nt, False, ublk),
        out_specs=[pl.BlockSpec((tc, LANES), lambda j, b, t: (b * nt + t, j)), cs_spec, cs_spec],
        scratch_shapes=[pltpu.VMEM((SUBLANES, S5_SLANES), F32)] * 2,
        compiler_params=_cparams(("parallel", "parallel", "arbitrary")),
    )(u, lam_re, lam_im, Bre, Bim, Cre, Cim, dskip)


def _s5_scan_bwd(u, lam_re, lam_im, Bre, Bim, Cre, Cim, dskip, csr, csi, dy, B, name):
    u, ublk = u
    T = u.shape[0]
    nb = Bre.shape[0]
    Wd = nb * LANES
    tc = S5_TCHUNK
    nt = T // B // tc
    L = nb * S5_SLANES

    def body(u_ref, lr_ref, li_ref, br_ref, bi_ref, cr_ref, ci_ref, d_ref, csr_ref, csi_ref, dy_ref,
             du_ref, dlr_ref, dli_ref, dbr_ref, dbi_ref, dcr_ref, dci_ref, dd_ref, gr_ref, gi_ref):
        first = jnp.logical_and(pl.program_id(1) == 0, pl.program_id(2) == 0)

        @pl.when(pl.program_id(2) == 0)
        def _():
            gr_ref[...] = jnp.zeros_like(gr_ref)
            gi_ref[...] = jnp.zeros_like(gi_ref)

        @pl.when(first)
        def _():
            for r in (dlr_ref, dli_ref, dbr_ref, dbi_ref, dcr_ref, dci_ref, dd_ref):
                r[...] = jnp.zeros_like(r)

        u_, dy_ = u_ref[...], dy_ref[...]
        lr, li = lr_ref[...], li_ref[...]
        powers = _s5_powers(lr, li, tc)
        c_in_r, c_in_i = csr_ref[0, 0, 0:1, :], csi_ref[0, 0, 0:1, :]
        sr, si = _s5_chunk_states(u_, lr, li, br_ref[0], bi_ref[0], c_in_r, c_in_i, powers)
        dcr_ref[0] += _dot_hi(sr, dy_, 0, 0)
        dci_ref[0] -= _dot_hi(si, dy_, 0, 0)
        dd_ref[...] += jnp.sum(dy_ * u_, axis=0, keepdims=True)
        row = lax.broadcasted_iota(jnp.int32, sr.shape, 0)
        gr = _dot_hi(dy_, cr_ref[0], 1, 1)
        gi = -_dot_hi(dy_, ci_ref[0], 1, 1)
        inr, ini = _cmul(lr, -li, gr_ref[0:1, :], gi_ref[0:1, :])
        gr = gr + jnp.where(row == tc - 1, inr, 0.0)
        gi = gi + jnp.where(row == tc - 1, ini, 0.0)
        gr, gi = _s5_local_scan(gr, gi, [(pr, -pi) for pr, pi in powers], True)
        gr_ref[0:1, :] = jnp.sum(jnp.where(row == 0, gr, 0.0), axis=0, keepdims=True)
        gi_ref[0:1, :] = jnp.sum(jnp.where(row == 0, gi, 0.0), axis=0, keepdims=True)
        pr_ = _shift_down(sr, 1) + jnp.where(row == 0, c_in_r, 0.0)
        pi_ = _shift_down(si, 1) + jnp.where(row == 0, c_in_i, 0.0)
        dlr_ref[...] += jnp.sum(gr * pr_ + gi * pi_, axis=0, keepdims=True)
        dli_ref[...] += jnp.sum(gi * pr_ - gr * pi_, axis=0, keepdims=True)
        dbr_ref[0] += _dot_hi(u_, gr, 0, 0)
        dbi_ref[0] += _dot_hi(u_, gi, 0, 0)
        du_ref[...] = dy_ * d_ref[...] + _dot_hi(gr, br_ref[0], 1, 1) + _dot_hi(gi, bi_ref[0], 1, 1)

    cs_spec = pl.BlockSpec((1, 1, SUBLANES, S5_SLANES), lambda j, b, t: (b, nt - 1 - t, 0, j))
    rrow = pl.BlockSpec((tc, LANES), lambda j, b, t: (b * nt + nt - 1 - t, j))
    lam_spec = pl.BlockSpec((1, S5_SLANES), lambda j, b, t: (0, j))
    b_spec = pl.BlockSpec((1, LANES, S5_SLANES), lambda j, b, t: (j, 0, 0))
    c_spec = pl.BlockSpec((1, S5_SLANES, LANES), lambda j, b, t: (j, 0, 0))
    return pl.pallas_call(
        body, name=name,
        out_shape=[jax.ShapeDtypeStruct((T, Wd), F32)] + [jax.ShapeDtypeStruct((1, L), F32)] * 2
        + [jax.ShapeDtypeStruct((nb, LANES, S5_SLANES), F32)] * 2
        + [jax.ShapeDtypeStruct((nb, S5_SLANES, LANES), F32)] * 2 + [jax.ShapeDtypeStruct((1, Wd), F32)],
        grid=(nb, B, nt), in_specs=_s5_specs(nt, True, ublk) + [cs_spec, cs_spec, rrow],
        out_specs=[rrow, lam_spec, lam_spec, b_spec, b_spec, c_spec, c_spec,
                   pl.BlockSpec((1, LANES), lambda j, b, t: (0, j))],
        scratch_shapes=[pltpu.VMEM((SUBLANES, S5_SLANES), F32)] * 2,
        compiler_params=_cparams(("arbitrary", "arbitrary", "arbitrary")),
    )(u, lam_re, lam_im, Bre, Bim, Cre, Cim, dskip, csr, csi, dy)


def _f_gelu(y):
    return 0.5 * y * (1.0 + jnp.tanh(math.sqrt(2.0 / math.pi) * (y + 0.044715 * (y * y * y))))


def _f_glu(pv, pg, bv, bg):
    return (pv + bv) * jax.nn.sigmoid(pg + bg)


def _s5_params(p):
    prep_in = (p["a_re"], p["a_im"], p["log_step"][:, None], jnp.swapaxes(p["b_re"], 1, 2), jnp.swapaxes(p["b_im"], 1, 2))
    return prep_in


def _s5_fwd(u, p, glu_w, B, tag):
    Wd = p["d"].shape[0]
    prep_in = _s5_params(p)
    lbr, lbi, bbr, bbi = _s5_prep_fwd(prep_in, f"{tag}_prep")
    ops = (lbr.reshape(1, -1), lbi.reshape(1, -1), _s5_blockdiag_in(bbr), _s5_blockdiag_in(bbi),
           _s5_blockdiag_out(p["c_re"]), _s5_blockdiag_out(p["c_im"]), p["d"][None])
    y, csr, csi = _s5_scan_fwd(u, *ops, B, f"{tag}_scan")
    (yg,) = _ew_fwd(_f_gelu, [y], [], [], [(Wd, BF16, False)], name=f"{tag}_gelu")
    pj = _mm(yg, glu_w, name=f"{tag}_glu")
    bv, bg = p["glu_b"][None, :Wd], p["glu_b"][None, Wd:]
    (out,) = _ew_fwd(_f_glu, [(pj, Wd, 0), (pj, Wd, 1)], [], [bv, bg], [(Wd, BF16, False)], name=f"{tag}_gate")
    return out, (u, prep_in, ops, csr, csi, y, yg, pj, bv, bg)


def _s5_bwd(dout, saved, p, glu_w, B, tag):
    u, prep_in, ops, csr, csi, y, yg, pj, bv, bg = saved
    Wd = p["d"].shape[0]
    dpv, dpg, dbv, dbg = _ew_bwd(_f_glu, [(pj, Wd, 0), (pj, Wd, 1)], [], [bv, bg], [dout], [True] * 4,
                                 row_dtypes=[BF16, BF16], name=f"{tag}_gate_bwd")
    dpj = jnp.concatenate([dpv, dpg], axis=1)
    d_glu_w = _mm(yg, dpj, ta=True, out_dtype=BF16, name=f"{tag}_glu_dw")
    dyg = _mm(dpj, glu_w, tb=True, name=f"{tag}_glu_dx")
    (dy,) = _ew_bwd(_f_gelu, [y], [], [], [dyg], [True], name=f"{tag}_gelu_bwd")
    du, dlr, dli, dBr, dBi, dCr, dCi, dd = _s5_scan_bwd(u, *ops, csr, csi, dy, B, f"{tag}_scan_bwd")
    G = p["a_re"].shape[0]
    cts = (dlr.reshape(G, S5_STATE), dli.reshape(G, S5_STATE), _s5_blockdiag_in_t(dBr), _s5_blockdiag_in_t(dBi))
    da_re, da_im, dls, db_re, db_im = _s5_prep_bwd(prep_in, cts, f"{tag}_prep_bwd")
    small = dict(a_re=da_re, a_im=da_im, log_step=dls[:, 0], b_re=jnp.swapaxes(db_re, 1, 2),
                 b_im=jnp.swapaxes(db_im, 1, 2), c_re=_s5_blockdiag_out_t(dCr), c_im=_s5_blockdiag_out_t(dCi),
                 d=dd[0], glu_b=jnp.concatenate([dbv[0], dbg[0]]))
    return du, d_glu_w, small


DIL_GW = DIL_HPG * DIL_DIM


def _f_qknorm(scale, x, w):
    n = x.shape[1]
    ri = lax.broadcasted_iota(jnp.int32, (n, n), 0)
    ci = lax.broadcasted_iota(jnp.int32, (n, n), 1)
    seg = jnp.where(ri // DIL_DIM == ci // DIL_DIM, 1.0 / DIL_DIM, 0.0)
    ms = jnp.dot(x * x, seg, precision=HI, preferred_element_type=F32)
    return x * lax.rsqrt(ms + EPS) * (w * scale)


def _dil_to_blocks(x, B):
    T = x.shape[0]
    S = T // B
    parts = []
    for gi, (_, dil) in enumerate(DIL_PAIRS):
        xg = x[:, gi * DIL_GW:(gi + 1) * DIL_GW].reshape(B, S // dil, dil, DIL_HPG, DIL_DIM)
        parts.append(xg.transpose(0, 2, 3, 1, 4).reshape(-1, DIL_DIM))
    return jnp.concatenate(parts, axis=0)


def _dil_from_blocks(y, B):
    n = y.shape[0] // len(DIL_PAIRS)
    T = n // DIL_HPG
    S = T // B
    parts = []
    for gi, (_, dil) in enumerate(DIL_PAIRS):
        yg = y[gi * n:(gi + 1) * n].reshape(B, dil, DIL_HPG, S // dil, DIL_DIM)
        parts.append(yg.transpose(0, 3, 1, 2, 4).reshape(T, DIL_GW))
    return jnp.concatenate(parts, axis=1)


DIL_BPS = 8


def _dil_block_fn(slope, has_prev, q, kp, kc, vp, vc):
    G, n, _ = q.shape
    qi = lax.broadcasted_iota(jnp.int32, (G, n, n), 1)
    kj = lax.broadcasted_iota(jnp.int32, (G, n, n), 2)
    dist = (qi - kj).astype(F32)
    sc = _bmm(q, kc, 2, 2) - slope * dist
    sp = _bmm(q, kp, 2, 2) - slope * (dist + n)
    sc = jnp.where(qi >= kj, sc, -1e30)
    sp = jnp.where(jnp.logical_and(kj >= qi, has_prev > 0.5), sp, -1e30)
    m = lax.stop_gradient(jnp.maximum(jnp.max(sc, axis=2, keepdims=True), jnp.max(sp, axis=2, keepdims=True)))
    pc = jnp.exp(sc - m)
    pp = jnp.exp(sp - m)
    l = jnp.sum(pc, axis=2, keepdims=True) + jnp.sum(pp, axis=2, keepdims=True)
    o = (_bmm(pp, vp, 2, 1) + _bmm(pc, vc, 2, 1)) / l
    return o, jnp.broadcast_to(m + jnp.log(l), o.shape)


def _dil_tables(B, S):
    nh = len(DIL_PAIRS) * DIL_HPG
    slopes, has_prev = [], []
    for gi, (_, dil) in enumerate(DIL_PAIRS):
        nbk = S // dil // DIL_BLK
        for seq in range(B * dil * DIL_HPG):
            head = gi * DIL_HPG + seq % DIL_HPG
            for n in range(nbk):
                slopes.append(dil * 2.0 ** (-ALIBI_MAX * (head + 1) / nh))
                has_prev.append(1.0 if n > 0 else 0.0)
    shape = (len(slopes), 1, 1)
    return jnp.asarray(np.array(slopes, np.float32).reshape(shape)), jnp.asarray(np.array(has_prev, np.float32).reshape(shape))


def _dil_blocks3(t):
    return t.reshape(-1, DIL_BLK, DIL_DIM)


def _dil_prev(t3):
    return jnp.concatenate([jnp.zeros_like(t3[:1]), t3[:-1]], axis=0)


def _dil_attn_fwd(qb, kb, vb, B, S, name):
    q3, k3, v3 = _dil_blocks3(qb), _dil_blocks3(kb), _dil_blocks3(vb)
    nbt = q3.shape[0]
    slope, has_prev = _dil_tables(B, S)

    def body(s_ref, h_ref, q_ref, kp_ref, kc_ref, vp_ref, vc_ref, o_ref, l_ref):
        o, l = _dil_block_fn(s_ref[...], h_ref[...], q_ref[...], kp_ref[...], kc_ref[...], vp_ref[...], vc_ref[...])
        o_ref[...] = o
        l_ref[...] = l

    blk = pl.BlockSpec((DIL_BPS, DIL_BLK, DIL_DIM), lambda m: (m, 0, 0))
    tab = pl.BlockSpec((DIL_BPS, 1, 1), lambda m: (m, 0, 0))
    o, l = pl.pallas_call(
        body, name=name, out_shape=[jax.ShapeDtypeStruct(q3.shape, F32)] * 2, grid=(nbt // DIL_BPS,),
        in_specs=[tab, tab] + [blk] * 5, out_specs=[blk, blk], compiler_params=_cparams(("parallel",)),
    )(slope, has_prev, q3, _dil_prev(k3), k3, _dil_prev(v3), v3)
    return o.reshape(qb.shape), l.reshape(qb.shape)


def _dil_attn_bwd(qb, kb, vb, do, dl, B, S, name):
    q3, k3, v3 = _dil_blocks3(qb), _dil_blocks3(kb), _dil_blocks3(vb)
    nbt = q3.shape[0]
    slope, has_prev = _dil_tables(B, S)

    def body(s_ref, h_ref, q_ref, kp_ref, kc_ref, vp_ref, vc_ref, do_ref, dl_ref, *outs):
        _, vjp = jax.vjp(functools.partial(_dil_block_fn, s_ref[...], h_ref[...]),
                         q_ref[...], kp_ref[...], kc_ref[...], vp_ref[...], vc_ref[...])
        for o_ref, g in zip(outs, vjp((do_ref[...], dl_ref[...]))):
            o_ref[...] = g

    blk = pl.BlockSpec((DIL_BPS, DIL_BLK, DIL_DIM), lambda m: (m, 0, 0))
    tab = pl.BlockSpec((DIL_BPS, 1, 1), lambda m: (m, 0, 0))
    outs = pl.pallas_call(
        body, name=name, out_shape=[jax.ShapeDtypeStruct(q3.shape, F32)] * 5, grid=(nbt // DIL_BPS,),
        in_specs=[tab, tab] + [blk] * 7, out_specs=[blk] * 5, compiler_params=_cparams(("parallel",)),
    )(slope, has_prev, q3, _dil_prev(k3), k3, _dil_prev(v3), v3, _dil_blocks3(do), _dil_blocks3(dl))
    return [t.reshape(qb.shape) for t in outs]


def _f_dil_merge(o0, o1, o2, l0, l1, l2):
    m = lax.stop_gradient(jnp.maximum(jnp.maximum(l0, l1), l2))
    e0, e1, e2 = jnp.exp(l0 - m), jnp.exp(l1 - m), jnp.exp(l2 - m)
    return (e0 * o0 + e1 * o1 + e2 * o2) / (e0 + e1 + e2)


def _dil_fwd(qkv, q_norm, k_norm, B, tag):
    qkv, fb = qkv
    T = qkv.shape[0]
    S = T // B
    Wd = len(DIL_PAIRS) * DIL_GW
    nblk = Wd // LANES
    (qn,) = _ew_fwd(functools.partial(_f_qknorm, DIL_DIM ** -0.5), [(qkv, LANES, fb)], [], [q_norm],
                    [(Wd, F32, True)], ncb=nblk, name=f"{tag}_qnorm")
    (kn,) = _ew_fwd(functools.partial(_f_qknorm, 1.0), [(qkv, LANES, fb + nblk)], [], [k_norm],
                    [(Wd, F32, True)], ncb=nblk, name=f"{tag}_knorm")
    v0 = (fb + 2 * nblk) * LANES
    qb, kb, vb = _dil_to_blocks(qn, B), _dil_to_blocks(kn, B), _dil_to_blocks(qkv[:, v0:v0 + Wd], B)
    ob, lb = _dil_attn_fwd(qb, kb, vb, B, S, f"{tag}_attn")
    o, l = _dil_from_blocks(ob, B), _dil_from_blocks(lb, B)
    gw = DIL_GW
    rows = [(o, gw, 0), (o, gw, 1), (o, gw, 2), (l, gw, 0), (l, gw, 1), (l, gw, 2)]
    (y,) = _ew_fwd(_f_dil_merge, rows, [], [], [(gw, BF16, False)], name=f"{tag}_merge")
    return y, (qkv, fb, qb, kb, vb, o, l)


def _dil_bwd(dy, saved, q_norm, k_norm, B, tag):
    qkv, fb, qb, kb, vb, o, l = saved
    T = qkv.shape[0]
    S = T // B
    Wd = len(DIL_PAIRS) * DIL_GW
    nblk = Wd // LANES
    gw = DIL_GW
    rows = [(o, gw, 0), (o, gw, 1), (o, gw, 2), (l, gw, 0), (l, gw, 1), (l, gw, 2)]
    g = _ew_bwd(_f_dil_merge, rows, [], [], [dy], [True] * 6, name=f"{tag}_merge_bwd")
    do = _dil_to_blocks(jnp.concatenate(g[:3], axis=1), B)
    dl = _dil_to_blocks(jnp.concatenate(g[3:], axis=1), B)
    dq, dkp, dkc, dvp, dvc = _dil_attn_bwd(qb, kb, vb, do, dl, B, S, f"{tag}_attn_bwd")
    nxt = lambda t: jnp.concatenate([t[DIL_BLK:], jnp.zeros((DIL_BLK, DIL_DIM), t.dtype)], axis=0)
    dqn = _dil_from_blocks(dq, B)
    dkn = _dil_from_blocks(dkc + nxt(dkp), B)
    dv = _dil_from_blocks(dvc + nxt(dvp), B)
    dq_raw, dqw = _ew_bwd(functools.partial(_f_qknorm, DIL_DIM ** -0.5), [(qkv, LANES, fb)], [], [q_norm],
                          [(dqn, LANES, 0)], [True, True], ncb=nblk, name=f"{tag}_qnorm_bwd")
    dk_raw, dkw = _ew_bwd(functools.partial(_f_qknorm, 1.0), [(qkv, LANES, fb + nblk)], [], [k_norm],
                          [(dkn, LANES, 0)], [True, True], ncb=nblk, name=f"{tag}_knorm_bwd")
    return jnp.concatenate([dq_raw, dk_raw, dv], axis=1), dqw, dkw


MESH_ID = pl.DeviceIdType.MESH
ANY = pl.BlockSpec(memory_space=pl.ANY)


def _my_place():
    return lax.axis_index("x"), lax.axis_index("y"), lax.axis_index("c")


def _flat_index(px, py, pc):
    return 4 * px + 2 * py + pc


def _all_gather(x, name):
    R_, C = x.shape

    def body(x_ref, out_ref, send_sems, recv_sems, local_sem):
        mx, my, mc = _my_place()
        me, sibling = (mx, my, mc), (mx, my, 1 - mc)
        chips = [(1 - mx, my), (mx, 1 - my), (1 - mx, 1 - my)]

        def rows(p):
            return out_ref.at[_flat_index(*p)]

        def copy(k, block, to, src=None):
            return pltpu.make_async_remote_copy(
                src_ref=rows(block) if src is None else src, dst_ref=rows(block),
                send_sem=send_sems.at[k], recv_sem=recv_sems.at[k], device_id=to, device_id_type=MESH_ID)

        mine = pltpu.make_async_copy(x_ref, rows(me), local_sem)
        mine.start()
        first = [copy(0, me, sibling, src=x_ref)]
        first += [copy(1 + j, me, (*chip, mc), src=x_ref) for j, chip in enumerate(chips)]
        for cp in first:
            cp.start()
        passed = [copy(4 + j, (*chip, mc), sibling) for j, chip in enumerate(chips)]
        for j, chip in enumerate(chips):
            copy(1 + j, (*chip, mc), me).wait_recv()
            passed[j].start()
        copy(0, sibling, me).wait_recv()
        for j, chip in enumerate(chips):
            copy(4 + j, (*chip, 1 - mc), me).wait_recv()
        for cp in first + passed:
            cp.wait_send()
        mine.wait()

    return pl.pallas_call(
        body, name=name, out_shape=jax.ShapeDtypeStruct((N_DEV, R_, C), x.dtype),
        in_specs=[ANY], out_specs=ANY,
        scratch_shapes=[pltpu.SemaphoreType.DMA((7,)), pltpu.SemaphoreType.DMA((7,)), pltpu.SemaphoreType.DMA],
        compiler_params=pltpu.CompilerParams(has_side_effects=True),
    )(x)


def _exchange_sibling(x, name):
    nchip, _, R_, C = x.shape

    def body(x_ref, out_ref, send_sems, recv_sems):
        mx, my, mc = _my_place()
        sibling = (mx, my, 1 - mc)
        copies = [pltpu.make_async_remote_copy(
            src_ref=x_ref.at[k, 1 - mc], dst_ref=out_ref.at[k], send_sem=send_sems.at[k], recv_sem=recv_sems.at[k],
            device_id=sibling, device_id_type=MESH_ID) for k in range(nchip)]
        for cp in copies:
            cp.start()
        for cp in copies:
            cp.wait_recv()
        for cp in copies:
            cp.wait_send()

    return pl.pallas_call(
        body, name=name, out_shape=jax.ShapeDtypeStruct((nchip, R_, C), x.dtype), in_specs=[ANY], out_specs=ANY,
        scratch_shapes=[pltpu.SemaphoreType.DMA((nchip,)), pltpu.SemaphoreType.DMA((nchip,))],
        compiler_params=pltpu.CompilerParams(has_side_effects=True),
    )(x)


def _exchange_chips(x, name):
    nchip, R_, C = x.shape

    def body(x_ref, out_ref, send_sems, recv_sems, local_sem):
        mx, my, mc = _my_place()
        mk = 2 * mx + my
        chips = [(1 - mx, my), (mx, 1 - my), (1 - mx, 1 - my)]
        mine = pltpu.make_async_copy(x_ref.at[mk], out_ref.at[mk], local_sem)
        mine.start()
        copies = [pltpu.make_async_remote_copy(
            src_ref=x_ref.at[2 * px + py], dst_ref=out_ref.at[mk], send_sem=send_sems.at[j], recv_sem=recv_sems.at[j],
            device_id=(px, py, mc), device_id_type=MESH_ID) for j, (px, py) in enumerate(chips)]
        for cp in copies:
            cp.start()
        for j, (px, py) in enumerate(chips):
            pltpu.make_async_remote_copy(
                src_ref=x_ref.at[mk], dst_ref=out_ref.at[2 * px + py], send_sem=send_sems.at[j],
                recv_sem=recv_sems.at[j], device_id=(px, py, mc), device_id_type=MESH_ID).wait_recv()
        for cp in copies:
            cp.wait_send()
        mine.wait()

    return pl.pallas_call(
        body, name=name, out_shape=jax.ShapeDtypeStruct(x.shape, x.dtype), in_specs=[ANY], out_specs=ANY,
        scratch_shapes=[pltpu.SemaphoreType.DMA((3,)), pltpu.SemaphoreType.DMA((3,)), pltpu.SemaphoreType.DMA],
        compiler_params=pltpu.CompilerParams(has_side_effects=True),
    )(x)


BLOCK_BYTES = 2 * 1024 * 1024


def _rows_for(R_, row_bytes):
    for t in (2048, 1024, 512, 256, 128, 64, 32, 16, 8):
        if R_ % t == 0 and t * row_bytes <= BLOCK_BYTES:
            return t
    return R_


def _add_pieces(a, b, name):
    n, R_, C = a.shape
    tr = _rows_for(R_, C * a.dtype.itemsize)

    def body(a_ref, b_ref, o_ref):
        o_ref[...] = (a_ref[...].astype(F32) + b_ref[...].astype(F32)).astype(o_ref.dtype)

    spec = pl.BlockSpec((1, tr, C), lambda k, i: (k, i, 0))
    return pl.pallas_call(
        body, name=name, out_shape=jax.ShapeDtypeStruct(a.shape, a.dtype), grid=(n, R_ // tr),
        in_specs=[spec, spec], out_specs=spec, compiler_params=_cparams(("parallel", "parallel")),
    )(a, b)


def _reduce_to_owner(buf, mc, tag):
    _, R_, C = buf.shape
    buf4 = buf.reshape(N_DEV // 2, 2, R_, C)
    from_sibling = _exchange_sibling(buf4, f"scatter_sib_{tag}")
    mine = lax.dynamic_index_in_dim(buf4, mc, axis=1, keepdims=False)
    chip_sums = _add_pieces(mine, from_sibling, f"add_sib_{tag}")
    return _sum0(_exchange_chips(chip_sums, f"scatter_chips_{tag}"), f"sum_grads_{tag}")


def _sum0(x, name):
    n, R_, C = x.shape
    tr = _rows_for(R_, n * C * x.dtype.itemsize)

    def body(x_ref, o_ref):
        acc = x_ref[0].astype(F32)
        for k in range(1, n):
            acc = acc + x_ref[k].astype(F32)
        o_ref[...] = acc

    return pl.pallas_call(
        body, name=name, out_shape=jax.ShapeDtypeStruct((R_, C), F32), grid=(R_ // tr,),
        in_specs=[pl.BlockSpec((n, tr, C), lambda i: (0, i, 0))], out_specs=pl.BlockSpec((tr, C), lambda i: (i, 0)),
        compiler_params=_cparams(("parallel",)),
    )(x)


PACK_ROWS = 256


def _pack(arrs, dtype, width):
    flat = jnp.concatenate([a.astype(dtype).reshape(-1) for a in arrs])
    quantum = width * PACK_ROWS
    pad = (-flat.shape[0]) % quantum
    if pad:
        flat = jnp.concatenate([flat, jnp.zeros((pad,), dtype)])
    return flat.reshape(-1, width)


def _unpack(flat, shapes):
    out, off = [], 0
    for s in shapes:
        n = int(np.prod(s))
        out.append(flat[..., off:off + n].reshape(flat.shape[:-1] + tuple(s)))
        off += n
    return out


def _ada_fwd(c_all, ada_w, bias, name):
    M, D = c_all.shape
    n = ada_w.shape[1]
    tn = _pick(n, (768, 512, 256, 128))

    def body(c_ref, w_ref, b_ref, o_ref):
        c_ = c_ref[...]
        a = (c_ * jax.nn.sigmoid(c_)).astype(BF16)
        o_ref[...] = jnp.dot(a, w_ref[...].astype(BF16), preferred_element_type=F32) + b_ref[...]

    return pl.pallas_call(
        body, name=name, out_shape=jax.ShapeDtypeStruct((M, n), F32), grid=(n // tn,),
        in_specs=[pl.BlockSpec((M, D), lambda j: (0, 0)), pl.BlockSpec((D, tn), lambda j: (0, j)),
                  pl.BlockSpec((1, tn), lambda j: (0, j))],
        out_specs=pl.BlockSpec((M, tn), lambda j: (0, j)), compiler_params=_cparams(("parallel",)),
    )(c_all, ada_w, bias)


def _ada_bwd(c_all, dmod, name):
    M, D = c_all.shape
    n = dmod.shape[1]
    tn = _pick(n, (768, 512, 256, 128))

    def body(c_ref, d_ref, o_ref):
        c_ = c_ref[...]
        a = (c_ * jax.nn.sigmoid(c_)).astype(BF16)
        o_ref[...] = lax.dot_general(a, d_ref[...].astype(BF16), (((0,), (0,)), ((), ())), preferred_element_type=F32)

    return pl.pallas_call(
        body, name=name, out_shape=jax.ShapeDtypeStruct((D, n), F32), grid=(n // tn,),
        in_specs=[pl.BlockSpec((M, D), lambda j: (0, 0)), pl.BlockSpec((M, tn), lambda j: (0, j))],
        out_specs=pl.BlockSpec((D, tn), lambda j: (0, j)), compiler_params=_cparams(("parallel",)),
    )(c_all, dmod)


def _loss_head(y, target, name, tr=256):
    T, D = y.shape

    def body(y_ref, t_ref, l_ref, d_ref):
        e = y_ref[...] - t_ref[...]
        d_ref[...] = e * (1.0 / D)
        part = jnp.sum(jnp.sum(e * e, axis=1, keepdims=True), axis=0, keepdims=True) * (0.5 / D)

        @pl.when(pl.program_id(0) == 0)
        def _():
            l_ref[...] = jnp.zeros_like(l_ref)

        l_ref[...] += jnp.broadcast_to(part, l_ref.shape)

    row = pl.BlockSpec((tr, D), lambda i: (i, 0))
    return pl.pallas_call(
        body, name=name, out_shape=[jax.ShapeDtypeStruct((1, LANES), F32), jax.ShapeDtypeStruct((T, D), F32)],
        grid=(T // tr,), in_specs=[row, row], out_specs=[pl.BlockSpec((1, LANES), lambda i: (0, 0)), row],
        compiler_params=_cparams(("arbitrary",)),
    )(y, target)


def _adamw(w, g, m, v, name):
    shape = w.shape
    C = shape[-1]
    R_ = int(np.prod(shape[:-1]))
    w2, g2, m2, v2 = [a.reshape(R_, C) for a in (w, g, m, v)]
    tr = _pick(R_, (256, 128, 64, 32, 16, 8)) if R_ > 8 else R_
    c1 = 1.0 / (1.0 - ADAM_B1 ** ADAM_STEP)
    c2 = 1.0 / (1.0 - ADAM_B2 ** ADAM_STEP)

    def body(w_ref, g_ref, m_ref, v_ref, d_ref, nm_ref, nv_ref):
        g_ = g_ref[...]
        nm = ADAM_B1 * m_ref[...] + (1.0 - ADAM_B1) * g_
        nv = ADAM_B2 * v_ref[...] + (1.0 - ADAM_B2) * (g_ * g_)
        d_ref[...] = -ADAM_LR * ((nm * c1) / (jnp.sqrt(nv * c2) + ADAM_EPS) + ADAM_WD * w_ref[...])
        nm_ref[...] = nm
        nv_ref[...] = nv

    spec = pl.BlockSpec((tr, C), lambda i: (i, 0))
    outs = pl.pallas_call(
        body, name=name, out_shape=[jax.ShapeDtypeStruct((R_, C), F32)] * 3, grid=(R_ // tr,),
        in_specs=[spec] * 4, out_specs=[spec] * 3, compiler_params=_cparams(("parallel",)),
    )(w2, g2, m2, v2)
    return [o.reshape(shape) for o in outs]


GDN_W = GDN_HEADS * GDN_DIM
IN_SPLITS = (3 * GDN_W, GDN_W, GDN_HEADS, GDN_HEADS, 768, 3 * 768, None)
IN_PAD_GATES = LANES - 2 * GDN_HEADS


def _f_merge(pa, pb, pc, ga, gb, gc):
    return jax.nn.sigmoid(ga) * pa + jax.nn.sigmoid(gb) * pb + jax.nn.sigmoid(gc) * pc


def _f_id_rmsmod(x, sh, sc, g):
    return x, _rms(x, g) * (1.0 + sc) + sh


def _in_layout(D):
    widths = [3 * D, 3 * GDN_W, GDN_W, LANES, 768, 3 * 768]
    offs = np.concatenate([[0], np.cumsum(widths)]).tolist()
    total = -(-offs[-1] // 768) * 768
    return offs, total


def _pad_w_in(w_in):
    D = w_in.shape[0]
    offs, total = _in_layout(D)
    cut = 4 * GDN_W + 2 * GDN_HEADS
    ng = w_in.shape[1] - 3 * D
    return jnp.concatenate([w_in[:, ng:], w_in[:, :cut], jnp.zeros((D, IN_PAD_GATES), w_in.dtype), w_in[:, cut:ng],
                            jnp.zeros((D, total - offs[-1]), w_in.dtype)], axis=1)


def _unpad_w_in(d):
    D = d.shape[0]
    offs, _ = _in_layout(D)
    cut = 4 * GDN_W + 2 * GDN_HEADS
    g0 = 3 * D
    return jnp.concatenate([d[:, g0:g0 + cut], d[:, g0 + cut + IN_PAD_GATES:offs[-1]], d[:, :g0]], axis=1)


def _layer_fwd(x0, mod, W, sm, B, tag):
    T, D = x0.shape
    S = T // B
    sh1, sc1, g1, sh2, sc2, g2, sh3, sc3, g3 = mod
    n1, nm, n3 = sm["norm_ffn1"][None], sm["norm_mix"][None], sm["norm_ffn2"][None]
    (h1,) = _ew_fwd(_f_rmsmod, [x0], [sh1, sc1], [n1], [(D, BF16, False)], seq=S, name=f"{tag}_norm1")
    y1, sv1 = _ffn_fwd(h1, W["ffn1_w1"], W["ffn1_w3"], W["ffn1_w2"], f"{tag}_ffn1")
    x1, h2 = _ew_fwd(functools.partial(_f_res_rmsmod, FFN_RES), [x0, y1], [g1, sh2, sc2], [nm],
                     [(D, F32, False), (D, BF16, False)], seq=S, name=f"{tag}_res1")
    P = _mm(h2, W["w_in"], name=f"{tag}_in")
    offs, _ = _in_layout(D)
    qkv_a, z, ba, u, qkv_c = [(P, off // LANES) for off in offs[1:6]]
    lane8 = lambda v: jnp.pad(v, (GDN_HEADS, LANES - 2 * GDN_HEADS))[None]
    gate_params = (lane8(sm["gdn_a_log"]), lane8(sm["gdn_dt_bias"]))
    out_norm = sm["gdn_out_norm"][None]
    y_a, sva = _gdn_fwd(qkv_a, z, ba, sm["gdn_conv"], gate_params, out_norm, B, f"{tag}_gdn")
    s5p = {k[3:]: v for k, v in sm.items() if k.startswith("s5_")}
    y_b, svb = _s5_fwd(u, s5p, W["s5_glu_w"], B, f"{tag}_s5")
    qn2, kn2 = jnp.tile(sm["dil_q_norm"], 2)[None], jnp.tile(sm["dil_k_norm"], 2)[None]
    y_c, svc = _dil_fwd(qkv_c, qn2, kn2, B, f"{tag}_dil")
    pa = _mm(y_a, W["w_branch_a"], name=f"{tag}_pa")
    pb = _mm(y_b, W["w_branch_b"], name=f"{tag}_pb")
    pc = _mm(y_c, W["w_branch_c"], name=f"{tag}_pc")
    mrows = [pa, pb, pc, (P, D, 0), (P, D, 1), (P, D, 2)]
    (merged,) = _ew_fwd(_f_merge, mrows, [], [], [(D, BF16, False)], tr=128, name=f"{tag}_merge")
    mo = _mm(merged, W["w_out"], name=f"{tag}_out")
    x2, h3 = _ew_fwd(functools.partial(_f_res_rmsmod, 1.0), [x1, mo], [g2, sh3, sc3], [n3],
                     [(D, F32, False), (D, BF16, False)], seq=S, name=f"{tag}_res2")
    y3, sv3 = _ffn_fwd(h3, W["ffn2_w1"], W["ffn2_w3"], W["ffn2_w2"], f"{tag}_ffn2")
    (x3,) = _ew_fwd(functools.partial(_f_res, FFN_RES), [x2, y3], [g3], [], [(D, F32, False)], seq=S,
                    name=f"{tag}_res3")
    saved = dict(x0=x0, x1=x1, x2=x2, y1=y1, y3=y3, mo=mo, h2=h2, sv1=sv1, sv3=sv3, sva=sva, svb=svb, svc=svc,
                 y_a=y_a, y_b=y_b, y_c=y_c, mrows=mrows, merged=merged, gate_params=gate_params, out_norm=out_norm,
                 s5p=s5p, qn2=qn2, kn2=kn2)
    return x3, saved


def _layer_bwd(dx3, sv, mod, W, sm, B, tag):
    T, D = dx3.shape
    S = T // B
    sh1, sc1, g1, sh2, sc2, g2, sh3, sc3, g3 = mod
    n1, nm, n3 = sm["norm_ffn1"][None], sm["norm_mix"][None], sm["norm_ffn2"][None]
    big, small = {}, {}
    dx2, dy3, dg3 = _ew_bwd(functools.partial(_f_res, FFN_RES), [sv["x2"], sv["y3"]], [g3], [], [dx3], [True] * 3,
                            seq=S, row_dtypes=[F32, BF16], name=f"{tag}_res3_bwd")
    dh3, big["ffn2_w1"], big["ffn2_w3"], big["ffn2_w2"] = _ffn_bwd(
        dy3, sv["sv3"], W["ffn2_w1"], W["ffn2_w3"], W["ffn2_w2"], f"{tag}_ffn2")
    dx1, dmo, dg2, dsh3, dsc3, dn3 = _ew_bwd(
        functools.partial(_f_res_rmsmod, 1.0), [sv["x1"], sv["mo"]], [g2, sh3, sc3], [n3], [dx2, dh3], [True] * 6,
        seq=S, row_dtypes=[F32, BF16], name=f"{tag}_res2_bwd")
    big["w_out"] = _mm(sv["merged"], dmo, ta=True, out_dtype=BF16, name=f"{tag}_out_dw")
    dmerged = _mm(dmo, W["w_out"], tb=True, name=f"{tag}_out_dx")
    dpa, dpb, dpc, dga, dgb, dgc = _ew_bwd(_f_merge, sv["mrows"], [], [], [dmerged], [True] * 6, tr=128,
                                           row_dtypes=[BF16] * 6, name=f"{tag}_merge_bwd")
    big["w_branch_a"] = _mm(sv["y_a"], dpa, ta=True, out_dtype=BF16, name=f"{tag}_pa_dw")
    big["w_branch_b"] = _mm(sv["y_b"], dpb, ta=True, out_dtype=BF16, name=f"{tag}_pb_dw")
    big["w_branch_c"] = _mm(sv["y_c"], dpc, ta=True, out_dtype=BF16, name=f"{tag}_pc_dw")
    dy_a = _mm(dpa, W["w_branch_a"], tb=True, name=f"{tag}_pa_dx")
    dy_b = _mm(dpb, W["w_branch_b"], tb=True, name=f"{tag}_pb_dx")
    dy_c = _mm(dpc, W["w_branch_c"], tb=True, name=f"{tag}_pc_dx")
    dqkv_a, dz, dba, gdn_small = _gdn_bwd(dy_a, sv["sva"], sm["gdn_conv"], sv["gate_params"], sv["out_norm"], B,
                                          f"{tag}_gdn")
    du, big["s5_glu_w"], s5_small = _s5_bwd(dy_b, sv["svb"], sv["s5p"], W["s5_glu_w"], B, f"{tag}_s5")
    dqkv_c, dqw, dkw = _dil_bwd(dy_c, sv["svc"], sv["qn2"], sv["kn2"], B, f"{tag}_dil")
    offs, total = _in_layout(D)
    dP = jnp.concatenate([t.astype(BF16) for t in (dga, dgb, dgc, dqkv_a, dz, dba, du, dqkv_c)]
                         + [jnp.zeros((T, total - offs[-1]), BF16)], axis=1)
    big["w_in"] = _mm(sv["h2"], dP, ta=True, out_dtype=BF16, name=f"{tag}_in_dw")
    dh2 = _mm(dP, W["w_in"], tb=True, name=f"{tag}_in_dx")
    dx0a, dy1, dg1, dsh2, dsc2, dnm = _ew_bwd(
        functools.partial(_f_res_rmsmod, FFN_RES), [sv["x0"], sv["y1"]], [g1, sh2, sc2], [nm], [dx1, dh2], [True] * 6,
        seq=S, row_dtypes=[F32, BF16], name=f"{tag}_res1_bwd")
    dh1, big["ffn1_w1"], big["ffn1_w3"], big["ffn1_w2"] = _ffn_bwd(
        dy1, sv["sv1"], W["ffn1_w1"], W["ffn1_w3"], W["ffn1_w2"], f"{tag}_ffn1")
    dx0, dsh1, dsc1, dn1 = _ew_bwd(_f_id_rmsmod, [sv["x0"]], [sh1, sc1], [n1], [dx0a, dh1], [True] * 4, seq=S,
                                   name=f"{tag}_norm1_bwd")
    half = DIL_DIM
    small.update(norm_ffn1=dn1[0], norm_mix=dnm[0], norm_ffn2=dn3[0], gdn_conv=gdn_small["conv"],
                 gdn_a_log=gdn_small["a_log"][0, GDN_HEADS:2 * GDN_HEADS],
                 gdn_dt_bias=gdn_small["dt_bias"][0, GDN_HEADS:2 * GDN_HEADS], gdn_out_norm=gdn_small["out_norm"][0],
                 dil_q_norm=dqw[0, :half] + dqw[0, half:], dil_k_norm=dkw[0, :half] + dkw[0, half:])
    small.update({"s5_" + k: v for k, v in s5_small.items()})
    dmod = [dsh1, dsc1, dg1, dsh2, dsc2, dg2, dsh3, dsc3, dg3]
    return dx0, big, small, dmod


WEIGHTS = ['ada_w', 'ada_b', 'norm_ffn1', 'ffn1_w1', 'ffn1_w3', 'ffn1_w2', 'norm_mix', 'w_in', 'gdn_conv', 'gdn_a_log',
           'gdn_dt_bias', 'gdn_out_norm', 's5_a_re', 's5_a_im', 's5_b_re', 's5_b_im', 's5_c_re', 's5_c_im', 's5_d',
           's5_log_step', 's5_glu_w', 's5_glu_b', 'dil_q_norm', 'dil_k_norm', 'w_branch_a', 'w_branch_b', 'w_branch_c',
           'w_out', 'norm_ffn2', 'ffn2_w1', 'ffn2_w3', 'ffn2_w2']
BIG = dict(ffn1_w1=True, ffn1_w3=True, ffn1_w2=False, w_in=True, s5_glu_w=True, w_branch_a=True, w_branch_b=True,
           w_branch_c=True, w_out=False, ffn2_w1=True, ffn2_w3=True, ffn2_w2=False)
FFN_W = ("ffn1_w1", "ffn1_w3", "ffn1_w2", "ffn2_w1", "ffn2_w3", "ffn2_w2")
SMALL = ['norm_ffn1', 'norm_mix', 'norm_ffn2', 'gdn_conv', 'gdn_a_log', 'gdn_dt_bias', 'gdn_out_norm', 's5_a_re',
         's5_a_im', 's5_b_re', 's5_b_im', 's5_c_re', 's5_c_im', 's5_d', 's5_log_step', 's5_glu_b', 'dil_q_norm',
         'dil_k_norm']


def _full_from_shards(g, cols):
    n, r, c = g.shape
    return jnp.transpose(g, (1, 0, 2)).reshape(r, n * c) if cols else g.reshape(n * r, c)


def _shards_from_full(w, cols):
    if cols:
        r, nc = w.shape
        return jnp.transpose(w.reshape(r, N_DEV, nc // N_DEV), (1, 0, 2))
    nr, c = w.shape
    return w.reshape(N_DEV, nr // N_DEV, c)


def kernel(x, c, ada_w, ada_b, norm_ffn1, ffn1_w1, ffn1_w3, ffn1_w2, norm_mix, w_in, gdn_conv, gdn_a_log, gdn_dt_bias, gdn_out_norm, s5_a_re, s5_a_im, s5_b_re, s5_b_im, s5_c_re, s5_c_im, s5_d, s5_log_step, s5_glu_w, s5_glu_b, dil_q_norm, dil_k_norm, w_branch_a, w_branch_b, w_branch_c, w_out, norm_ffn2, ffn2_w1, ffn2_w3, ffn2_w2, loss_target, m_ada_w, m_ada_b, m_norm_ffn1, m_ffn1_w1, m_ffn1_w3, m_ffn1_w2, m_norm_mix, m_w_in, m_gdn_conv, m_gdn_a_log, m_gdn_dt_bias, m_gdn_out_norm, m_s5_a_re, m_s5_a_im, m_s5_b_re, m_s5_b_im, m_s5_c_re, m_s5_c_im, m_s5_d, m_s5_log_step, m_s5_glu_w, m_s5_glu_b, m_dil_q_norm, m_dil_k_norm, m_w_branch_a, m_w_branch_b, m_w_branch_c, m_w_out, m_norm_ffn2, m_ffn2_w1, m_ffn2_w3, m_ffn2_w2, v_ada_w, v_ada_b, v_norm_ffn1, v_ffn1_w1, v_ffn1_w3, v_ffn1_w2, v_norm_mix, v_w_in, v_gdn_conv, v_gdn_a_log, v_gdn_dt_bias, v_gdn_out_norm, v_s5_a_re, v_s5_a_im, v_s5_b_re, v_s5_b_im, v_s5_c_re, v_s5_c_im, v_s5_d, v_s5_log_step, v_s5_glu_w, v_s5_glu_b, v_dil_q_norm, v_dil_k_norm, v_w_branch_a, v_w_branch_b, v_w_branch_c, v_w_out, v_norm_ffn2, v_ffn2_w1, v_ffn2_w3, v_ffn2_w2):
    env = dict(locals())
    w = {n: env[n] for n in WEIGHTS}
    m = {n: env["m_" + n] for n in WEIGHTS}
    v = {n: env["v_" + n] for n in WEIGHTS}
    L = ada_w.shape[0]
    B, S, D = x.shape
    T = B * S
    me = _flat_index(*_my_place())

    big_keys = [(n, l) for l in range(L) for n in BIG]
    groups = {}
    for n, l in big_keys:
        r, cc = w[n].shape[1:]
        groups.setdefault((BIG[n], r if BIG[n] else cc), []).append((n, l))
    shards = {}
    for (cols, dim), keys in groups.items():
        buf = jnp.concatenate([w[n][l].astype(BF16) for n, l in keys], axis=1 if cols else 0)
        got = _all_gather(buf, f"gather_weights_{'c' if cols else 'r'}{dim}")
        off = 0
        for n, l in keys:
            k = w[n].shape[2] if cols else w[n].shape[1]
            shards[(n, l)] = got[:, :, off:off + k] if cols else got[:, off:off + k, :]
            off += k
    small_in = _pack([jnp.pad(c, ((0, SUBLANES - B), (0, 0))), gdn_conv], F32, LANES)
    c_g, conv_g = _unpack(_all_gather(small_in, "gather_cond").reshape(N_DEV, -1),
                          [(SUBLANES, D), gdn_conv.shape])
    c_all = c_g[:, :B].reshape(N_DEV * B, D)
    conv_full = jnp.transpose(conv_g, (1, 2, 0, 3)).reshape(L, GDN_CONV, -1)
    Ws = []
    for l in range(L):
        Wl = {n: shards[(n, l)] for n in FFN_W}
        Wl.update({n: _full_from_shards(shards[(n, l)], BIG[n])
                   for n in ("s5_glu_w", "w_branch_a", "w_branch_b", "w_branch_c", "w_out")})
        Wl["w_in"] = _pad_w_in(_full_from_shards(shards[("w_in", l)], True))
        Ws.append(Wl)
    sms = [dict({n: w[n][l] for n in SMALL}, gdn_conv=conv_full[l]) for l in range(L)]

    n_ada = ada_w.shape[2]
    bias = lax.dynamic_slice(ada_b, (0, me * n_ada), (L, n_ada))
    mod_cols = jnp.concatenate([_ada_fwd(c_all, ada_w[l], bias[l][None], f"ada{l}") for l in range(L)], axis=0)
    mod_g = _all_gather(mod_cols, "gather_mod").reshape(N_DEV, L, N_DEV * B, n_ada)
    mod_mine = lax.dynamic_slice(mod_g, (0, 0, me * B, 0), (N_DEV, L, B, n_ada))
    mod_mine = jnp.transpose(mod_mine, (1, 2, 0, 3)).reshape(L, B, N_DEV * n_ada)
    mods = [[mod_mine[l][:, None, k * D:(k + 1) * D] for k in range(9)] for l in range(L)]

    h = x.reshape(T, D)
    saved = []
    for l in range(L):
        h, sv = _layer_fwd(h, mods[l], Ws[l], sms[l], B, f"l{l}")
        saved.append(sv)
    loss_row, dh = _loss_head(h, loss_target.reshape(T, D), "loss")
    loss = lax.psum(loss_row[0, 0], ("x", "y", "c"))
    bigs, smalls, dmods = [None] * L, [None] * L, [None] * L
    for l in reversed(range(L)):
        dh, bigs[l], smalls[l], dmods[l] = _layer_bwd(dh, saved[l], mods[l], Ws[l], sms[l], B, f"l{l}")
    grad_x = dh.reshape(B, S, D)

    def grad_pieces(n, l):
        if n in FFN_W:
            return bigs[l][n]
        full = _unpad_w_in(bigs[l]["w_in"]) if n == "w_in" else bigs[l][n]
        return _shards_from_full(full, BIG[n])

    g = dict()
    mc = lax.axis_index("c")
    for (cols, dim), keys in groups.items():
        tag = f"{'c' if cols else 'r'}{dim}"
        if cols and dim == D:
            for n, l in keys:
                g.setdefault(n, [None] * L)[l] = _reduce_to_owner(grad_pieces(n, l), mc, f"{n}_{l}")
            continue
        buf = jnp.concatenate([grad_pieces(n, l) for n, l in keys], axis=2 if cols else 1)
        summed = _reduce_to_owner(buf, mc, tag)
        off = 0
        for n, l in keys:
            k = w[n].shape[2] if cols else w[n].shape[1]
            g.setdefault(n, [None] * L)[l] = summed[:, off:off + k] if cols else summed[off:off + k, :]
            off += k
    g = {n: jnp.stack(ts) for n, ts in g.items()}

    small_keys = [(n, l) for l in range(L) for n in SMALL]
    small_flat = _pack([smalls[l][n] for n, l in small_keys], F32, LANES)
    small_sum = _sum0(_all_gather(small_flat, "gather_small_grads"), "sum_small_grads")
    small_full = {}
    for (n, l), t in zip(small_keys, _unpack(small_sum.reshape(-1), [smalls[l][n].shape for n, l in small_keys])):
        small_full.setdefault(n, [None] * L)[l] = t
    for n, ts in small_full.items():
        g[n] = jnp.stack(ts)
    n_conv = gdn_conv.shape[2]
    g["gdn_conv"] = lax.dynamic_slice(g["gdn_conv"], (0, 0, me * n_conv), (L, GDN_CONV, n_conv))

    dmod_mine = jnp.stack([jnp.concatenate([d[:, 0] for d in dmods[l]], axis=1) for l in range(L)])
    dmod_in = jnp.pad(dmod_mine.reshape(L * B, -1), ((0, SUBLANES - L * B), (0, 0)))
    dmod_g = _all_gather(dmod_in, "gather_dmod")[:, :L * B].reshape(N_DEV, L, B, -1)
    dmod_all = jnp.transpose(dmod_g, (1, 0, 2, 3)).reshape(L, N_DEV * B, -1)
    g["ada_b"] = _sum0(dmod_all.reshape(L, N_DEV * B, -1, LANES).transpose(1, 0, 2, 3).reshape(N_DEV * B, -1, LANES),
                       "sum_ada_b").reshape(L, -1)
    dmod_cols = lax.dynamic_slice(dmod_all, (0, 0, me * n_ada), (L, N_DEV * B, n_ada))
    g["ada_w"] = jnp.stack([_ada_bwd(c_all, dmod_cols[l], f"ada{l}_bwd") for l in range(L)])

    upd = {n: _adamw(w[n], g[n], m[n], v[n], f"adamw_{n}") for n in WEIGHTS}
    return (loss, grad_x, *[g[n] for n in WEIGHTS], *[upd[n][0] for n in WEIGHTS],
            *[upd[n][1] for n in WEIGHTS], *[upd[n][2] for n in WEIGHTS])
```

```python
import functools
import math

import jax
import jax.numpy as jnp
import numpy as np
from jax import lax
from jax.experimental import pallas as pl
from jax.experimental.pallas import tpu as pltpu

F32 = jnp.float32
BF16 = jnp.bfloat16

LANES = 128
SUBLANES = 8
VMEM_LIMIT = 56 * 1024 * 1024

N_DEV = 8
EPS = 1e-6
FFN_RES = 0.5
GDN_HEADS = 8
GDN_DIM = 128
GDN_CONV = 4
GDN_CHUNK = 128
S5_GROUP = 16
S5_STATE = 64
S5_MAX_RE = -1e-4
S5_TCHUNK = 512
DIL_PAIRS = ((128, 1), (512, 4), (2048, 16))
DIL_HPG = 4
DIL_DIM = 64
DIL_BLK = 128
ALIBI_MAX = 8.0
ADAM_LR, ADAM_B1, ADAM_B2, ADAM_EPS, ADAM_WD, ADAM_STEP = 0.001, 0.9, 0.999, 1e-08, 0.01, 10

HI = lax.Precision.HIGHEST
HI3 = lax.Precision.HIGH


def _cparams(sem=None, **kw):
    return pltpu.CompilerParams(dimension_semantics=sem, vmem_limit_bytes=VMEM_LIMIT, **kw)


def _pick(n, cands):
    for c in cands:
        if n % c == 0:
            return c
    return n


def _mm(a, b, *, ta=False, tb=False, out_dtype=F32, name):
    M, K = (a.shape[1], a.shape[0]) if ta else a.shape
    N = b.shape[0] if tb else b.shape[1]
    assert (b.shape[1] if tb else b.shape[0]) == K, (a.shape, b.shape, ta, tb)
    tm = _pick(M, (1024, 512, 256, 128))
    tn = _pick(N, (1024, 768, 512, 384, 256, 128))
    tk = _pick(K, (2048, 1536, 1408, 1024, 768, 512, 256, 128))
    nk = K // tk

    def body(a_ref, b_ref, o_ref, acc_ref):
        k = pl.program_id(2)

        @pl.when(k == 0)
        def _():
            acc_ref[...] = jnp.zeros_like(acc_ref)

        dn = (((0 if ta else 1,), (1 if tb else 0,)), ((), ()))
        acc_ref[...] += lax.dot_general(a_ref[...], b_ref[...], dn, preferred_element_type=F32)

        @pl.when(k == nk - 1)
        def _():
            o_ref[...] = acc_ref[...].astype(o_ref.dtype)

    a_spec = pl.BlockSpec((tk, tm), lambda i, j, k: (k, i)) if ta else pl.BlockSpec((tm, tk), lambda i, j, k: (i, k))
    b_spec = pl.BlockSpec((tn, tk), lambda i, j, k: (j, k)) if tb else pl.BlockSpec((tk, tn), lambda i, j, k: (k, j))
    return pl.pallas_call(
        body, name=name,
        out_shape=jax.ShapeDtypeStruct((M, N), out_dtype),
        grid=(M // tm, N // tn, nk),
        in_specs=[a_spec, b_spec],
        out_specs=pl.BlockSpec((tm, tn), lambda i, j, k: (i, j)),
        scratch_shapes=[pltpu.VMEM((tm, tn), F32)],
        compiler_params=_cparams(("parallel", "parallel", "arbitrary")),
    )(a, b)


def _norm_arg(a):
    return a if isinstance(a, tuple) else (a, None, 0)


def _ew_specs(rows, exs, ws, tr, tpe):
    specs = []
    for arr, cw, off in rows:
        if cw is None:
            specs.append(pl.BlockSpec((tr, arr.shape[1]), lambda j, i: (i, 0)))
        else:
            specs.append(pl.BlockSpec((tr, cw), lambda j, i, off=off: (i, j + off)))
    for arr, cw, off in exs:
        if cw is None:
            specs.append(pl.BlockSpec((1, 1, arr.shape[2]), lambda j, i: (i // tpe, 0, 0)))
        else:
            specs.append(pl.BlockSpec((1, 1, cw), lambda j, i, off=off: (i // tpe, 0, j + off)))
    for arr, cw, off in ws:
        if cw is None:
            specs.append(pl.BlockSpec((1, arr.shape[1]), lambda j, i: (0, 0)))
        else:
            specs.append(pl.BlockSpec((1, cw), lambda j, i, off=off: (0, j + off)))
    return specs


def _ew_fwd(fn, rows, exs, ws, outs, *, ncb=1, tr=256, seq=None, name):
    rows, exs, ws = [list(map(_norm_arg, g)) for g in (rows, exs, ws)]
    T = rows[0][0].shape[0]
    seq = seq or T
    tr = min(tr, seq)
    tpe = seq // tr
    nr, ne, nw = len(rows), len(exs), len(ws)

    def body(*refs):
        ins = [r[...].astype(F32) for r in refs[:nr]]
        ins += [r[0].astype(F32) for r in refs[nr:nr + ne]]
        ins += [r[...].astype(F32) for r in refs[nr + ne:nr + ne + nw]]
        res = fn(*ins)
        if not isinstance(res, (tuple, list)):
            res = (res,)
        for o_ref, v in zip(refs[nr + ne + nw:], res):
            o_ref[...] = v.astype(o_ref.dtype)

    out_shape, out_specs = [], []
    for width, dtype, blocked in outs:
        out_shape.append(jax.ShapeDtypeStruct((T, width), dtype))
        if blocked:
            out_specs.append(pl.BlockSpec((tr, width // ncb), lambda j, i: (i, j)))
        else:
            out_specs.append(pl.BlockSpec((tr, width), lambda j, i: (i, 0)))
    res = pl.pallas_call(
        body, name=name, out_shape=out_shape, grid=(ncb, T // tr),
        in_specs=_ew_specs(rows, exs, ws, tr, tpe), out_specs=out_specs,
        compiler_params=_cparams(("parallel", "parallel")),
    )(*[a[0] for a in rows + exs + ws])
    return res


def _ew_bwd(fn, rows, exs, ws, douts, need, *, ncb=1, tr=256, seq=None, row_dtypes=None, name):
    rows, exs, ws, douts = [list(map(_norm_arg, g)) for g in (rows, exs, ws, douts)]
    T = rows[0][0].shape[0]
    seq = seq or T
    tr = min(tr, seq)
    tpe = seq // tr
    nrt = T // tr
    nr, ne, nw, nd = len(rows), len(exs), len(ws), len(douts)
    nin = nr + ne + nw
    args = rows + exs + ws
    row_dtypes = row_dtypes or [F32] * nr
    for k, (arr, cw, off) in enumerate(exs):
        assert not (need[nr + k] and cw is None and ncb > 1)

    def body(*refs):
        j, i = pl.program_id(0), pl.program_id(1)
        ins = [r[...].astype(F32) for r in refs[:nr]]
        ins += [r[0].astype(F32) for r in refs[nr:nr + ne]]
        ins += [r[...].astype(F32) for r in refs[nr + ne:nin]]
        cts = [r[...].astype(F32) for r in refs[nin:nin + nd]]
        res, vjp = jax.vjp(fn, *ins)
        if isinstance(res, (tuple, list)):
            grads = vjp(tuple(cts))
        else:
            grads = vjp(cts[0])
        o = nin + nd
        for k in range(nin):
            if not need[k]:
                continue
            o_ref, g = refs[o], grads[k]
            o += 1
            if k < nr:
                o_ref[...] = g.astype(o_ref.dtype)
            elif k < nr + ne:
                first = (i % tpe) == 0

                @pl.when(first)
                def _(o_ref=o_ref, g=g):
                    o_ref[0] = g

                @pl.when(jnp.logical_not(first))
                def _(o_ref=o_ref, g=g):
                    o_ref[0] += g
            else:
                blocked = args[k][1] is not None
                first = (i == 0) if blocked else jnp.logical_and(i == 0, j == 0)

                @pl.when(first)
                def _(o_ref=o_ref, g=g):
                    o_ref[...] = g

                @pl.when(jnp.logical_not(first))
                def _(o_ref=o_ref, g=g):
                    o_ref[...] += g

    in_specs = _ew_specs(rows, exs, ws, tr, tpe) + _ew_specs(douts, [], [], tr, tpe)
    out_shape, out_specs = [], []
    all_specs = _ew_specs(rows, exs, ws, tr, tpe)
    for k in range(nin):
        if not need[k]:
            continue
        arr, cw, off = args[k]
        if k < nr and cw is not None:
            out_shape.append(jax.ShapeDtypeStruct((T, ncb * cw), row_dtypes[k]))
            out_specs.append(pl.BlockSpec((tr, cw), lambda j, i: (i, j)))
        elif k >= nr and cw is not None:
            assert off == 0 and arr.shape[-1] == ncb * cw
            out_shape.append(jax.ShapeDtypeStruct(arr.shape, F32))
            out_specs.append(all_specs[k])
        else:
            out_shape.append(jax.ShapeDtypeStruct(arr.shape, row_dtypes[k] if k < nr else F32))
            out_specs.append(all_specs[k])
    res = pl.pallas_call(
        body, name=name, out_shape=out_shape, grid=(ncb, nrt),
        in_specs=in_specs, out_specs=out_specs,
        compiler_params=_cparams(("arbitrary", "arbitrary")),
    )(*[a[0] for a in args + douts])
    return res


def _rms(x, g):
    return x * lax.rsqrt(jnp.mean(x * x, axis=-1, keepdims=True) + EPS) * g


def _f_rmsmod(x, sh, sc, g):
    return _rms(x, g) * (1.0 + sc) + sh


def _f_swiglu(a, b):
    return a * jax.nn.sigmoid(a) * b


def _f_res(res, x, y, gate):
    return x + res * gate * y


def _f_res_rmsmod(res, x, y, gate, sh, sc, g):
    x1 = x + res * gate * y
    return x1, _rms(x1, g) * (1.0 + sc) + sh


def _mm_sh(step, ins, in_specs, out_shape, out_spec, acc_shape, grid, name):
    nk = grid[2]
    n = len(ins)

    def body(*refs):
        o_ref, acc_ref = refs[n], refs[n + 1]
        k = pl.program_id(2)

        @pl.when(k == 0)
        def _():
            acc_ref[...] = jnp.zeros_like(acc_ref)

        step(k, acc_ref, *refs[:n])

        @pl.when(k == nk - 1)
        def _():
            if len(o_ref.shape) == 3:
                o_ref[0] = acc_ref[...].astype(o_ref.dtype)
            else:
                o_ref[...] = acc_ref[...].astype(o_ref.dtype)

    return pl.pallas_call(
        body, name=name, out_shape=out_shape, grid=grid, in_specs=in_specs, out_specs=out_spec,
        scratch_shapes=[pltpu.VMEM(acc_shape, F32)],
        compiler_params=_cparams(("parallel", "parallel", "arbitrary")),
    )(*ins)


def _dg(a, b, ca, cb):
    return lax.dot_general(a, b, (((ca,), (cb,)), ((), ())), preferred_element_type=F32)


def _ffn_tiles(T, D):
    return _pick(T, (1024, 512, 256, 128)), _pick(D, (1024, 512, 256, 128)), _pick(D, (2048, 1024, 512, 256, 128)), \
        _pick(T, (2048, 1024, 512, 256, 128))


def _ffn_up(h, ws, name):
    n, D, c = ws.shape
    T = h.shape[0]
    tm, _, tkd, _ = _ffn_tiles(T, D)

    def step(k, acc, a_ref, b_ref):
        acc[...] += _dg(a_ref[...], b_ref[0], 1, 0)

    return _mm_sh(step, [h, ws],
                  [pl.BlockSpec((tm, tkd), lambda i, j, k: (i, k)), pl.BlockSpec((1, tkd, c), lambda i, j, k: (j, k, 0))],
                  jax.ShapeDtypeStruct((n, T, c), F32), pl.BlockSpec((1, tm, c), lambda i, j, k: (j, i, 0)),
                  (tm, c), (T // tm, n, D // tkd), name)


def _ffn_down(s3, w2s, name):
    n, T, c = s3.shape
    D = w2s.shape[2]
    tm, tn, _, _ = _ffn_tiles(T, D)

    def step(k, acc, a_ref, b_ref):
        acc[...] += _dg(a_ref[0], b_ref[0], 1, 0)

    return _mm_sh(step, [s3, w2s],
                  [pl.BlockSpec((1, tm, c), lambda i, j, k: (k, i, 0)), pl.BlockSpec((1, c, tn), lambda i, j, k: (k, 0, j))],
                  jax.ShapeDtypeStruct((T, D), F32), pl.BlockSpec((tm, tn), lambda i, j, k: (i, j)),
                  (tm, tn), (T // tm, D // tn, n), name)


def _ffn_down_dx(dy, w2s, name):
    n, c, D = w2s.shape
    T = dy.shape[0]
    tm, _, tkd, _ = _ffn_tiles(T, D)

    def step(k, acc, a_ref, b_ref):
        acc[...] += _dg(a_ref[...], b_ref[0], 1, 1)

    return _mm_sh(step, [dy, w2s],
                  [pl.BlockSpec((tm, tkd), lambda i, j, k: (i, k)), pl.BlockSpec((1, c, tkd), lambda i, j, k: (j, 0, k))],
                  jax.ShapeDtypeStruct((n, T, c), F32), pl.BlockSpec((1, tm, c), lambda i, j, k: (j, i, 0)),
                  (tm, c), (T // tm, n, D // tkd), name)


def _ffn_down_dw(s3, dy, name):
    n, T, c = s3.shape
    D = dy.shape[1]
    _, tn, _, tkt = _ffn_tiles(T, D)

    def step(k, acc, a_ref, b_ref):
        acc[...] += _dg(a_ref[0], b_ref[...], 0, 0)

    return _mm_sh(step, [s3, dy],
                  [pl.BlockSpec((1, tkt, c), lambda i, j, k: (i, k, 0)), pl.BlockSpec((tkt, tn), lambda i, j, k: (k, j))],
                  jax.ShapeDtypeStruct((n, c, D), BF16), pl.BlockSpec((1, c, tn), lambda i, j, k: (i, 0, j)),
                  (c, tn), (n, D // tn, T // tkt), name)


def _ffn_up_dw(h, d3, name):
    n, T, c = d3.shape
    D = h.shape[1]
    _, tn, _, tkt = _ffn_tiles(T, D)

    def step(k, acc, a_ref, b_ref):
        acc[...] += _dg(a_ref[...], b_ref[0], 0, 0)

    return _mm_sh(step, [h, d3],
                  [pl.BlockSpec((tkt, tn), lambda i, j, k: (k, i)), pl.BlockSpec((1, tkt, c), lambda i, j, k: (j, k, 0))],
                  jax.ShapeDtypeStruct((n, D, c), BF16), pl.BlockSpec((1, tn, c), lambda i, j, k: (j, i, 0)),
                  (tn, c), (D // tn, n, T // tkt), name)


def _ffn_up_dx(da3, db3, w1s, w3s, name):
    n, T, c = da3.shape
    D = w1s.shape[1]
    tm, tn, _, _ = _ffn_tiles(T, D)

    def step(k, acc, da_ref, db_ref, w1_ref, w3_ref):
        @pl.when(k < n)
        def _():
            acc[...] += _dg(da_ref[0], w1_ref[0], 1, 1)

        @pl.when(k >= n)
        def _():
            acc[...] += _dg(db_ref[0], w3_ref[0], 1, 1)

    lo = lambda k: jnp.minimum(k, n - 1)
    hi = lambda k: jnp.maximum(k - n, 0)
    return _mm_sh(step, [da3, db3, w1s, w3s],
                  [pl.BlockSpec((1, tm, c), lambda i, j, k: (lo(k), i, 0)), pl.BlockSpec((1, tm, c), lambda i, j, k: (hi(k), i, 0)),
                   pl.BlockSpec((1, tn, c), lambda i, j, k: (lo(k), j, 0)), pl.BlockSpec((1, tn, c), lambda i, j, k: (hi(k), j, 0))],
                  jax.ShapeDtypeStruct((T, D), F32), pl.BlockSpec((tm, tn), lambda i, j, k: (i, j)),
                  (tm, tn), (T // tm, D // tn, 2 * n), name)


def _ffn_act(a3, b3, name, tr=512):
    n, T, c = a3.shape
    tr = _pick(T, (tr, 256, 128))

    def body(a_ref, b_ref, o_ref):
        o_ref[...] = _f_swiglu(a_ref[...], b_ref[...]).astype(o_ref.dtype)

    spec = pl.BlockSpec((1, tr, c), lambda j, i: (j, i, 0))
    return pl.pallas_call(body, name=name, out_shape=jax.ShapeDtypeStruct((n, T, c), BF16), grid=(n, T // tr),
                          in_specs=[spec, spec], out_specs=spec, compiler_params=_cparams(("parallel", "parallel")))(a3, b3)


def _ffn_act_bwd(a3, b3, ds3, name, tr=512):
    n, T, c = a3.shape
    tr = _pick(T, (tr, 256, 128))

    def body(a_ref, b_ref, d_ref, da_ref, db_ref):
        _, vjp = jax.vjp(_f_swiglu, a_ref[...], b_ref[...])
        da, db = vjp(d_ref[...])
        da_ref[...] = da.astype(da_ref.dtype)
        db_ref[...] = db.astype(db_ref.dtype)

    spec = pl.BlockSpec((1, tr, c), lambda j, i: (j, i, 0))
    return pl.pallas_call(body, name=name, out_shape=[jax.ShapeDtypeStruct((n, T, c), BF16)] * 2, grid=(n, T // tr),
                          in_specs=[spec] * 3, out_specs=[spec, spec],
                          compiler_params=_cparams(("parallel", "parallel")))(a3, b3, ds3)


def _ffn_fwd(h, w1s, w3s, w2s, tag):
    a3 = _ffn_up(h, w1s, f"{tag}_up1")
    b3 = _ffn_up(h, w3s, f"{tag}_up3")
    s3 = _ffn_act(a3, b3, f"{tag}_act")
    return _ffn_down(s3, w2s, f"{tag}_down"), (h, a3, b3, s3)


def _ffn_bwd(dy, saved, w1s, w3s, w2s, tag):
    h, a3, b3, s3 = saved
    ds3 = _ffn_down_dx(dy, w2s, f"{tag}_down_dx")
    dw2s = _ffn_down_dw(s3, dy, f"{tag}_down_dw")
    da3, db3 = _ffn_act_bwd(a3, b3, ds3, f"{tag}_act_bwd")
    dw1s = _ffn_up_dw(h, da3, f"{tag}_up1_dw")
    dw3s = _ffn_up_dw(h, db3, f"{tag}_up3_dw")
    dh = _ffn_up_dx(da3, db3, w1s, w3s, f"{tag}_up_dx")
    return dh, dw1s, dw3s, dw2s


def _shift_down(x, j):
    if j == 0:
        return x
    row = lax.broadcasted_iota(jnp.int32, x.shape, 0)
    return jnp.where(row >= j, pltpu.roll(x, j, 0), 0.0)


def _shift_up(x, j):
    if j == 0:
        return x
    n = x.shape[0]
    row = lax.broadcasted_iota(jnp.int32, x.shape, 0)
    return jnp.where(row < n - j, pltpu.roll(x, n - j, 0), 0.0)


def _gdn_post_conv(kind, y):
    s = y * jax.nn.sigmoid(y)
    if kind == "v":
        return s
    n = lax.rsqrt(jnp.sum(s * s, axis=-1, keepdims=True) + EPS)
    return s * n * (GDN_DIM ** -0.5 if kind == "q" else 1.0)


def _conv_taps(w_ref):
    return [w_ref[k:k + 1, :] for k in range(GDN_CONV)]


def _gdn_conv(x, w):
    y = w[GDN_CONV - 1] * x
    for k in range(GDN_CONV - 1):
        y = y + w[k] * _shift_down(x, GDN_CONV - 1 - k)
    return y


def _gdn_pre_fwd(proj, conv_w, kind, first_block, B, name):
    T = proj.shape[0]
    S = T // B
    nh = GDN_HEADS

    def body(x_ref, w_ref, o_ref):
        y = _gdn_conv(x_ref[...], _conv_taps(w_ref))
        o_ref[...] = _gdn_post_conv(kind, y)

    return pl.pallas_call(
        body, name=name, out_shape=jax.ShapeDtypeStruct((T, nh * GDN_DIM), F32), grid=(nh, B),
        in_specs=[pl.BlockSpec((S, GDN_DIM), lambda c, b: (b, c + first_block)),
                  pl.BlockSpec((GDN_CONV, GDN_DIM), lambda c, b: (0, c))],
        out_specs=pl.BlockSpec((S, GDN_DIM), lambda c, b: (b, c)),
        compiler_params=_cparams(("parallel", "parallel")),
    )(proj, conv_w)


def _gdn_pre_bwd(proj, conv_w, dout, kind, first_block, B, name):
    T = proj.shape[0]
    S = T // B
    nh = GDN_HEADS

    def body(x_ref, w_ref, d_ref, dx_ref, dw_ref):
        b = pl.program_id(1)
        x, w = x_ref[...], _conv_taps(w_ref)
        y = _gdn_conv(x, w)
        _, vjp = jax.vjp(functools.partial(_gdn_post_conv, kind), y)
        (dy,) = vjp(d_ref[...])
        dx = w[GDN_CONV - 1] * dy

        @pl.when(b == 0)
        def _():
            dw_ref[...] = jnp.zeros_like(dw_ref)

        for k in range(GDN_CONV):
            j = GDN_CONV - 1 - k
            if j:
                dx = dx + w[k] * _shift_up(dy, j)
            dw_ref[k:k + 1, :] += jnp.sum(dy * _shift_down(x, j), axis=0, keepdims=True)
        dx_ref[...] = dx

    return pl.pallas_call(
        body, name=name,
        out_shape=[jax.ShapeDtypeStruct((T, nh * GDN_DIM), F32), jax.ShapeDtypeStruct((GDN_CONV, nh * GDN_DIM), F32)],
        grid=(nh, B),
        in_specs=[pl.BlockSpec((S, GDN_DIM), lambda c, b: (b, c + first_block)),
                  pl.BlockSpec((GDN_CONV, GDN_DIM), lambda c, b: (0, c)),
                  pl.BlockSpec((S, GDN_DIM), lambda c, b: (b, c))],
        out_specs=[pl.BlockSpec((S, GDN_DIM), lambda c, b: (b, c)),
                   pl.BlockSpec((GDN_CONV, GDN_DIM), lambda c, b: (0, c))],
        compiler_params=_cparams(("arbitrary", "arbitrary")),
    )(proj, conv_w, dout)


def _softplus(x):
    return jnp.maximum(x, 0.0) + jnp.log(1.0 + jnp.exp(-jnp.abs(x)))


def _f_gdn_gates(ba, a_log, dt_bias):
    n = ba.shape[0]
    lane = lax.broadcasted_iota(jnp.int32, ba.shape, 1)
    beta = jax.nn.sigmoid(ba)
    g = -jnp.exp(a_log) * _softplus(ba + dt_bias)
    ri = lax.broadcasted_iota(jnp.int32, (n, n), 0)
    ci = lax.broadcasted_iota(jnp.int32, (n, n), 1)
    tri = jnp.where((ri // GDN_CHUNK == ci // GDN_CHUNK) & (ci <= ri), 1.0, 0.0)
    gc = jnp.dot(tri, g, precision=HI, preferred_element_type=F32)
    return jnp.where(lane < GDN_HEADS, beta, gc)


def _bmm(a, b, ca, cb):
    return lax.dot_general(a, b, (((ca,), (cb,)), ((0,), (0,))), precision=HI3, preferred_element_type=F32)


GDN_INV_LEAF = 16


def _unit_lower_inverse(low, ri, ci):
    C = low.shape[1]
    b = GDN_INV_LEAF
    p = jnp.where(ri // b == ci // b, low, 0.0)
    x = jnp.where(ci == ri, 1.0, 0.0) - p
    for _ in range(int(math.log2(b)) - 1):
        p = _bmm(p, p, 2, 1)
        x = x + _bmm(x, p, 2, 1)
    while b < C:
        off = jnp.where(jnp.logical_and(ri // (2 * b) == ci // (2 * b), ri // b != ci // b), low, 0.0)
        x = x - _bmm(_bmm(x, off, 2, 1), x, 2, 1)
        b *= 2
    return x


def _gdn_chunk_fn(q, k, v, gc, beta, h):
    N, C, d = q.shape
    ri = lax.broadcasted_iota(jnp.int32, (N, C, C), 1)
    ci = lax.broadcasted_iota(jnp.int32, (N, C, C), 2)
    kb = k * beta
    vb = v * beta
    gi = jnp.broadcast_to(gc, (N, C, C))
    gj = jnp.swapaxes(gi, 1, 2)
    decay = jnp.exp(jnp.where(ci <= ri, gi - gj, -1e30))
    low = jnp.where(ci < ri, _bmm(kb, k, 2, 2) * decay, 0.0)
    ainv = _unit_lower_inverse(low, ri, ci)
    eg = jnp.exp(gc)
    u = _bmm(ainv, vb, 2, 1)
    w = _bmm(ainv, kb * eg, 2, 1)
    attn = _bmm(q, k, 2, 2) * decay
    v_new = u - _bmm(w, h, 2, 1)
    o = _bmm(q * eg, h, 2, 1) + _bmm(attn, v_new, 2, 1)
    rc = lax.broadcasted_iota(jnp.int32, (N, C, 1), 1)
    g_last = jnp.sum(jnp.where(rc == C - 1, gc, 0.0), axis=1, keepdims=True)
    h_new = h * jnp.exp(g_last) + _bmm(k * jnp.exp(g_last - gc), v_new, 1, 1)
    return o, h_new


def _gdn_heads(x):
    return jnp.stack([x[:, h * GDN_DIM:(h + 1) * GDN_DIM] for h in range(GDN_HEADS)], axis=0)


def _gdn_gate_cols(G, first_lane):
    lane = lax.broadcasted_iota(jnp.int32, G.shape, 1)
    return jnp.stack([jnp.sum(jnp.where(lane == first_lane + h, G, 0.0), axis=1, keepdims=True)
                      for h in range(GDN_HEADS)], axis=0)


def _gdn_chunk_specs(nc, rev):
    C, W = GDN_CHUNK, GDN_HEADS * GDN_DIM

    def at(n):
        return nc - 1 - n if rev else n

    row = lambda b, n: (b * nc + at(n), 0)
    return [pl.BlockSpec((C, W), row)] * 3 + [pl.BlockSpec((C, LANES), row)]


def _gdn_scan_fwd(q, k, v, G, B, name):
    T, W = q.shape
    C = GDN_CHUNK
    nc = T // B // C

    def body(q_ref, k_ref, v_ref, g_ref, o_ref, hs_ref, h_ref):
        @pl.when(pl.program_id(1) == 0)
        def _():
            h_ref[...] = jnp.zeros_like(h_ref)

        G_ = g_ref[...]
        h = h_ref[...]
        hs_ref[0, 0] = h
        o, hn = _gdn_chunk_fn(_gdn_heads(q_ref[...]), _gdn_heads(k_ref[...]), _gdn_heads(v_ref[...]),
                              _gdn_gate_cols(G_, GDN_HEADS), _gdn_gate_cols(G_, 0), h)
        h_ref[...] = hn
        for hd in range(GDN_HEADS):
            o_ref[:, hd * GDN_DIM:(hd + 1) * GDN_DIM] = o[hd]

    return pl.pallas_call(
        body, name=name,
        out_shape=[jax.ShapeDtypeStruct((T, W), F32), jax.ShapeDtypeStruct((B, nc, GDN_HEADS, GDN_DIM, GDN_DIM), F32)],
        grid=(B, nc), in_specs=_gdn_chunk_specs(nc, False),
        out_specs=[pl.BlockSpec((C, W), lambda b, n: (b * nc + n, 0)),
                   pl.BlockSpec((1, 1, GDN_HEADS, GDN_DIM, GDN_DIM), lambda b, n: (b, n, 0, 0, 0))],
        scratch_shapes=[pltpu.VMEM((GDN_HEADS, GDN_DIM, GDN_DIM), F32)],
        compiler_params=_cparams(("parallel", "arbitrary")),
    )(q, k, v, G)


def _gdn_scan_bwd(q, k, v, G, hs, do, B, name):
    T, W = q.shape
    C = GDN_CHUNK
    nc = T // B // C

    def body(q_ref, k_ref, v_ref, g_ref, hs_ref, do_ref, dq_ref, dk_ref, dv_ref, dg_ref, dh_ref):
        @pl.when(pl.program_id(1) == 0)
        def _():
            dh_ref[...] = jnp.zeros_like(dh_ref)

        G_ = g_ref[...]
        args = (_gdn_heads(q_ref[...]), _gdn_heads(k_ref[...]), _gdn_heads(v_ref[...]),
                _gdn_gate_cols(G_, GDN_HEADS), _gdn_gate_cols(G_, 0), hs_ref[0, 0])
        _, vjp = jax.vjp(_gdn_chunk_fn, *args)
        dq, dk, dv, dgc, dbeta, dh = vjp((_gdn_heads(do_ref[...]), dh_ref[...]))
        dh_ref[...] = dh
        lane = lax.broadcasted_iota(jnp.int32, G_.shape, 1)
        dG = jnp.zeros_like(G_)
        for hd in range(GDN_HEADS):
            sl = slice(hd * GDN_DIM, (hd + 1) * GDN_DIM)
            dq_ref[:, sl] = dq[hd]
            dk_ref[:, sl] = dk[hd]
            dv_ref[:, sl] = dv[hd]
            dG = dG + jnp.where(lane == hd, dbeta[hd], 0.0) + jnp.where(lane == GDN_HEADS + hd, dgc[hd], 0.0)
        dg_ref[...] = dG

    rrow = lambda b, n: (b * nc + nc - 1 - n, 0)
    return pl.pallas_call(
        body, name=name,
        out_shape=[jax.ShapeDtypeStruct((T, W), F32)] * 3 + [jax.ShapeDtypeStruct((T, LANES), F32)],
        grid=(B, nc),
        in_specs=_gdn_chunk_specs(nc, True) + [
            pl.BlockSpec((1, 1, GDN_HEADS, GDN_DIM, GDN_DIM), lambda b, n: (b, nc - 1 - n, 0, 0, 0)),
            pl.BlockSpec((C, W), rrow)],
        out_specs=[pl.BlockSpec((C, W), rrow)] * 3 + [pl.BlockSpec((C, LANES), rrow)],
        scratch_shapes=[pltpu.VMEM((GDN_HEADS, GDN_DIM, GDN_DIM), F32)],
        compiler_params=_cparams(("parallel", "arbitrary")),
    )(q, k, v, G, hs, do)


def _f_gdn_out(o, z, w):
    return _rms(o, w) * z * jax.nn.sigmoid(z)


def _gdn_fwd(qkv, z, ba, conv_w, gate_params, out_norm, B, tag):
    a_log, dt_bias = gate_params
    W = GDN_HEADS * GDN_DIM
    qn, kn, vn = [_gdn_pre_fwd(qkv[0], conv_w[:, i * W:(i + 1) * W], kd, qkv[1] + i * GDN_HEADS, B, f"{tag}_pre_{kd}")
                  for i, kd in enumerate("qkv")]
    (G,) = _ew_fwd(_f_gdn_gates, [(ba[0], LANES, ba[1])], [], [a_log, dt_bias], [(LANES, F32, False)],
                   name=f"{tag}_gates")
    o, hs = _gdn_scan_fwd(qn, kn, vn, G, B, f"{tag}_scan")
    (y,) = _ew_fwd(_f_gdn_out, [(o, GDN_DIM, 0), (z[0], GDN_DIM, z[1])], [], [out_norm], [(W, BF16, True)],
                   ncb=GDN_HEADS, name=f"{tag}_out")
    return y, (qkv, z, ba, qn, kn, vn, G, hs, o)


def _gdn_bwd(dy, saved, conv_w, gate_params, out_norm, B, tag):
    qkv, z, ba, qn, kn, vn, G, hs, o = saved
    a_log, dt_bias = gate_params
    W = GDN_HEADS * GDN_DIM
    do, dz, d_out_norm = _ew_bwd(_f_gdn_out, [(o, GDN_DIM, 0), (z[0], GDN_DIM, z[1])], [], [out_norm],
                                 [(dy, GDN_DIM, 0)], [True] * 3, ncb=GDN_HEADS, name=f"{tag}_out_bwd")
    dq, dk, dv, dG = _gdn_scan_bwd(qn, kn, vn, G, hs, do, B, f"{tag}_scan_bwd")
    dba, d_a_log, d_dt_bias = _ew_bwd(_f_gdn_gates, [(ba[0], LANES, ba[1])], [], [a_log, dt_bias], [dG], [True] * 3,
                                      name=f"{tag}_gates_bwd")
    dxs, dws = [], []
    for i, (kd, d) in enumerate(zip("qkv", (dq, dk, dv))):
        dx, dw = _gdn_pre_bwd(qkv[0], conv_w[:, i * W:(i + 1) * W], d, kd, qkv[1] + i * GDN_HEADS, B,
                              f"{tag}_pre_{kd}_bwd")
        dxs.append(dx)
        dws.append(dw)
    return (jnp.concatenate(dxs, axis=1), dz, dba,
            dict(conv=jnp.concatenate(dws, axis=1), a_log=d_a_log, dt_bias=d_dt_bias, out_norm=d_out_norm))


S5_GPB = LANES // S5_GROUP
S5_SLANES = S5_GPB * S5_STATE


def _cmul(ar, ai, br, bi):
    return ar * br - ai * bi, ar * bi + ai * br


def _s5_prep_fn(a_re, a_im, ls, b_re, b_im):
    lr = jnp.minimum(a_re, S5_MAX_RE)
    li = a_im
    step = jnp.exp(ls)
    mag = jnp.exp(lr * step)
    lbr, lbi = mag * jnp.cos(li * step), mag * jnp.sin(li * step)
    den = lr * lr + li * li
    cr = ((lbr - 1.0) * lr + lbi * li) / den
    ci = (lbi * lr - (lbr - 1.0) * li) / den
    bbr = cr[:, None, :] * b_re - ci[:, None, :] * b_im
    bbi = cr[:, None, :] * b_im + ci[:, None, :] * b_re
    return lbr, lbi, bbr, bbi


def _s5_prep_fwd(args, name):
    G, I, P = args[3].shape
    shp = [jax.ShapeDtypeStruct((G, P), F32)] * 2 + [jax.ShapeDtypeStruct((G, I, P), F32)] * 2

    def body(*refs):
        for o_ref, v in zip(refs[5:], _s5_prep_fn(*[r[...] for r in refs[:5]])):
            o_ref[...] = v

    return pl.pallas_call(body, name=name, out_shape=shp, compiler_params=_cparams())(*args)


def _s5_prep_bwd(args, cts, name):
    shp = [jax.ShapeDtypeStruct(a.shape, F32) for a in args]

    def body(*refs):
        _, vjp = jax.vjp(_s5_prep_fn, *[r[...] for r in refs[:5]])
        for o_ref, v in zip(refs[9:], vjp(tuple(r[...] for r in refs[5:9]))):
            o_ref[...] = v

    return pl.pallas_call(body, name=name, out_shape=shp, compiler_params=_cparams())(*args, *cts)


def _s5_blockdiag_in(bb):
    G, I, P = bb.shape
    nb = G // S5_GPB
    return jnp.einsum("jgip,gh->jgihp", bb.reshape(nb, S5_GPB, I, P), jnp.eye(S5_GPB, dtype=bb.dtype)).reshape(
        nb, S5_GPB * I, S5_GPB * P)


def _s5_blockdiag_in_t(d):
    nb = d.shape[0]
    d = d.reshape(nb, S5_GPB, S5_GROUP, S5_GPB, S5_STATE)
    return jnp.einsum("jgihp,gh->jgip", d, jnp.eye(S5_GPB, dtype=d.dtype)).reshape(nb * S5_GPB, S5_GROUP, S5_STATE)


def _s5_blockdiag_out(c):
    G, I, P = c.shape
    nb = G // S5_GPB
    return jnp.einsum("jgip,gh->jgphi", c.reshape(nb, S5_GPB, I, P), jnp.eye(S5_GPB, dtype=c.dtype)).reshape(
        nb, S5_GPB * P, S5_GPB * I)


def _s5_blockdiag_out_t(d):
    nb = d.shape[0]
    d = d.reshape(nb, S5_GPB, S5_STATE, S5_GPB, S5_GROUP)
    return jnp.einsum("jgphi,gh->jgip", d, jnp.eye(S5_GPB, dtype=d.dtype)).reshape(nb * S5_GPB, S5_GROUP, S5_STATE)


def _s5_powers(lr, li, n):
    out = []
    for _ in range(int(math.log2(n))):
        out.append((lr, li))
        lr, li = _cmul(lr, li, lr, li)
    return out


def _s5_local_scan(sr, si, powers, up):
    shift = _shift_up if up else _shift_down
    for k, (pr, pi) in enumerate(powers):
        d = 1 << k
        tr_, ti_ = _cmul(pr, pi, shift(sr, d), shift(si, d))
        sr, si = sr + tr_, si + ti_
    return sr, si


def _dot_hi(a, b, ca=1, cb=0):
    return lax.dot_general(a, b, (((ca,), (cb,)), ((), ())), precision=HI3, preferred_element_type=F32)


def _s5_specs(nt, rev, ublk):
    tc = S5_TCHUNK

    def at(t):
        return nt - 1 - t if rev else t

    return [
        pl.BlockSpec((tc, LANES), lambda j, b, t: (b * nt + at(t), j + ublk)),
        pl.BlockSpec((1, S5_SLANES), lambda j, b, t: (0, j)),
        pl.BlockSpec((1, S5_SLANES), lambda j, b, t: (0, j)),
        pl.BlockSpec((1, LANES, S5_SLANES), lambda j, b, t: (j, 0, 0)),
        pl.BlockSpec((1, LANES, S5_SLANES), lambda j, b, t: (j, 0, 0)),
        pl.BlockSpec((1, S5_SLANES, LANES), lambda j, b, t: (j, 0, 0)),
        pl.BlockSpec((1, S5_SLANES, LANES), lambda j, b, t: (j, 0, 0)),
        pl.BlockSpec((1, LANES), lambda j, b, t: (0, j)),
    ]


def _s5_chunk_states(u, lr, li, b_re, b_im, cr, ci, powers):
    bur, bui = _dot_hi(u, b_re), _dot_hi(u, b_im)
    row = lax.broadcasted_iota(jnp.int32, bur.shape, 0)
    inr, ini = _cmul(lr, li, cr, ci)
    bur = bur + jnp.where(row == 0, inr, 0.0)
    bui = bui + jnp.where(row == 0, ini, 0.0)
    return _s5_local_scan(bur, bui, powers, False)


def _s5_scan_fwd(u, lam_re, lam_im, Bre, Bim, Cre, Cim, dskip, B, name):
    u, ublk = u
    T = u.shape[0]
    nb = Bre.shape[0]
    Wd = nb * LANES
    tc = S5_TCHUNK
    nt = T // B // tc
    L = nb * S5_SLANES

    def body(u_ref, lr_ref, li_ref, br_ref, bi_ref, cr_ref, ci_ref, d_ref, y_ref, csr_ref, csi_ref, car_ref, cai_ref):
        @pl.when(pl.program_id(2) == 0)
        def _():
            car_ref[...] = jnp.zeros_like(car_ref)
            cai_ref[...] = jnp.zeros_like(cai_ref)

        csr_ref[0, 0] = car_ref[...]
        csi_ref[0, 0] = cai_ref[...]
        u_ = u_ref[...]
        lr, li = lr_ref[...], li_ref[...]
        sr, si = _s5_chunk_states(u_, lr, li, br_ref[0], bi_ref[0], car_ref[0:1, :], cai_ref[0:1, :],
                                  _s5_powers(lr, li, tc))
        y_ref[...] = _dot_hi(sr, cr_ref[0]) - _dot_hi(si, ci_ref[0]) + d_ref[...] * u_
        row = lax.broadcasted_iota(jnp.int32, sr.shape, 0)
        car_ref[0:1, :] = jnp.sum(jnp.where(row == tc - 1, sr, 0.0), axis=0, keepdims=True)
        cai_ref[0:1, :] = jnp.sum(jnp.where(row == tc - 1, si, 0.0), axis=0, keepdims=True)

    cs_shape = jax.ShapeDtypeStruct((B, nt, SUBLANES, L), F32)
    cs_spec = pl.BlockSpec((1, 1, SUBLANES, S5_SLANES), lambda j, b, t: (b, t, 0, j))
    return pl.pallas_call(
        body, name=name, out_shape=[jax.ShapeDtypeStruct((T, Wd), F32), cs_shape, cs_shape],
        grid=(nb, B, nt), in_specs=_s5_specs(nt, False, ublk),
        out_specs=[pl.BlockSpec((tc, LANES), lambda j, b, t: (b * nt + t, j)), cs_spec, cs_spec],
        scratch_shapes=[pltpu.VMEM((SUBLANES, S5_SLANES), F32)] * 2,
        compiler_params=_cparams(("parallel", "parallel", "arbitrary")),
    )(u, lam_re, lam_im, Bre, Bim, Cre, Cim, dskip)


def _s5_scan_bwd(u, lam_re, lam_im, Bre, Bim, Cre, Cim, dskip, csr, csi, dy, B, name):
    u, ublk = u
    T = u.shape[0]
    nb = Bre.shape[0]
    Wd = nb * LANES
    tc = S5_TCHUNK
    nt = T // B // tc
    L = nb * S5_SLANES

    def body(u_ref, lr_ref, li_ref, br_ref, bi_ref, cr_ref, ci_ref, d_ref, csr_ref, csi_ref, dy_ref,
             du_ref, dlr_ref, dli_ref, dbr_ref, dbi_ref, dcr_ref, dci_ref, dd_ref, gr_ref, gi_ref):
        first = jnp.logical_and(pl.program_id(1) == 0, pl.program_id(2) == 0)

        @pl.when(pl.program_id(2) == 0)
        def _():
            gr_ref[...] = jnp.zeros_like(gr_ref)
            gi_ref[...] = jnp.zeros_like(gi_ref)

        @pl.when(first)
        def _():
            for r in (dlr_ref, dli_ref, dbr_ref, dbi_ref, dcr_ref, dci_ref, dd_ref):
                r[...] = jnp.zeros_like(r)

        u_, dy_ = u_ref[...], dy_ref[...]
        lr, li = lr_ref[...], li_ref[...]
        powers = _s5_powers(lr, li, tc)
        c_in_r, c_in_i = csr_ref[0, 0, 0:1, :], csi_ref[0, 0, 0:1, :]
        sr, si = _s5_chunk_states(u_, lr, li, br_ref[0], bi_ref[0], c_in_r, c_in_i, powers)
        dcr_ref[0] += _dot_hi(sr, dy_, 0, 0)
        dci_ref[0] -= _dot_hi(si, dy_, 0, 0)
        dd_ref[...] += jnp.sum(dy_ * u_, axis=0, keepdims=True)
        row = lax.broadcasted_iota(jnp.int32, sr.shape, 0)
        gr = _dot_hi(dy_, cr_ref[0], 1, 1)
        gi = -_dot_hi(dy_, ci_ref[0], 1, 1)
        inr, ini = _cmul(lr, -li, gr_ref[0:1, :], gi_ref[0:1, :])
        gr = gr + jnp.where(row == tc - 1, inr, 0.0)
        gi = gi + jnp.where(row == tc - 1, ini, 0.0)
        gr, gi = _s5_local_scan(gr, gi, [(pr, -pi) for pr, pi in powers], True)
        gr_ref[0:1, :] = jnp.sum(jnp.where(row == 0, gr, 0.0), axis=0, keepdims=True)
        gi_ref[0:1, :] = jnp.sum(jnp.where(row == 0, gi, 0.0), axis=0, keepdims=True)
        pr_ = _shift_down(sr, 1) + jnp.where(row == 0, c_in_r, 0.0)
        pi_ = _shift_down(si, 1) + jnp.where(row == 0, c_in_i, 0.0)
        dlr_ref[...] += jnp.sum(gr * pr_ + gi * pi_, axis=0, keepdims=True)
        dli_ref[...] += jnp.sum(gi * pr_ - gr * pi_, axis=0, keepdims=True)
        dbr_ref[0] += _dot_hi(u_, gr, 0, 0)
        dbi_ref[0] += _dot_hi(u_, gi, 0, 0)
        du_ref[...] = dy_ * d_ref[...] + _dot_hi(gr, br_ref[0], 1, 1) + _dot_hi(gi, bi_ref[0], 1, 1)

    cs_spec = pl.BlockSpec((1, 1, SUBLANES, S5_SLANES), lambda j, b, t: (b, nt - 1 - t, 0, j))
    rrow = pl.BlockSpec((tc, LANES), lambda j, b, t: (b * nt + nt - 1 - t, j))
    lam_spec = pl.BlockSpec((1, S5_SLANES), lambda j, b, t: (0, j))
    b_spec = pl.BlockSpec((1, LANES, S5_SLANES), lambda j, b, t: (j, 0, 0))
    c_spec = pl.BlockSpec((1, S5_SLANES, LANES), lambda j, b, t: (j, 0, 0))
    return pl.pallas_call(
        body, name=name,
        out_shape=[jax.ShapeDtypeStruct((T, Wd), F32)] + [jax.ShapeDtypeStruct((1, L), F32)] * 2
        + [jax.ShapeDtypeStruct((nb, LANES, S5_SLANES), F32)] * 2
        + [jax.ShapeDtypeStruct((nb, S5_SLANES, LANES), F32)] * 2 + [jax.ShapeDtypeStruct((1, Wd), F32)],
        grid=(nb, B, nt), in_specs=_s5_specs(nt, True, ublk) + [cs_spec, cs_spec, rrow],
        out_specs=[rrow, lam_spec, lam_spec, b_spec, b_spec, c_spec, c_spec,
                   pl.BlockSpec((1, LANES), lambda j, b, t: (0, j))],
        scratch_shapes=[pltpu.VMEM((SUBLANES, S5_SLANES), F32)] * 2,
        compiler_params=_cparams(("arbitrary", "arbitrary", "arbitrary")),
    )(u, lam_re, lam_im, Bre, Bim, Cre, Cim, dskip, csr, csi, dy)


def _f_gelu(y):
    return 0.5 * y * (1.0 + jnp.tanh(math.sqrt(2.0 / math.pi) * (y + 0.044715 * (y * y * y))))


def _f_glu(pv, pg, bv, bg):
    return (pv + bv) * jax.nn.sigmoid(pg + bg)


def _s5_params(p):
    prep_in = (p["a_re"], p["a_im"], p["log_step"][:, None], jnp.swapaxes(p["b_re"], 1, 2), jnp.swapaxes(p["b_im"], 1, 2))
    return prep_in


def _s5_fwd(u, p, glu_w, B, tag):
    Wd = p["d"].shape[0]
    prep_in = _s5_params(p)
    lbr, lbi, bbr, bbi = _s5_prep_fwd(prep_in, f"{tag}_prep")
    ops = (lbr.reshape(1, -1), lbi.reshape(1, -1), _s5_blockdiag_in(bbr), _s5_blockdiag_in(bbi),
           _s5_blockdiag_out(p["c_re"]), _s5_blockdiag_out(p["c_im"]), p["d"][None])
    y, csr, csi = _s5_scan_fwd(u, *ops, B, f"{tag}_scan")
    (yg,) = _ew_fwd(_f_gelu, [y], [], [], [(Wd, BF16, False)], name=f"{tag}_gelu")
    pj = _mm(yg, glu_w, name=f"{tag}_glu")
    bv, bg = p["glu_b"][None, :Wd], p["glu_b"][None, Wd:]
    (out,) = _ew_fwd(_f_glu, [(pj, Wd, 0), (pj, Wd, 1)], [], [bv, bg], [(Wd, BF16, False)], name=f"{tag}_gate")
    return out, (u, prep_in, ops, csr, csi, y, yg, pj, bv, bg)


def _s5_bwd(dout, saved, p, glu_w, B, tag):
    u, prep_in, ops, csr, csi, y, yg, pj, bv, bg = saved
    Wd = p["d"].shape[0]
    dpv, dpg, dbv, dbg = _ew_bwd(_f_glu, [(pj, Wd, 0), (pj, Wd, 1)], [], [bv, bg], [dout], [True] * 4,
                                 row_dtypes=[BF16, BF16], name=f"{tag}_gate_bwd")
    dpj = jnp.concatenate([dpv, dpg], axis=1)
    d_glu_w = _mm(yg, dpj, ta=True, out_dtype=BF16, name=f"{tag}_glu_dw")
    dyg = _mm(dpj, glu_w, tb=True, name=f"{tag}_glu_dx")
    (dy,) = _ew_bwd(_f_gelu, [y], [], [], [dyg], [True], name=f"{tag}_gelu_bwd")
    du, dlr, dli, dBr, dBi, dCr, dCi, dd = _s5_scan_bwd(u, *ops, csr, csi, dy, B, f"{tag}_scan_bwd")
    G = p["a_re"].shape[0]
    cts = (dlr.reshape(G, S5_STATE), dli.reshape(G, S5_STATE), _s5_blockdiag_in_t(dBr), _s5_blockdiag_in_t(dBi))
    da_re, da_im, dls, db_re, db_im = _s5_prep_bwd(prep_in, cts, f"{tag}_prep_bwd")
    small = dict(a_re=da_re, a_im=da_im, log_step=dls[:, 0], b_re=jnp.swapaxes(db_re, 1, 2),
                 b_im=jnp.swapaxes(db_im, 1, 2), c_re=_s5_blockdiag_out_t(dCr), c_im=_s5_blockdiag_out_t(dCi),
                 d=dd[0], glu_b=jnp.concatenate([dbv[0], dbg[0]]))
    return du, d_glu_w, small


DIL_GW = DIL_HPG * DIL_DIM


def _f_qknorm(scale, x, w):
    n = x.shape[1]
    ri = lax.broadcasted_iota(jnp.int32, (n, n), 0)
    ci = lax.broadcasted_iota(jnp.int32, (n, n), 1)
    seg = jnp.where(ri // DIL_DIM == ci // DIL_DIM, 1.0 / DIL_DIM, 0.0)
    ms = jnp.dot(x * x, seg, precision=HI, preferred_element_type=F32)
    return x * lax.rsqrt(ms + EPS) * (w * scale)


def _dil_to_blocks(x, B):
    T = x.shape[0]
    S = T // B
    parts = []
    for gi, (_, dil) in enumerate(DIL_PAIRS):
        xg = x[:, gi * DIL_GW:(gi + 1) * DIL_GW].reshape(B, S // dil, dil, DIL_HPG, DIL_DIM)
        parts.append(xg.transpose(0, 2, 3, 1, 4).reshape(-1, DIL_DIM))
    return jnp.concatenate(parts, axis=0)


def _dil_from_blocks(y, B):
    n = y.shape[0] // len(DIL_PAIRS)
    T = n // DIL_HPG
    S = T // B
    parts = []
    for gi, (_, dil) in enumerate(DIL_PAIRS):
        yg = y[gi * n:(gi + 1) * n].reshape(B, dil, DIL_HPG, S // dil, DIL_DIM)
        parts.append(yg.transpose(0, 3, 1, 2, 4).reshape(T, DIL_GW))
    return jnp.concatenate(parts, axis=1)


DIL_BPS = 8


def _dil_block_fn(slope, has_prev, q, kp, kc, vp, vc):
    G, n, _ = q.shape
    qi = lax.broadcasted_iota(jnp.int32, (G, n, n), 1)
    kj = lax.broadcasted_iota(jnp.int32, (G, n, n), 2)
    dist = (qi - kj).astype(F32)
    sc = _bmm(q, kc, 2, 2) - slope * dist
    sp = _bmm(q, kp, 2, 2) - slope * (dist + n)
    sc = jnp.where(qi >= kj, sc, -1e30)
    sp = jnp.where(jnp.logical_and(kj >= qi, has_prev > 0.5), sp, -1e30)
    m = lax.stop_gradient(jnp.maximum(jnp.max(sc, axis=2, keepdims=True), jnp.max(sp, axis=2, keepdims=True)))
    pc = jnp.exp(sc - m)
    pp = jnp.exp(sp - m)
    l = jnp.sum(pc, axis=2, keepdims=True) + jnp.sum(pp, axis=2, keepdims=True)
    o = (_bmm(pp, vp, 2, 1) + _bmm(pc, vc, 2, 1)) / l
    return o, jnp.broadcast_to(m + jnp.log(l), o.shape)


def _dil_tables(B, S):
    nh = len(DIL_PAIRS) * DIL_HPG
    slopes, has_prev = [], []
    for gi, (_, dil) in enumerate(DIL_PAIRS):
        nbk = S // dil // DIL_BLK
        for seq in range(B * dil * DIL_HPG):
            head = gi * DIL_HPG + seq % DIL_HPG
            for n in range(nbk):
                slopes.append(dil * 2.0 ** (-ALIBI_MAX * (head + 1) / nh))
                has_prev.append(1.0 if n > 0 else 0.0)
    shape = (len(slopes), 1, 1)
    return jnp.asarray(np.array(slopes, np.float32).reshape(shape)), jnp.asarray(np.array(has_prev, np.float32).reshape(shape))


def _dil_blocks3(t):
    return t.reshape(-1, DIL_BLK, DIL_DIM)


def _dil_prev(t3):
    return jnp.concatenate([jnp.zeros_like(t3[:1]), t3[:-1]], axis=0)


def _dil_attn_fwd(qb, kb, vb, B, S, name):
    q3, k3, v3 = _dil_blocks3(qb), _dil_blocks3(kb), _dil_blocks3(vb)
    nbt = q3.shape[0]
    slope, has_prev = _dil_tables(B, S)

    def body(s_ref, h_ref, q_ref, kp_ref, kc_ref, vp_ref, vc_ref, o_ref, l_ref):
        o, l = _dil_block_fn(s_ref[...], h_ref[...], q_ref[...], kp_ref[...], kc_ref[...], vp_ref[...], vc_ref[...])
        o_ref[...] = o
        l_ref[...] = l

    blk = pl.BlockSpec((DIL_BPS, DIL_BLK, DIL_DIM), lambda m: (m, 0, 0))
    tab = pl.BlockSpec((DIL_BPS, 1, 1), lambda m: (m, 0, 0))
    o, l = pl.pallas_call(
        body, name=name, out_shape=[jax.ShapeDtypeStruct(q3.shape, F32)] * 2, grid=(nbt // DIL_BPS,),
        in_specs=[tab, tab] + [blk] * 5, out_specs=[blk, blk], compiler_params=_cparams(("parallel",)),
    )(slope, has_prev, q3, _dil_prev(k3), k3, _dil_prev(v3), v3)
    return o.reshape(qb.shape), l.reshape(qb.shape)


def _dil_attn_bwd(qb, kb, vb, do, dl, B, S, name):
    q3, k3, v3 = _dil_blocks3(qb), _dil_blocks3(kb), _dil_blocks3(vb)
    nbt = q3.shape[0]
    slope, has_prev = _dil_tables(B, S)

    def body(s_ref, h_ref, q_ref, kp_ref, kc_ref, vp_ref, vc_ref, do_ref, dl_ref, *outs):
        _, vjp = jax.vjp(functools.partial(_dil_block_fn, s_ref[...], h_ref[...]),
                         q_ref[...], kp_ref[...], kc_ref[...], vp_ref[...], vc_ref[...])
        for o_ref, g in zip(outs, vjp((do_ref[...], dl_ref[...]))):
            o_ref[...] = g

    blk = pl.BlockSpec((DIL_BPS, DIL_BLK, DIL_DIM), lambda m: (m, 0, 0))
    tab = pl.BlockSpec((DIL_BPS, 1, 1), lambda m: (m, 0, 0))
    outs = pl.pallas_call(
        body, name=name, out_shape=[jax.ShapeDtypeStruct(q3.shape, F32)] * 5, grid=(nbt // DIL_BPS,),
        in_specs=[tab, tab] + [blk] * 7, out_specs=[blk] * 5, compiler_params=_cparams(("parallel",)),
    )(slope, has_prev, q3, _dil_prev(k3), k3, _dil_prev(v3), v3, _dil_blocks3(do), _dil_blocks3(dl))
    return [t.reshape(qb.shape) for t in outs]


def _f_dil_merge(o0, o1, o2, l0, l1, l2):
    m = lax.stop_gradient(jnp.maximum(jnp.maximum(l0, l1), l2))
    e0, e1, e2 = jnp.exp(l0 - m), jnp.exp(l1 - m), jnp.exp(l2 - m)
    return (e0 * o0 + e1 * o1 + e2 * o2) / (e0 + e1 + e2)


def _dil_fwd(qkv, q_norm, k_norm, B, tag):
    qkv, fb = qkv
    T = qkv.shape[0]
    S = T // B
    Wd = len(DIL_PAIRS) * DIL_GW
    nblk = Wd // LANES
    (qn,) = _ew_fwd(functools.partial(_f_qknorm, DIL_DIM ** -0.5), [(qkv, LANES, fb)], [], [q_norm],
                    [(Wd, F32, True)], ncb=nblk, name=f"{tag}_qnorm")
    (kn,) = _ew_fwd(functools.partial(_f_qknorm, 1.0), [(qkv, LANES, fb + nblk)], [], [k_norm],
                    [(Wd, F32, True)], ncb=nblk, name=f"{tag}_knorm")
    v0 = (fb + 2 * nblk) * LANES
    qb, kb, vb = _dil_to_blocks(qn, B), _dil_to_blocks(kn, B), _dil_to_blocks(qkv[:, v0:v0 + Wd], B)
    ob, lb = _dil_attn_fwd(qb, kb, vb, B, S, f"{tag}_attn")
    o, l = _dil_from_blocks(ob, B), _dil_from_blocks(lb, B)
    gw = DIL_GW
    rows = [(o, gw, 0), (o, gw, 1), (o, gw, 2), (l, gw, 0), (l, gw, 1), (l, gw, 2)]
    (y,) = _ew_fwd(_f_dil_merge, rows, [], [], [(gw, BF16, False)], name=f"{tag}_merge")
    return y, (qkv, fb, qb, kb, vb, o, l)


def _dil_bwd(dy, saved, q_norm, k_norm, B, tag):
    qkv, fb, qb, kb, vb, o, l = saved
    T = qkv.shape[0]
    S = T // B
    Wd = len(DIL_PAIRS) * DIL_GW
    nblk = Wd // LANES
    gw = DIL_GW
    rows = [(o, gw, 0), (o, gw, 1), (o, gw, 2), (l, gw, 0), (l, gw, 1), (l, gw, 2)]
    g = _ew_bwd(_f_dil_merge, rows, [], [], [dy], [True] * 6, name=f"{tag}_merge_bwd")
    do = _dil_to_blocks(jnp.concatenate(g[:3], axis=1), B)
    dl = _dil_to_blocks(jnp.concatenate(g[3:], axis=1), B)
    dq, dkp, dkc, dvp, dvc = _dil_attn_bwd(qb, kb, vb, do, dl, B, S, f"{tag}_attn_bwd")
    nxt = lambda t: jnp.concatenate([t[DIL_BLK:], jnp.zeros((DIL_BLK, DIL_DIM), t.dtype)], axis=0)
    dqn = _dil_from_blocks(dq, B)
    dkn = _dil_from_blocks(dkc + nxt(dkp), B)
    dv = _dil_from_blocks(dvc + nxt(dvp), B)
    dq_raw, dqw = _ew_bwd(functools.partial(_f_qknorm, DIL_DIM ** -0.5), [(qkv, LANES, fb)], [], [q_norm],
                          [(dqn, LANES, 0)], [True, True], ncb=nblk, name=f"{tag}_qnorm_bwd")
    dk_raw, dkw = _ew_bwd(functools.partial(_f_qknorm, 1.0), [(qkv, LANES, fb + nblk)], [], [k_norm],
                          [(dkn, LANES, 0)], [True, True], ncb=nblk, name=f"{tag}_knorm_bwd")
    return jnp.concatenate([dq_raw, dk_raw, dv], axis=1), dqw, dkw


MESH_ID = pl.DeviceIdType.MESH
ANY = pl.BlockSpec(memory_space=pl.ANY)


def _my_place():
    return lax.axis_index("x"), lax.axis_index("y"), lax.axis_index("c")


def _flat_index(px, py, pc):
    return 4 * px + 2 * py + pc


def _all_gather(x, name):
    R_, C = x.shape

    def body(x_ref, out_ref, send_sems, recv_sems, local_sem):
        mx, my, mc = _my_place()
        me, sibling = (mx, my, mc), (mx, my, 1 - mc)
        chips = [(1 - mx, my), (mx, 1 - my), (1 - mx, 1 - my)]

        def rows(p):
            return out_ref.at[_flat_index(*p)]

        def copy(k, block, to, src=None):
            return pltpu.make_async_remote_copy(
                src_ref=rows(block) if src is None else src, dst_ref=rows(block),
                send_sem=send_sems.at[k], recv_sem=recv_sems.at[k], device_id=to, device_id_type=MESH_ID)

        mine = pltpu.make_async_copy(x_ref, rows(me), local_sem)
        mine.start()
        first = [copy(0, me, sibling, src=x_ref)]
        first += [copy(1 + j, me, (*chip, mc), src=x_ref) for j, chip in enumerate(chips)]
        for cp in first:
            cp.start()
        passed = [copy(4 + j, (*chip, mc), sibling) for j, chip in enumerate(chips)]
        for j, chip in enumerate(chips):
            copy(1 + j, (*chip, mc), me).wait_recv()
            passed[j].start()
        copy(0, sibling, me).wait_recv()
        for j, chip in enumerate(chips):
            copy(4 + j, (*chip, 1 - mc), me).wait_recv()
        for cp in first + passed:
            cp.wait_send()
        mine.wait()

    return pl.pallas_call(
        body, name=name, out_shape=jax.ShapeDtypeStruct((N_DEV, R_, C), x.dtype),
        in_specs=[ANY], out_specs=ANY,
        scratch_shapes=[pltpu.SemaphoreType.DMA((7,)), pltpu.SemaphoreType.DMA((7,)), pltpu.SemaphoreType.DMA],
        compiler_params=pltpu.CompilerParams(has_side_effects=True),
    )(x)


def _exchange_sibling(x, name):
    nchip, _, R_, C = x.shape

    def body(x_ref, out_ref, send_sems, recv_sems):
        mx, my, mc = _my_place()
        sibling = (mx, my, 1 - mc)
        copies = [pltpu.make_async_remote_copy(
            src_ref=x_ref.at[k, 1 - mc], dst_ref=out_ref.at[k], send_sem=send_sems.at[k], recv_sem=recv_sems.at[k],
            device_id=sibling, device_id_type=MESH_ID) for k in range(nchip)]
        for cp in copies:
            cp.start()
        for cp in copies:
            cp.wait_recv()
        for cp in copies:
            cp.wait_send()

    return pl.pallas_call(
        body, name=name, out_shape=jax.ShapeDtypeStruct((nchip, R_, C), x.dtype), in_specs=[ANY], out_specs=ANY,
        scratch_shapes=[pltpu.SemaphoreType.DMA((nchip,)), pltpu.SemaphoreType.DMA((nchip,))],
        compiler_params=pltpu.CompilerParams(has_side_effects=True),
    )(x)


def _exchange_chips(x, name):
    nchip, R_, C = x.shape

    def body(x_ref, out_ref, send_sems, recv_sems, local_sem):
        mx, my, mc = _my_place()
        mk = 2 * mx + my
        chips = [(1 - mx, my), (mx, 1 - my), (1 - mx, 1 - my)]
        mine = pltpu.make_async_copy(x_ref.at[mk], out_ref.at[mk], local_sem)
        mine.start()
        copies = [pltpu.make_async_remote_copy(
            src_ref=x_ref.at[2 * px + py], dst_ref=out_ref.at[mk], send_sem=send_sems.at[j], recv_sem=recv_sems.at[j],
            device_id=(px, py, mc), device_id_type=MESH_ID) for j, (px, py) in enumerate(chips)]
        for cp in copies:
            cp.start()
        for j, (px, py) in enumerate(chips):
            pltpu.make_async_remote_copy(
                src_ref=x_ref.at[mk], dst_ref=out_ref.at[2 * px + py], send_sem=send_sems.at[j],
                recv_sem=recv_sems.at[j], device_id=(px, py, mc), device_id_type=MESH_ID).wait_recv()
        for cp in copies:
            cp.wait_send()
        mine.wait()

    return pl.pallas_call(
        body, name=name, out_shape=jax.ShapeDtypeStruct(x.shape, x.dtype), in_specs=[ANY], out_specs=ANY,
        scratch_shapes=[pltpu.SemaphoreType.DMA((3,)), pltpu.SemaphoreType.DMA((3,)), pltpu.SemaphoreType.DMA],
        compiler_params=pltpu.CompilerParams(has_side_effects=True),
    )(x)


BLOCK_BYTES = 2 * 1024 * 1024


def _rows_for(R_, row_bytes):
    for t in (2048, 1024, 512, 256, 128, 64, 32, 16, 8):
        if R_ % t == 0 and t * row_bytes <= BLOCK_BYTES:
            return t
    return R_


def _add_pieces(a, b, name):
    n, R_, C = a.shape
    tr = _rows_for(R_, C * a.dtype.itemsize)

    def body(a_ref, b_ref, o_ref):
        o_ref[...] = (a_ref[...].astype(F32) + b_ref[...].astype(F32)).astype(o_ref.dtype)

    spec = pl.BlockSpec((1, tr, C), lambda k, i: (k, i, 0))
    return pl.pallas_call(
        body, name=name, out_shape=jax.ShapeDtypeStruct(a.shape, a.dtype), grid=(n, R_ // tr),
        in_specs=[spec, spec], out_specs=spec, compiler_params=_cparams(("parallel", "parallel")),
    )(a, b)


def _reduce_to_owner(buf, mc, tag):
    _, R_, C = buf.shape
    buf4 = buf.reshape(N_DEV // 2, 2, R_, C)
    from_sibling = _exchange_sibling(buf4, f"scatter_sib_{tag}")
    mine = lax.dynamic_index_in_dim(buf4, mc, axis=1, keepdims=False)
    chip_sums = _add_pieces(mine, from_sibling, f"add_sib_{tag}")
    return _sum0(_exchange_chips(chip_sums, f"scatter_chips_{tag}"), f"sum_grads_{tag}")


def _sum0(x, name):
    n, R_, C = x.shape
    tr = _rows_for(R_, n * C * x.dtype.itemsize)

    def body(x_ref, o_ref):
        acc = x_ref[0].astype(F32)
        for k in range(1, n):
            acc = acc + x_ref[k].astype(F32)
        o_ref[...] = acc

    return pl.pallas_call(
        body, name=name, out_shape=jax.ShapeDtypeStruct((R_, C), F32), grid=(R_ // tr,),
        in_specs=[pl.BlockSpec((n, tr, C), lambda i: (0, i, 0))], out_specs=pl.BlockSpec((tr, C), lambda i: (i, 0)),
        compiler_params=_cparams(("parallel",)),
    )(x)


PACK_ROWS = 256


def _pack(arrs, dtype, width):
    flat = jnp.concatenate([a.astype(dtype).reshape(-1) for a in arrs])
    quantum = width * PACK_ROWS
    pad = (-flat.shape[0]) % quantum
    if pad:
        flat = jnp.concatenate([flat, jnp.zeros((pad,), dtype)])
    return flat.reshape(-1, width)


def _unpack(flat, shapes):
    out, off = [], 0
    for s in shapes:
        n = int(np.prod(s))
        out.append(flat[..., off:off + n].reshape(flat.shape[:-1] + tuple(s)))
        off += n
    return out


def _ada_fwd(c_all, ada_w, bias, name):
    M, D = c_all.shape
    n = ada_w.shape[1]
    tn = _pick(n, (768, 512, 256, 128))

    def body(c_ref, w_ref, b_ref, o_ref):
        c_ = c_ref[...]
        a = (c_ * jax.nn.sigmoid(c_)).astype(BF16)
        o_ref[...] = jnp.dot(a, w_ref[...].astype(BF16), preferred_element_type=F32) + b_ref[...]

    return pl.pallas_call(
        body, name=name, out_shape=jax.ShapeDtypeStruct((M, n), F32), grid=(n // tn,),
        in_specs=[pl.BlockSpec((M, D), lambda j: (0, 0)), pl.BlockSpec((D, tn), lambda j: (0, j)),
                  pl.BlockSpec((1, tn), lambda j: (0, j))],
        out_specs=pl.BlockSpec((M, tn), lambda j: (0, j)), compiler_params=_cparams(("parallel",)),
    )(c_all, ada_w, bias)


def _ada_bwd(c_all, dmod, name):
    M, D = c_all.shape
    n = dmod.shape[1]
    tn = _pick(n, (768, 512, 256, 128))

    def body(c_ref, d_ref, o_ref):
        c_ = c_ref[...]
        a = (c_ * jax.nn.sigmoid(c_)).astype(BF16)
        o_ref[...] = lax.dot_general(a, d_ref[...].astype(BF16), (((0,), (0,)), ((), ())), preferred_element_type=F32)

    return pl.pallas_call(
        body, name=name, out_shape=jax.ShapeDtypeStruct((D, n), F32), grid=(n // tn,),
        in_specs=[pl.BlockSpec((M, D), lambda j: (0, 0)), pl.BlockSpec((M, tn), lambda j: (0, j))],
        out_specs=pl.BlockSpec((D, tn), lambda j: (0, j)), compiler_params=_cparams(("parallel",)),
    )(c_all, dmod)


def _loss_head(y, target, name, tr=256):
    T, D = y.shape

    def body(y_ref, t_ref, l_ref, d_ref):
        e = y_ref[...] - t_ref[...]
        d_ref[...] = e * (1.0 / D)
        part = jnp.sum(jnp.sum(e * e, axis=1, keepdims=True), axis=0, keepdims=True) * (0.5 / D)

        @pl.when(pl.program_id(0) == 0)
        def _():
            l_ref[...] = jnp.zeros_like(l_ref)

        l_ref[...] += jnp.broadcast_to(part, l_ref.shape)

    row = pl.BlockSpec((tr, D), lambda i: (i, 0))
    return pl.pallas_call(
        body, name=name, out_shape=[jax.ShapeDtypeStruct((1, LANES), F32), jax.ShapeDtypeStruct((T, D), F32)],
        grid=(T // tr,), in_specs=[row, row], out_specs=[pl.BlockSpec((1, LANES), lambda i: (0, 0)), row],
        compiler_params=_cparams(("arbitrary",)),
    )(y, target)


def _adamw(w, g, m, v, name):
    shape = w.shape
    C = shape[-1]
    R_ = int(np.prod(shape[:-1]))
    w2, g2, m2, v2 = [a.reshape(R_, C) for a in (w, g, m, v)]
    tr = _pick(R_, (256, 128, 64, 32, 16, 8)) if R_ > 8 else R_
    c1 = 1.0 / (1.0 - ADAM_B1 ** ADAM_STEP)
    c2 = 1.0 / (1.0 - ADAM_B2 ** ADAM_STEP)

    def body(w_ref, g_ref, m_ref, v_ref, d_ref, nm_ref, nv_ref):
        g_ = g_ref[...]
        nm = ADAM_B1 * m_ref[...] + (1.0 - ADAM_B1) * g_
        nv = ADAM_B2 * v_ref[...] + (1.0 - ADAM_B2) * (g_ * g_)
        d_ref[...] = -ADAM_LR * ((nm * c1) / (jnp.sqrt(nv * c2) + ADAM_EPS) + ADAM_WD * w_ref[...])
        nm_ref[...] = nm
        nv_ref[...] = nv

    spec = pl.BlockSpec((tr, C), lambda i: (i, 0))
    outs = pl.pallas_call(
        body, name=name, out_shape=[jax.ShapeDtypeStruct((R_, C), F32)] * 3, grid=(R_ // tr,),
        in_specs=[spec] * 4, out_specs=[spec] * 3, compiler_params=_cparams(("parallel",)),
    )(w2, g2, m2, v2)
    return [o.reshape(shape) for o in outs]


GDN_W = GDN_HEADS * GDN_DIM
IN_SPLITS = (3 * GDN_W, GDN_W, GDN_HEADS, GDN_HEADS, 768, 3 * 768, None)
IN_PAD_GATES = LANES - 2 * GDN_HEADS


def _f_merge(pa, pb, pc, ga, gb, gc):
    return jax.nn.sigmoid(ga) * pa + jax.nn.sigmoid(gb) * pb + jax.nn.sigmoid(gc) * pc


def _f_id_rmsmod(x, sh, sc, g):
    return x, _rms(x, g) * (1.0 + sc) + sh


def _in_layout(D):
    widths = [3 * D, 3 * GDN_W, GDN_W, LANES, 768, 3 * 768]
    offs = np.concatenate([[0], np.cumsum(widths)]).tolist()
    total = -(-offs[-1] // 768) * 768
    return offs, total


def _pad_w_in(w_in):
    D = w_in.shape[0]
    offs, total = _in_layout(D)
    cut = 4 * GDN_W + 2 * GDN_HEADS
    ng = w_in.shape[1] - 3 * D
    return jnp.concatenate([w_in[:, ng:], w_in[:, :cut], jnp.zeros((D, IN_PAD_GATES), w_in.dtype), w_in[:, cut:ng],
                            jnp.zeros((D, total - offs[-1]), w_in.dtype)], axis=1)


def _unpad_w_in(d):
    D = d.shape[0]
    offs, _ = _in_layout(D)
    cut = 4 * GDN_W + 2 * GDN_HEADS
    g0 = 3 * D
    return jnp.concatenate([d[:, g0:g0 + cut], d[:, g0 + cut + IN_PAD_GATES:offs[-1]], d[:, :g0]], axis=1)


def _layer_fwd(x0, mod, W, sm, B, tag):
    T, D = x0.shape
    S = T // B
    sh1, sc1, g1, sh2, sc2, g2, sh3, sc3, g3 = mod
    n1, nm, n3 = sm["norm_ffn1"][None], sm["norm_mix"][None], sm["norm_ffn2"][None]
    (h1,) = _ew_fwd(_f_rmsmod, [x0], [sh1, sc1], [n1], [(D, BF16, False)], seq=S, name=f"{tag}_norm1")
    y1, sv1 = _ffn_fwd(h1, W["ffn1_w1"], W["ffn1_w3"], W["ffn1_w2"], f"{tag}_ffn1")
    x1, h2 = _ew_fwd(functools.partial(_f_res_rmsmod, FFN_RES), [x0, y1], [g1, sh2, sc2], [nm],
                     [(D, F32, False), (D, BF16, False)], seq=S, name=f"{tag}_res1")
    P = _mm(h2, W["w_in"], name=f"{tag}_in")
    offs, _ = _in_layout(D)
    qkv_a, z, ba, u, qkv_c = [(P, off // LANES) for off in offs[1:6]]
    lane8 = lambda v: jnp.pad(v, (GDN_HEADS, LANES - 2 * GDN_HEADS))[None]
    gate_params = (lane8(sm["gdn_a_log"]), lane8(sm["gdn_dt_bias"]))
    out_norm = sm["gdn_out_norm"][None]
    y_a, sva = _gdn_fwd(qkv_a, z, ba, sm["gdn_conv"], gate_params, out_norm, B, f"{tag}_gdn")
    s5p = {k[3:]: v for k, v in sm.items() if k.startswith("s5_")}
    y_b, svb = _s5_fwd(u, s5p, W["s5_glu_w"], B, f"{tag}_s5")
    qn2, kn2 = jnp.tile(sm["dil_q_norm"], 2)[None], jnp.tile(sm["dil_k_norm"], 2)[None]
    y_c, svc = _dil_fwd(qkv_c, qn2, kn2, B, f"{tag}_dil")
    pa = _mm(y_a, W["w_branch_a"], name=f"{tag}_pa")
    pb = _mm(y_b, W["w_branch_b"], name=f"{tag}_pb")
    pc = _mm(y_c, W["w_branch_c"], name=f"{tag}_pc")
    mrows = [pa, pb, pc, (P, D, 0), (P, D, 1), (P, D, 2)]
    (merged,) = _ew_fwd(_f_merge, mrows, [], [], [(D, BF16, False)], tr=128, name=f"{tag}_merge")
    mo = _mm(merged, W["w_out"], name=f"{tag}_out")
    x2, h3 = _ew_fwd(functools.partial(_f_res_rmsmod, 1.0), [x1, mo], [g2, sh3, sc3], [n3],
                     [(D, F32, False), (D, BF16, False)], seq=S, name=f"{tag}_res2")
    y3, sv3 = _ffn_fwd(h3, W["ffn2_w1"], W["ffn2_w3"], W["ffn2_w2"], f"{tag}_ffn2")
    (x3,) = _ew_fwd(functools.partial(_f_res, FFN_RES), [x2, y3], [g3], [], [(D, F32, False)], seq=S,
                    name=f"{tag}_res3")
    saved = dict(x0=x0, x1=x1, x2=x2, y1=y1, y3=y3, mo=mo, h2=h2, sv1=sv1, sv3=sv3, sva=sva, svb=svb, svc=svc,
                 y_a=y_a, y_b=y_b, y_c=y_c, mrows=mrows, merged=merged, gate_params=gate_params, out_norm=out_norm,
                 s5p=s5p, qn2=qn2, kn2=kn2)
    return x3, saved


def _layer_bwd(dx3, sv, mod, W, sm, B, tag):
    T, D = dx3.shape
    S = T // B
    sh1, sc1, g1, sh2, sc2, g2, sh3, sc3, g3 = mod
    n1, nm, n3 = sm["norm_ffn1"][None], sm["norm_mix"][None], sm["norm_ffn2"][None]
    big, small = {}, {}
    dx2, dy3, dg3 = _ew_bwd(functools.partial(_f_res, FFN_RES), [sv["x2"], sv["y3"]], [g3], [], [dx3], [True] * 3,
                            seq=S, row_dtypes=[F32, BF16], name=f"{tag}_res3_bwd")
    dh3, big["ffn2_w1"], big["ffn2_w3"], big["ffn2_w2"] = _ffn_bwd(
        dy3, sv["sv3"], W["ffn2_w1"], W["ffn2_w3"], W["ffn2_w2"], f"{tag}_ffn2")
    dx1, dmo, dg2, dsh3, dsc3, dn3 = _ew_bwd(
        functools.partial(_f_res_rmsmod, 1.0), [sv["x1"], sv["mo"]], [g2, sh3, sc3], [n3], [dx2, dh3], [True] * 6,
        seq=S, row_dtypes=[F32, BF16], name=f"{tag}_res2_bwd")
    big["w_out"] = _mm(sv["merged"], dmo, ta=True, out_dtype=BF16, name=f"{tag}_out_dw")
    dmerged = _mm(dmo, W["w_out"], tb=True, name=f"{tag}_out_dx")
    dpa, dpb, dpc, dga, dgb, dgc = _ew_bwd(_f_merge, sv["mrows"], [], [], [dmerged], [True] * 6, tr=128,
                                           row_dtypes=[BF16] * 6, name=f"{tag}_merge_bwd")
    big["w_branch_a"] = _mm(sv["y_a"], dpa, ta=True, out_dtype=BF16, name=f"{tag}_pa_dw")
    big["w_branch_b"] = _mm(sv["y_b"], dpb, ta=True, out_dtype=BF16, name=f"{tag}_pb_dw")
    big["w_branch_c"] = _mm(sv["y_c"], dpc, ta=True, out_dtype=BF16, name=f"{tag}_pc_dw")
    dy_a = _mm(dpa, W["w_branch_a"], tb=True, name=f"{tag}_pa_dx")
    dy_b = _mm(dpb, W["w_branch_b"], tb=True, name=f"{tag}_pb_dx")
    dy_c = _mm(dpc, W["w_branch_c"], tb=True, name=f"{tag}_pc_dx")
    dqkv_a, dz, dba, gdn_small = _gdn_bwd(dy_a, sv["sva"], sm["gdn_conv"], sv["gate_params"], sv["out_norm"], B,
                                          f"{tag}_gdn")
    du, big["s5_glu_w"], s5_small = _s5_bwd(dy_b, sv["svb"], sv["s5p"], W["s5_glu_w"], B, f"{tag}_s5")
    dqkv_c, dqw, dkw = _dil_bwd(dy_c, sv["svc"], sv["qn2"], sv["kn2"], B, f"{tag}_dil")
    offs, total = _in_layout(D)
    dP = jnp.concatenate([t.astype(BF16) for t in (dga, dgb, dgc, dqkv_a, dz, dba, du, dqkv_c)]
                         + [jnp.zeros((T, total - offs[-1]), BF16)], axis=1)
    big["w_in"] = _mm(sv["h2"], dP, ta=True, out_dtype=BF16, name=f"{tag}_in_dw")
    dh2 = _mm(dP, W["w_in"], tb=True, name=f"{tag}_in_dx")
    dx0a, dy1, dg1, dsh2, dsc2, dnm = _ew_bwd(
        functools.partial(_f_res_rmsmod, FFN_RES), [sv["x0"], sv["y1"]], [g1, sh2, sc2], [nm], [dx1, dh2], [True] * 6,
        seq=S, row_dtypes=[F32, BF16], name=f"{tag}_res1_bwd")
    dh1, big["ffn1_w1"], big["ffn1_w3"], big["ffn1_w2"] = _ffn_bwd(
        dy1, sv["sv1"], W["ffn1_w1"], W["ffn1_w3"], W["ffn1_w2"], f"{tag}_ffn1")
    dx0, dsh1, dsc1, dn1 = _ew_bwd(_f_id_rmsmod, [sv["x0"]], [sh1, sc1], [n1], [dx0a, dh1], [True] * 4, seq=S,
                                   name=f"{tag}_norm1_bwd")
    half = DIL_DIM
    small.update(norm_ffn1=dn1[0], norm_mix=dnm[0], norm_ffn2=dn3[0], gdn_conv=gdn_small["conv"],
                 gdn_a_log=gdn_small["a_log"][0, GDN_HEADS:2 * GDN_HEADS],
                 gdn_dt_bias=gdn_small["dt_bias"][0, GDN_HEADS:2 * GDN_HEADS], gdn_out_norm=gdn_small["out_norm"][0],
                 dil_q_norm=dqw[0, :half] + dqw[0, half:], dil_k_norm=dkw[0, :half] + dkw[0, half:])
    small.update({"s5_" + k: v for k, v in s5_small.items()})
    dmod = [dsh1, dsc1, dg1, dsh2, dsc2, dg2, dsh3, dsc3, dg3]
    return dx0, big, small, dmod


WEIGHTS = ['ada_w', 'ada_b', 'norm_ffn1', 'ffn1_w1', 'ffn1_w3', 'ffn1_w2', 'norm_mix', 'w_in', 'gdn_conv', 'gdn_a_log',
           'gdn_dt_bias', 'gdn_out_norm', 's5_a_re', 's5_a_im', 's5_b_re', 's5_b_im', 's5_c_re', 's5_c_im', 's5_d',
           's5_log_step', 's5_glu_w', 's5_glu_b', 'dil_q_norm', 'dil_k_norm', 'w_branch_a', 'w_branch_b', 'w_branch_c',
           'w_out', 'norm_ffn2', 'ffn2_w1', 'ffn2_w3', 'ffn2_w2']
BIG = dict(ffn1_w1=True, ffn1_w3=True, ffn1_w2=False, w_in=True, s5_glu_w=True, w_branch_a=True, w_branch_b=True,
           w_branch_c=True, w_out=False, ffn2_w1=True, ffn2_w3=True, ffn2_w2=False)
FFN_W = ("ffn1_w1", "ffn1_w3", "ffn1_w2", "ffn2_w1", "ffn2_w3", "ffn2_w2")
SMALL = ['norm_ffn1', 'norm_mix', 'norm_ffn2', 'gdn_conv', 'gdn_a_log', 'gdn_dt_bias', 'gdn_out_norm', 's5_a_re',
         's5_a_im', 's5_b_re', 's5_b_im', 's5_c_re', 's5_c_im', 's5_d', 's5_log_step', 's5_glu_b', 'dil_q_norm',
         'dil_k_norm']


def _full_from_shards(g, cols):
    n, r, c = g.shape
    return jnp.transpose(g, (1, 0, 2)).reshape(r, n * c) if cols else g.reshape(n * r, c)


def _shards_from_full(w, cols):
    if cols:
        r, nc = w.shape
        return jnp.transpose(w.reshape(r, N_DEV, nc // N_DEV), (1, 0, 2))
    nr, c = w.shape
    return w.reshape(N_DEV, nr // N_DEV, c)


def kernel(x, c, ada_w, ada_b, norm_ffn1, ffn1_w1, ffn1_w3, ffn1_w2, norm_mix, w_in, gdn_conv, gdn_a_log, gdn_dt_bias, gdn_out_norm, s5_a_re, s5_a_im, s5_b_re, s5_b_im, s5_c_re, s5_c_im, s5_d, s5_log_step, s5_glu_w, s5_glu_b, dil_q_norm, dil_k_norm, w_branch_a, w_branch_b, w_branch_c, w_out, norm_ffn2, ffn2_w1, ffn2_w3, ffn2_w2, loss_target, m_ada_w, m_ada_b, m_norm_ffn1, m_ffn1_w1, m_ffn1_w3, m_ffn1_w2, m_norm_mix, m_w_in, m_gdn_conv, m_gdn_a_log, m_gdn_dt_bias, m_gdn_out_norm, m_s5_a_re, m_s5_a_im, m_s5_b_re, m_s5_b_im, m_s5_c_re, m_s5_c_im, m_s5_d, m_s5_log_step, m_s5_glu_w, m_s5_glu_b, m_dil_q_norm, m_dil_k_norm, m_w_branch_a, m_w_branch_b, m_w_branch_c, m_w_out, m_norm_ffn2, m_ffn2_w1, m_ffn2_w3, m_ffn2_w2, v_ada_w, v_ada_b, v_norm_ffn1, v_ffn1_w1, v_ffn1_w3, v_ffn1_w2, v_norm_mix, v_w_in, v_gdn_conv, v_gdn_a_log, v_gdn_dt_bias, v_gdn_out_norm, v_s5_a_re, v_s5_a_im, v_s5_b_re, v_s5_b_im, v_s5_c_re, v_s5_c_im, v_s5_d, v_s5_log_step, v_s5_glu_w, v_s5_glu_b, v_dil_q_norm, v_dil_k_norm, v_w_branch_a, v_w_branch_b, v_w_branch_c, v_w_out, v_norm_ffn2, v_ffn2_w1, v_ffn2_w3, v_ffn2_w2):
    env = dict(locals())
    w = {n: env[n] for n in WEIGHTS}
    m = {n: env["m_" + n] for n in WEIGHTS}
    v = {n: env["v_" + n] for n in WEIGHTS}
    L = ada_w.shape[0]
    B, S, D = x.shape
    T = B * S
    me = _flat_index(*_my_place())

    big_keys = [(n, l) for l in range(L) for n in BIG]
    groups = {}
    for n, l in big_keys:
        r, cc = w[n].shape[1:]
        groups.setdefault((BIG[n], r if BIG[n] else cc), []).append((n, l))
    shards = {}
    for (cols, dim), keys in groups.items():
        if cols and dim == D:
            for n, l in keys:
                shards[(n, l)] = _all_gather(w[n][l].astype(BF16), f"gather_{n}_{l}")
            continue
        buf = jnp.concatenate([w[n][l].astype(BF16) for n, l in keys], axis=1 if cols else 0)
        got = _all_gather(buf, f"gather_weights_{'c' if cols else 'r'}{dim}")
        off = 0
        for n, l in keys:
            k = w[n].shape[2] if cols else w[n].shape[1]
            shards[(n, l)] = got[:, :, off:off + k] if cols else got[:, off:off + k, :]
            off += k
    small_in = _pack([jnp.pad(c, ((0, SUBLANES - B), (0, 0))), gdn_conv], F32, LANES)
    c_g, conv_g = _unpack(_all_gather(small_in, "gather_cond").reshape(N_DEV, -1),
                          [(SUBLANES, D), gdn_conv.shape])
    c_all = c_g[:, :B].reshape(N_DEV * B, D)
    conv_full = jnp.transpose(conv_g, (1, 2, 0, 3)).reshape(L, GDN_CONV, -1)
    Ws = []
    for l in range(L):
        Wl = {n: shards[(n, l)] for n in FFN_W}
        Wl.update({n: _full_from_shards(shards[(n, l)], BIG[n])
                   for n in ("s5_glu_w", "w_branch_a", "w_branch_b", "w_branch_c", "w_out")})
        Wl["w_in"] = _pad_w_in(_full_from_shards(shards[("w_in", l)], True))
        Ws.append(Wl)
    sms = [dict({n: w[n][l] for n in SMALL}, gdn_conv=conv_full[l]) for l in range(L)]

    n_ada = ada_w.shape[2]
    bias = lax.dynamic_slice(ada_b, (0, me * n_ada), (L, n_ada))
    mod_cols = jnp.concatenate([_ada_fwd(c_all, ada_w[l], bias[l][None], f"ada{l}") for l in range(L)], axis=0)
    mod_g = _all_gather(mod_cols, "gather_mod").reshape(N_DEV, L, N_DEV * B, n_ada)
    mod_mine = lax.dynamic_slice(mod_g, (0, 0, me * B, 0), (N_DEV, L, B, n_ada))
    mod_mine = jnp.transpose(mod_mine, (1, 2, 0, 3)).reshape(L, B, N_DEV * n_ada)
    mods = [[mod_mine[l][:, None, k * D:(k + 1) * D] for k in range(9)] for l in range(L)]

    h = x.reshape(T, D)
    saved = []
    for l in range(L):
        h, sv = _layer_fwd(h, mods[l], Ws[l], sms[l], B, f"l{l}")
        saved.append(sv)
    loss_row, dh = _loss_head(h, loss_target.reshape(T, D), "loss")
    loss = lax.psum(loss_row[0, 0], ("x", "y", "c"))
    bigs, smalls, dmods = [None] * L, [None] * L, [None] * L
    for l in reversed(range(L)):
        dh, bigs[l], smalls[l], dmods[l] = _layer_bwd(dh, saved[l], mods[l], Ws[l], sms[l], B, f"l{l}")
    grad_x = dh.reshape(B, S, D)

    def grad_pieces(n, l):
        if n in FFN_W:
            return bigs[l][n]
        full = _unpad_w_in(bigs[l]["w_in"]) if n == "w_in" else bigs[l][n]
        return _shards_from_full(full, BIG[n])

    g = dict()
    mc = lax.axis_index("c")
    for (cols, dim), keys in groups.items():
        tag = f"{'c' if cols else 'r'}{dim}"
        if cols and dim == D:
            for n, l in keys:
                g.setdefault(n, [None] * L)[l] = _reduce_to_owner(grad_pieces(n, l), mc, f"{n}_{l}")
            continue
        buf = jnp.concatenate([grad_pieces(n, l) for n, l in keys], axis=2 if cols else 1)
        summed = _reduce_to_owner(buf, mc, tag)
        off = 0
        for n, l in keys:
            k = w[n].shape[2] if cols else w[n].shape[1]
            g.setdefault(n, [None] * L)[l] = summed[:, off:off + k] if cols else summed[off:off + k, :]
            off += k
    g = {n: jnp.stack(ts) for n, ts in g.items()}

    small_keys = [(n, l) for l in range(L) for n in SMALL]
    small_flat = _pack([smalls[l][n] for n, l in small_keys], F32, LANES)
    small_sum = _sum0(_all_gather(small_flat, "gather_small_grads"), "sum_small_grads")
    small_full = {}
    for (n, l), t in zip(small_keys, _unpack(small_sum.reshape(-1), [smalls[l][n].shape for n, l in small_keys])):
        small_full.setdefault(n, [None] * L)[l] = t
    for n, ts in small_full.items():
        g[n] = jnp.stack(ts)
    n_conv = gdn_conv.shape[2]
    g["gdn_conv"] = lax.dynamic_slice(g["gdn_conv"], (0, 0, me * n_conv), (L, GDN_CONV, n_conv))

    dmod_mine = jnp.stack([jnp.concatenate([d[:, 0] for d in dmods[l]], axis=1) for l in range(L)])
    dmod_in = jnp.pad(dmod_mine.reshape(L * B, -1), ((0, SUBLANES - L * B), (0, 0)))
    dmod_g = _all_gather(dmod_in, "gather_dmod")[:, :L * B].reshape(N_DEV, L, B, -1)
    dmod_all = jnp.transpose(dmod_g, (1, 0, 2, 3)).reshape(L, N_DEV * B, -1)
    g["ada_b"] = _sum0(dmod_all.reshape(L, N_DEV * B, -1, LANES).transpose(1, 0, 2, 3).reshape(N_DEV * B, -1, LANES),
                       "sum_ada_b").reshape(L, -1)
    dmod_cols = lax.dynamic_slice(dmod_all, (0, 0, me * n_ada), (L, N_DEV * B, n_ada))
    g["ada_w"] = jnp.stack([_ada_bwd(c_all, dmod_cols[l], f"ada{l}_bwd") for l in range(L)])

    upd = {n: _adamw(w[n], g[n], m[n], v[n], f"adamw_{n}") for n in WEIGHTS}
    return (loss, grad_x, *[g[n] for n in WEIGHTS], *[upd[n][0] for n in WEIGHTS],
            *[upd[n][1] for n in WEIGHTS], *[upd[n][2] for n in WEIGHTS])
```

```python
import functools
import math

import jax
import jax.numpy as jnp
import numpy as np
from jax import lax
from jax.experimental import pallas as pl
from jax.experimental.pallas import tpu as pltpu

F32 = jnp.float32
BF16 = jnp.bfloat16

LANES = 128
SUBLANES = 8
VMEM_LIMIT = 56 * 1024 * 1024

N_DEV = 8
EPS = 1e-6
FFN_RES = 0.5
GDN_HEADS = 8
GDN_DIM = 128
GDN_CONV = 4
GDN_CHUNK = 128
S5_GROUP = 16
S5_STATE = 64
S5_MAX_RE = -1e-4
S5_TCHUNK = 512
DIL_PAIRS = ((128, 1), (512, 4), (2048, 16))
DIL_HPG = 4
DIL_DIM = 64
DIL_BLK = 128
ALIBI_MAX = 8.0
ADAM_LR, ADAM_B1, ADAM_B2, ADAM_EPS, ADAM_WD, ADAM_STEP = 0.001, 0.9, 0.999, 1e-08, 0.01, 10

HI = lax.Precision.HIGHEST
HI3 = lax.Precision.HIGH


def _cparams(sem=None, **kw):
    return pltpu.CompilerParams(dimension_semantics=sem, vmem_limit_bytes=VMEM_LIMIT, **kw)


def _pick(n, cands):
    for c in cands:
        if n % c == 0:
            return c
    return n


def _mm(a, b, *, ta=False, tb=False, out_dtype=F32, name):
    M, K = (a.shape[1], a.shape[0]) if ta else a.shape
    N = b.shape[0] if tb else b.shape[1]
    assert (b.shape[1] if tb else b.shape[0]) == K, (a.shape, b.shape, ta, tb)
    tm = _pick(M, (1024, 512, 256, 128))
    tn = _pick(N, (1024, 768, 512, 384, 256, 128))
    tk = _pick(K, (2048, 1536, 1408, 1024, 768, 512, 256, 128))
    nk = K // tk

    def body(a_ref, b_ref, o_ref, acc_ref):
        k = pl.program_id(2)

        @pl.when(k == 0)
        def _():
            acc_ref[...] = jnp.zeros_like(acc_ref)

        dn = (((0 if ta else 1,), (1 if tb else 0,)), ((), ()))
        acc_ref[...] += lax.dot_general(a_ref[...], b_ref[...], dn, preferred_element_type=F32)

        @pl.when(k == nk - 1)
        def _():
            o_ref[...] = acc_ref[...].astype(o_ref.dtype)

    a_spec = pl.BlockSpec((tk, tm), lambda i, j, k: (k, i)) if ta else pl.BlockSpec((tm, tk), lambda i, j, k: (i, k))
    b_spec = pl.BlockSpec((tn, tk), lambda i, j, k: (j, k)) if tb else pl.BlockSpec((tk, tn), lambda i, j, k: (k, j))
    return pl.pallas_call(
        body, name=name,
        out_shape=jax.ShapeDtypeStruct((M, N), out_dtype),
        grid=(M // tm, N // tn, nk),
        in_specs=[a_spec, b_spec],
        out_specs=pl.BlockSpec((tm, tn), lambda i, j, k: (i, j)),
        scratch_shapes=[pltpu.VMEM((tm, tn), F32)],
        compiler_params=_cparams(("parallel", "parallel", "arbitrary")),
    )(a, b)


def _norm_arg(a):
    return a if isinstance(a, tuple) else (a, None, 0)


def _ew_specs(rows, exs, ws, tr, tpe):
    specs = []
    for arr, cw, off in rows:
        if cw is None:
            specs.append(pl.BlockSpec((tr, arr.shape[1]), lambda j, i: (i, 0)))
        else:
            specs.append(pl.BlockSpec((tr, cw), lambda j, i, off=off: (i, j + off)))
    for arr, cw, off in exs:
        if cw is None:
            specs.append(pl.BlockSpec((1, 1, arr.shape[2]), lambda j, i: (i // tpe, 0, 0)))
        else:
            specs.append(pl.BlockSpec((1, 1, cw), lambda j, i, off=off: (i // tpe, 0, j + off)))
    for arr, cw, off in ws:
        if cw is None:
            specs.append(pl.BlockSpec((1, arr.shape[1]), lambda j, i: (0, 0)))
        else:
            specs.append(pl.BlockSpec((1, cw), lambda j, i, off=off: (0, j + off)))
    return specs


def _ew_fwd(fn, rows, exs, ws, outs, *, ncb=1, tr=256, seq=None, name):
    rows, exs, ws = [list(map(_norm_arg, g)) for g in (rows, exs, ws)]
    T = rows[0][0].shape[0]
    seq = seq or T
    tr = min(tr, seq)
    tpe = seq // tr
    nr, ne, nw = len(rows), len(exs), len(ws)

    def body(*refs):
        ins = [r[...].astype(F32) for r in refs[:nr]]
        ins += [r[0].astype(F32) for r in refs[nr:nr + ne]]
        ins += [r[...].astype(F32) for r in refs[nr + ne:nr + ne + nw]]
        res = fn(*ins)
        if not isinstance(res, (tuple, list)):
            res = (res,)
        for o_ref, v in zip(refs[nr + ne + nw:], res):
            o_ref[...] = v.astype(o_ref.dtype)

    out_shape, out_specs = [], []
    for width, dtype, blocked in outs:
        out_shape.append(jax.ShapeDtypeStruct((T, width), dtype))
        if blocked:
            out_specs.append(pl.BlockSpec((tr, width // ncb), lambda j, i: (i, j)))
        else:
            out_specs.append(pl.BlockSpec((tr, width), lambda j, i: (i, 0)))
    res = pl.pallas_call(
        body, name=name, out_shape=out_shape, grid=(ncb, T // tr),
        in_specs=_ew_specs(rows, exs, ws, tr, tpe), out_specs=out_specs,
        compiler_params=_cparams(("parallel", "parallel")),
    )(*[a[0] for a in rows + exs + ws])
    return res


def _ew_bwd(fn, rows, exs, ws, douts, need, *, ncb=1, tr=256, seq=None, row_dtypes=None, name):
    rows, exs, ws, douts = [list(map(_norm_arg, g)) for g in (rows, exs, ws, douts)]
    T = rows[0][0].shape[0]
    seq = seq or T
    tr = min(tr, seq)
    tpe = seq // tr
    nrt = T // tr
    nr, ne, nw, nd = len(rows), len(exs), len(ws), len(douts)
    nin = nr + ne + nw
    args = rows + exs + ws
    row_dtypes = row_dtypes or [F32] * nr
    for k, (arr, cw, off) in enumerate(exs):
        assert not (need[nr + k] and cw is None and ncb > 1)

    def body(*refs):
        j, i = pl.program_id(0), pl.program_id(1)
        ins = [r[...].astype(F32) for r in refs[:nr]]
        ins += [r[0].astype(F32) for r in refs[nr:nr + ne]]
        ins += [r[...].astype(F32) for r in refs[nr + ne:nin]]
        cts = [r[...].astype(F32) for r in refs[nin:nin + nd]]
        res, vjp = jax.vjp(fn, *ins)
        if isinstance(res, (tuple, list)):
            grads = vjp(tuple(cts))
        else:
            grads = vjp(cts[0])
        o = nin + nd
        for k in range(nin):
            if not need[k]:
                continue
            o_ref, g = refs[o], grads[k]
            o += 1
            if k < nr:
                o_ref[...] = g.astype(o_ref.dtype)
            elif k < nr + ne:
                first = (i % tpe) == 0

                @pl.when(first)
                def _(o_ref=o_ref, g=g):
                    o_ref[0] = g

                @pl.when(jnp.logical_not(first))
                def _(o_ref=o_ref, g=g):
                    o_ref[0] += g
            else:
                blocked = args[k][1] is not None
                first = (i == 0) if blocked else jnp.logical_and(i == 0, j == 0)

                @pl.when(first)
                def _(o_ref=o_ref, g=g):
                    o_ref[...] = g

                @pl.when(jnp.logical_not(first))
                def _(o_ref=o_ref, g=g):
                    o_ref[...] += g

    in_specs = _ew_specs(rows, exs, ws, tr, tpe) + _ew_specs(douts, [], [], tr, tpe)
    out_shape, out_specs = [], []
    all_specs = _ew_specs(rows, exs, ws, tr, tpe)
    for k in range(nin):
        if not need[k]:
            continue
        arr, cw, off = args[k]
        if k < nr and cw is not None:
            out_shape.append(jax.ShapeDtypeStruct((T, ncb * cw), row_dtypes[k]))
            out_specs.append(pl.BlockSpec((tr, cw), lambda j, i: (i, j)))
        elif k >= nr and cw is not None:
            assert off == 0 and arr.shape[-1] == ncb * cw
            out_shape.append(jax.ShapeDtypeStruct(arr.shape, F32))
            out_specs.append(all_specs[k])
        else:
            out_shape.append(jax.ShapeDtypeStruct(arr.shape, row_dtypes[k] if k < nr else F32))
            out_specs.append(all_specs[k])
    res = pl.pallas_call(
        body, name=name, out_shape=out_shape, grid=(ncb, nrt),
        in_specs=in_specs, out_specs=out_specs,
        compiler_params=_cparams(("arbitrary", "arbitrary")),
    )(*[a[0] for a in args + douts])
    return res


def _rms(x, g):
    return x * lax.rsqrt(jnp.mean(x * x, axis=-1, keepdims=True) + EPS) * g


def _f_rmsmod(x, sh, sc, g):
    return _rms(x, g) * (1.0 + sc) + sh


def _f_swiglu(a, b):
    return a * jax.nn.sigmoid(a) * b


def _f_res(res, x, y, gate):
    return x + res * gate * y


def _f_res_rmsmod(res, x, y, gate, sh, sc, g):
    x1 = x + res * gate * y
    return x1, _rms(x1, g) * (1.0 + sc) + sh


def _mm_sh(step, ins, in_specs, out_shape, out_spec, acc_shape, grid, name):
    nk = grid[2]
    n = len(ins)

    def body(*refs):
        o_ref, acc_ref = refs[n], refs[n + 1]
        k = pl.program_id(2)

        @pl.when(k == 0)
        def _():
            acc_ref[...] = jnp.zeros_like(acc_ref)

        step(k, acc_ref, *refs[:n])

        @pl.when(k == nk - 1)
        def _():
            if len(o_ref.shape) == 3:
                o_ref[0] = acc_ref[...].astype(o_ref.dtype)
            else:
                o_ref[...] = acc_ref[...].astype(o_ref.dtype)

    return pl.pallas_call(
        body, name=name, out_shape=out_shape, grid=grid, in_specs=in_specs, out_specs=out_spec,
        scratch_shapes=[pltpu.VMEM(acc_shape, F32)],
        compiler_params=_cparams(("parallel", "parallel", "arbitrary")),
    )(*ins)


def _dg(a, b, ca, cb):
    return lax.dot_general(a, b, (((ca,), (cb,)), ((), ())), preferred_element_type=F32)


def _ffn_tiles(T, D):
    return _pick(T, (1024, 512, 256, 128)), _pick(D, (1024, 512, 256, 128)), _pick(D, (2048, 1024, 512, 256, 128)), \
        _pick(T, (2048, 1024, 512, 256, 128))


def _ffn_up(h, ws, name):
    n, D, c = ws.shape
    T = h.shape[0]
    tm, _, tkd, _ = _ffn_tiles(T, D)

    def step(k, acc, a_ref, b_ref):
        acc[...] += _dg(a_ref[...], b_ref[0], 1, 0)

    return _mm_sh(step, [h, ws],
                  [pl.BlockSpec((tm, tkd), lambda i, j, k: (i, k)), pl.BlockSpec((1, tkd, c), lambda i, j, k: (j, k, 0))],
                  jax.ShapeDtypeStruct((n, T, c), F32), pl.BlockSpec((1, tm, c), lambda i, j, k: (j, i, 0)),
                  (tm, c), (T // tm, n, D // tkd), name)


def _ffn_down(s3, w2s, name):
    n, T, c = s3.shape
    D = w2s.shape[2]
    tm, tn, _, _ = _ffn_tiles(T, D)

    def step(k, acc, a_ref, b_ref):
        acc[...] += _dg(a_ref[0], b_ref[0], 1, 0)

    return _mm_sh(step, [s3, w2s],
                  [pl.BlockSpec((1, tm, c), lambda i, j, k: (k, i, 0)), pl.BlockSpec((1, c, tn), lambda i, j, k: (k, 0, j))],
                  jax.ShapeDtypeStruct((T, D), F32), pl.BlockSpec((tm, tn), lambda i, j, k: (i, j)),
                  (tm, tn), (T // tm, D // tn, n), name)


def _ffn_down_dx(dy, w2s, name):
    n, c, D = w2s.shape
    T = dy.shape[0]
    tm, _, tkd, _ = _ffn_tiles(T, D)

    def step(k, acc, a_ref, b_ref):
        acc[...] += _dg(a_ref[...], b_ref[0], 1, 1)

    return _mm_sh(step, [dy, w2s],
                  [pl.BlockSpec((tm, tkd), lambda i, j, k: (i, k)), pl.BlockSpec((1, c, tkd), lambda i, j, k: (j, 0, k))],
                  jax.ShapeDtypeStruct((n, T, c), F32), pl.BlockSpec((1, tm, c), lambda i, j, k: (j, i, 0)),
                  (tm, c), (T // tm, n, D // tkd), name)


def _ffn_down_dw(s3, dy, name):
    n, T, c = s3.shape
    D = dy.shape[1]
    _, tn, _, tkt = _ffn_tiles(T, D)

    def step(k, acc, a_ref, b_ref):
        acc[...] += _dg(a_ref[0], b_ref[...], 0, 0)

    return _mm_sh(step, [s3, dy],
                  [pl.BlockSpec((1, tkt, c), lambda i, j, k: (i, k, 0)), pl.BlockSpec((tkt, tn), lambda i, j, k: (k, j))],
                  jax.ShapeDtypeStruct((n, c, D), BF16), pl.BlockSpec((1, c, tn), lambda i, j, k: (i, 0, j)),
                  (c, tn), (n, D // tn, T // tkt), name)


def _ffn_up_dw(h, d3, name):
    n, T, c = d3.shape
    D = h.shape[1]
    _, tn, _, tkt = _ffn_tiles(T, D)

    def step(k, acc, a_ref, b_ref):
        acc[...] += _dg(a_ref[...], b_ref[0], 0, 0)

    return _mm_sh(step, [h, d3],
                  [pl.BlockSpec((tkt, tn), lambda i, j, k: (k, i)), pl.BlockSpec((1, tkt, c), lambda i, j, k: (j, k, 0))],
                  jax.ShapeDtypeStruct((n, D, c), BF16), pl.BlockSpec((1, tn, c), lambda i, j, k: (j, i, 0)),
                  (tn, c), (D // tn, n, T // tkt), name)


def _ffn_up_dx(da3, db3, w1s, w3s, name):
    n, T, c = da3.shape
    D = w1s.shape[1]
    tm, tn, _, _ = _ffn_tiles(T, D)

    def step(k, acc, da_ref, db_ref, w1_ref, w3_ref):
        @pl.when(k < n)
        def _():
            acc[...] += _dg(da_ref[0], w1_ref[0], 1, 1)

        @pl.when(k >= n)
        def _():
            acc[...] += _dg(db_ref[0], w3_ref[0], 1, 1)

    lo = lambda k: jnp.minimum(k, n - 1)
    hi = lambda k: jnp.maximum(k - n, 0)
    return _mm_sh(step, [da3, db3, w1s, w3s],
                  [pl.BlockSpec((1, tm, c), lambda i, j, k: (lo(k), i, 0)), pl.BlockSpec((1, tm, c), lambda i, j, k: (hi(k), i, 0)),
                   pl.BlockSpec((1, tn, c), lambda i, j, k: (lo(k), j, 0)), pl.BlockSpec((1, tn, c), lambda i, j, k: (hi(k), j, 0))],
                  jax.ShapeDtypeStruct((T, D), F32), pl.BlockSpec((tm, tn), lambda i, j, k: (i, j)),
                  (tm, tn), (T // tm, D // tn, 2 * n), name)


def _ffn_act(a3, b3, name, tr=512):
    n, T, c = a3.shape
    tr = _pick(T, (tr, 256, 128))

    def body(a_ref, b_ref, o_ref):
        o_ref[...] = _f_swiglu(a_ref[...], b_ref[...]).astype(o_ref.dtype)

    spec = pl.BlockSpec((1, tr, c), lambda j, i: (j, i, 0))
    return pl.pallas_call(body, name=name, out_shape=jax.ShapeDtypeStruct((n, T, c), BF16), grid=(n, T // tr),
                          in_specs=[spec, spec], out_specs=spec, compiler_params=_cparams(("parallel", "parallel")))(a3, b3)


def _ffn_act_bwd(a3, b3, ds3, name, tr=512):
    n, T, c = a3.shape
    tr = _pick(T, (tr, 256, 128))

    def body(a_ref, b_ref, d_ref, da_ref, db_ref):
        _, vjp = jax.vjp(_f_swiglu, a_ref[...], b_ref[...])
        da, db = vjp(d_ref[...])
        da_ref[...] = da.astype(da_ref.dtype)
        db_ref[...] = db.astype(db_ref.dtype)

    spec = pl.BlockSpec((1, tr, c), lambda j, i: (j, i, 0))
    return pl.pallas_call(body, name=name, out_shape=[jax.ShapeDtypeStruct((n, T, c), BF16)] * 2, grid=(n, T // tr),
                          in_specs=[spec] * 3, out_specs=[spec, spec],
                          compiler_params=_cparams(("parallel", "parallel")))(a3, b3, ds3)


def _ffn_fwd(h, w1s, w3s, w2s, tag):
    a3 = _ffn_up(h, w1s, f"{tag}_up1")
    b3 = _ffn_up(h, w3s, f"{tag}_up3")
    s3 = _ffn_act(a3, b3, f"{tag}_act")
    return _ffn_down(s3, w2s, f"{tag}_down"), (h, a3, b3, s3)


def _ffn_bwd(dy, saved, w1s, w3s, w2s, tag):
    h, a3, b3, s3 = saved
    ds3 = _ffn_down_dx(dy, w2s, f"{tag}_down_dx")
    dw2s = _ffn_down_dw(s3, dy, f"{tag}_down_dw")
    da3, db3 = _ffn_act_bwd(a3, b3, ds3, f"{tag}_act_bwd")
    dw1s = _ffn_up_dw(h, da3, f"{tag}_up1_dw")
    dw3s = _ffn_up_dw(h, db3, f"{tag}_up3_dw")
    dh = _ffn_up_dx(da3, db3, w1s, w3s, f"{tag}_up_dx")
    return dh, dw1s, dw3s, dw2s


def _shift_down(x, j):
    if j == 0:
        return x
    row = lax.broadcasted_iota(jnp.int32, x.shape, 0)
    return jnp.where(row >= j, pltpu.roll(x, j, 0), 0.0)


def _shift_up(x, j):
    if j == 0:
        return x
    n = x.shape[0]
    row = lax.broadcasted_iota(jnp.int32, x.shape, 0)
    return jnp.where(row < n - j, pltpu.roll(x, n - j, 0), 0.0)


def _gdn_post_conv(kind, y):
    s = y * jax.nn.sigmoid(y)
    if kind == "v":
        return s
    n = lax.rsqrt(jnp.sum(s * s, axis=-1, keepdims=True) + EPS)
    return s * n * (GDN_DIM ** -0.5 if kind == "q" else 1.0)


def _conv_taps(w_ref):
    return [w_ref[k:k + 1, :] for k in range(GDN_CONV)]


def _gdn_conv(x, w):
    y = w[GDN_CONV - 1] * x
    for k in range(GDN_CONV - 1):
        y = y + w[k] * _shift_down(x, GDN_CONV - 1 - k)
    return y


def _gdn_pre_fwd(proj, conv_w, kind, first_block, B, name):
    T = proj.shape[0]
    S = T // B
    nh = GDN_HEADS

    def body(x_ref, w_ref, o_ref):
        y = _gdn_conv(x_ref[...], _conv_taps(w_ref))
        o_ref[...] = _gdn_post_conv(kind, y)

    return pl.pallas_call(
        body, name=name, out_shape=jax.ShapeDtypeStruct((T, nh * GDN_DIM), F32), grid=(nh, B),
        in_specs=[pl.BlockSpec((S, GDN_DIM), lambda c, b: (b, c + first_block)),
                  pl.BlockSpec((GDN_CONV, GDN_DIM), lambda c, b: (0, c))],
        out_specs=pl.BlockSpec((S, GDN_DIM), lambda c, b: (b, c)),
        compiler_params=_cparams(("parallel", "parallel")),
    )(proj, conv_w)


def _gdn_pre_bwd(proj, conv_w, dout, kind, first_block, B, name):
    T = proj.shape[0]
    S = T // B
    nh = GDN_HEADS

    def body(x_ref, w_ref, d_ref, dx_ref, dw_ref):
        b = pl.program_id(1)
        x, w = x_ref[...], _conv_taps(w_ref)
        y = _gdn_conv(x, w)
        _, vjp = jax.vjp(functools.partial(_gdn_post_conv, kind), y)
        (dy,) = vjp(d_ref[...])
        dx = w[GDN_CONV - 1] * dy

        @pl.when(b == 0)
        def _():
            dw_ref[...] = jnp.zeros_like(dw_ref)

        for k in range(GDN_CONV):
            j = GDN_CONV - 1 - k
            if j:
                dx = dx + w[k] * _shift_up(dy, j)
            dw_ref[k:k + 1, :] += jnp.sum(dy * _shift_down(x, j), axis=0, keepdims=True)
        dx_ref[...] = dx

    return pl.pallas_call(
        body, name=name,
        out_shape=[jax.ShapeDtypeStruct((T, nh * GDN_DIM), F32), jax.ShapeDtypeStruct((GDN_CONV, nh * GDN_DIM), F32)],
        grid=(nh, B),
        in_specs=[pl.BlockSpec((S, GDN_DIM), lambda c, b: (b, c + first_block)),
                  pl.BlockSpec((GDN_CONV, GDN_DIM), lambda c, b: (0, c)),
                  pl.BlockSpec((S, GDN_DIM), lambda c, b: (b, c))],
        out_specs=[pl.BlockSpec((S, GDN_DIM), lambda c, b: (b, c)),
                   pl.BlockSpec((GDN_CONV, GDN_DIM), lambda c, b: (0, c))],
        compiler_params=_cparams(("arbitrary", "arbitrary")),
    )(proj, conv_w, dout)


def _softplus(x):
    return jnp.maximum(x, 0.0) + jnp.log(1.0 + jnp.exp(-jnp.abs(x)))


def _f_gdn_gates(ba, a_log, dt_bias):
    n = ba.shape[0]
    lane = lax.broadcasted_iota(jnp.int32, ba.shape, 1)
    beta = jax.nn.sigmoid(ba)
    g = -jnp.exp(a_log) * _softplus(ba + dt_bias)
    ri = lax.broadcasted_iota(jnp.int32, (n, n), 0)
    ci = lax.broadcasted_iota(jnp.int32, (n, n), 1)
    tri = jnp.where((ri // GDN_CHUNK == ci // GDN_CHUNK) & (ci <= ri), 1.0, 0.0)
    gc = jnp.dot(tri, g, precision=HI, preferred_element_type=F32)
    return jnp.where(lane < GDN_HEADS, beta, gc)


def _bmm(a, b, ca, cb):
    return lax.dot_general(a, b, (((ca,), (cb,)), ((0,), (0,))), precision=HI3, preferred_element_type=F32)


GDN_INV_LEAF = 16


def _unit_lower_inverse(low, ri, ci):
    C = low.shape[1]
    b = GDN_INV_LEAF
    p = jnp.where(ri // b == ci // b, low, 0.0)
    x = jnp.where(ci == ri, 1.0, 0.0) - p
    for _ in range(int(math.log2(b)) - 1):
        p = _bmm(p, p, 2, 1)
        x = x + _bmm(x, p, 2, 1)
    while b < C:
        off = jnp.where(jnp.logical_and(ri // (2 * b) == ci // (2 * b), ri // b != ci // b), low, 0.0)
        x = x - _bmm(_bmm(x, off, 2, 1), x, 2, 1)
        b *= 2
    return x


def _gdn_chunk_fn(q, k, v, gc, beta, h):
    N, C, d = q.shape
    ri = lax.broadcasted_iota(jnp.int32, (N, C, C), 1)
    ci = lax.broadcasted_iota(jnp.int32, (N, C, C), 2)
    kb = k * beta
    vb = v * beta
    gi = jnp.broadcast_to(gc, (N, C, C))
    gj = jnp.swapaxes(gi, 1, 2)
    decay = jnp.exp(jnp.where(ci <= ri, gi - gj, -1e30))
    low = jnp.where(ci < ri, _bmm(kb, k, 2, 2) * decay, 0.0)
    ainv = _unit_lower_inverse(low, ri, ci)
    eg = jnp.exp(gc)
    u = _bmm(ainv, vb, 2, 1)
    w = _bmm(ainv, kb * eg, 2, 1)
    attn = _bmm(q, k, 2, 2) * decay
    v_new = u - _bmm(w, h, 2, 1)
    o = _bmm(q * eg, h, 2, 1) + _bmm(attn, v_new, 2, 1)
    rc = lax.broadcasted_iota(jnp.int32, (N, C, 1), 1)
    g_last = jnp.sum(jnp.where(rc == C - 1, gc, 0.0), axis=1, keepdims=True)
    h_new = h * jnp.exp(g_last) + _bmm(k * jnp.exp(g_last - gc), v_new, 1, 1)
    return o, h_new


def _gdn_heads(x):
    return jnp.stack([x[:, h * GDN_DIM:(h + 1) * GDN_DIM] for h in range(GDN_HEADS)], axis=0)


def _gdn_gate_cols(G, first_lane):
    lane = lax.broadcasted_iota(jnp.int32, G.shape, 1)
    return jnp.stack([jnp.sum(jnp.where(lane == first_lane + h, G, 0.0), axis=1, keepdims=True)
                      for h in range(GDN_HEADS)], axis=0)


def _gdn_chunk_specs(nc, rev):
    C, W = GDN_CHUNK, GDN_HEADS * GDN_DIM

    def at(n):
        return nc - 1 - n if rev else n

    row = lambda b, n: (b * nc + at(n), 0)
    return [pl.BlockSpec((C, W), row)] * 3 + [pl.BlockSpec((C, LANES), row)]


def _gdn_scan_fwd(q, k, v, G, B, name):
    T, W = q.shape
    C = GDN_CHUNK
    nc = T // B // C

    def body(q_ref, k_ref, v_ref, g_ref, o_ref, hs_ref, h_ref):
        @pl.when(pl.program_id(1) == 0)
        def _():
            h_ref[...] = jnp.zeros_like(h_ref)

        G_ = g_ref[...]
        h = h_ref[...]
        hs_ref[0, 0] = h
        o, hn = _gdn_chunk_fn(_gdn_heads(q_ref[...]), _gdn_heads(k_ref[...]), _gdn_heads(v_ref[...]),
                              _gdn_gate_cols(G_, GDN_HEADS), _gdn_gate_cols(G_, 0), h)
        h_ref[...] = hn
        for hd in range(GDN_HEADS):
            o_ref[:, hd * GDN_DIM:(hd + 1) * GDN_DIM] = o[hd]

    return pl.pallas_call(
        body, name=name,
        out_shape=[jax.ShapeDtypeStruct((T, W), F32), jax.ShapeDtypeStruct((B, nc, GDN_HEADS, GDN_DIM, GDN_DIM), F32)],
        grid=(B, nc), in_specs=_gdn_chunk_specs(nc, False),
        out_specs=[pl.BlockSpec((C, W), lambda b, n: (b * nc + n, 0)),
                   pl.BlockSpec((1, 1, GDN_HEADS, GDN_DIM, GDN_DIM), lambda b, n: (b, n, 0, 0, 0))],
        scratch_shapes=[pltpu.VMEM((GDN_HEADS, GDN_DIM, GDN_DIM), F32)],
        compiler_params=_cparams(("parallel", "arbitrary")),
    )(q, k, v, G)


def _gdn_scan_bwd(q, k, v, G, hs, do, B, name):
    T, W = q.shape
    C = GDN_CHUNK
    nc = T // B // C

    def body(q_ref, k_ref, v_ref, g_ref, hs_ref, do_ref, dq_ref, dk_ref, dv_ref, dg_ref, dh_ref):
        @pl.when(pl.program_id(1) == 0)
        def _():
            dh_ref[...] = jnp.zeros_like(dh_ref)

        G_ = g_ref[...]
        args = (_gdn_heads(q_ref[...]), _gdn_heads(k_ref[...]), _gdn_heads(v_ref[...]),
                _gdn_gate_cols(G_, GDN_HEADS), _gdn_gate_cols(G_, 0), hs_ref[0, 0])
        _, vjp = jax.vjp(_gdn_chunk_fn, *args)
        dq, dk, dv, dgc, dbeta, dh = vjp((_gdn_heads(do_ref[...]), dh_ref[...]))
        dh_ref[...] = dh
        lane = lax.broadcasted_iota(jnp.int32, G_.shape, 1)
        dG = jnp.zeros_like(G_)
        for hd in range(GDN_HEADS):
            sl = slice(hd * GDN_DIM, (hd + 1) * GDN_DIM)
            dq_ref[:, sl] = dq[hd]
            dk_ref[:, sl] = dk[hd]
            dv_ref[:, sl] = dv[hd]
            dG = dG + jnp.where(lane == hd, dbeta[hd], 0.0) + jnp.where(lane == GDN_HEADS + hd, dgc[hd], 0.0)
        dg_ref[...] = dG

    rrow = lambda b, n: (b * nc + nc - 1 - n, 0)
    return pl.pallas_call(
        body, name=name,
        out_shape=[jax.ShapeDtypeStruct((T, W), F32)] * 3 + [jax.ShapeDtypeStruct((T, LANES), F32)],
        grid=(B, nc),
        in_specs=_gdn_chunk_specs(nc, True) + [
            pl.BlockSpec((1, 1, GDN_HEADS, GDN_DIM, GDN_DIM), lambda b, n: (b, nc - 1 - n, 0, 0, 0)),
            pl.BlockSpec((C, W), rrow)],
        out_specs=[pl.BlockSpec((C, W), rrow)] * 3 + [pl.BlockSpec((C, LANES), rrow)],
        scratch_shapes=[pltpu.VMEM((GDN_HEADS, GDN_DIM, GDN_DIM), F32)],
        compiler_params=_cparams(("parallel", "arbitrary")),
    )(q, k, v, G, hs, do)


def _f_gdn_out(o, z, w):
    return _rms(o, w) * z * jax.nn.sigmoid(z)


def _gdn_fwd(qkv, z, ba, conv_w, gate_params, out_norm, B, tag):
    a_log, dt_bias = gate_params
    W = GDN_HEADS * GDN_DIM
    qn, kn, vn = [_gdn_pre_fwd(qkv[0], conv_w[:, i * W:(i + 1) * W], kd, qkv[1] + i * GDN_HEADS, B, f"{tag}_pre_{kd}")
                  for i, kd in enumerate("qkv")]
    (G,) = _ew_fwd(_f_gdn_gates, [(ba[0], LANES, ba[1])], [], [a_log, dt_bias], [(LANES, F32, False)],
                   name=f"{tag}_gates")
    o, hs = _gdn_scan_fwd(qn, kn, vn, G, B, f"{tag}_scan")
    (y,) = _ew_fwd(_f_gdn_out, [(o, GDN_DIM, 0), (z[0], GDN_DIM, z[1])], [], [out_norm], [(W, BF16, True)],
                   ncb=GDN_HEADS, name=f"{tag}_out")
    return y, (qkv, z, ba, qn, kn, vn, G, hs, o)


def _gdn_bwd(dy, saved, conv_w, gate_params, out_norm, B, tag):
    qkv, z, ba, qn, kn, vn, G, hs, o = saved
    a_log, dt_bias = gate_params
    W = GDN_HEADS * GDN_DIM
    do, dz, d_out_norm = _ew_bwd(_f_gdn_out, [(o, GDN_DIM, 0), (z[0], GDN_DIM, z[1])], [], [out_norm],
                                 [(dy, GDN_DIM, 0)], [True] * 3, ncb=GDN_HEADS, name=f"{tag}_out_bwd")
    dq, dk, dv, dG = _gdn_scan_bwd(qn, kn, vn, G, hs, do, B, f"{tag}_scan_bwd")
    dba, d_a_log, d_dt_bias = _ew_bwd(_f_gdn_gates, [(ba[0], LANES, ba[1])], [], [a_log, dt_bias], [dG], [True] * 3,
                                      name=f"{tag}_gates_bwd")
    dxs, dws = [], []
    for i, (kd, d) in enumerate(zip("qkv", (dq, dk, dv))):
        dx, dw = _gdn_pre_bwd(qkv[0], conv_w[:, i * W:(i + 1) * W], d, kd, qkv[1] + i * GDN_HEADS, B,
                              f"{tag}_pre_{kd}_bwd")
        dxs.append(dx)
        dws.append(dw)
    return (jnp.concatenate(dxs, axis=1), dz, dba,
            dict(conv=jnp.concatenate(dws, axis=1), a_log=d_a_log, dt_bias=d_dt_bias, out_norm=d_out_norm))


S5_GPB = LANES // S5_GROUP
S5_SLANES = S5_GPB * S5_STATE


def _cmul(ar, ai, br, bi):
    return ar * br - ai * bi, ar * bi + ai * br


def _s5_prep_fn(a_re, a_im, ls, b_re, b_im):
    lr = jnp.minimum(a_re, S5_MAX_RE)
    li = a_im
    step = jnp.exp(ls)
    mag = jnp.exp(lr * step)
    lbr, lbi = mag * jnp.cos(li * step), mag * jnp.sin(li * step)
    den = lr * lr + li * li
    cr = ((lbr - 1.0) * lr + lbi * li) / den
    ci = (lbi * lr - (lbr - 1.0) * li) / den
    bbr = cr[:, None, :] * b_re - ci[:, None, :] * b_im
    bbi = cr[:, None, :] * b_im + ci[:, None, :] * b_re
    return lbr, lbi, bbr, bbi


def _s5_prep_fwd(args, name):
    G, I, P = args[3].shape
    shp = [jax.ShapeDtypeStruct((G, P), F32)] * 2 + [jax.ShapeDtypeStruct((G, I, P), F32)] * 2

    def body(*refs):
        for o_ref, v in zip(refs[5:], _s5_prep_fn(*[r[...] for r in refs[:5]])):
            o_ref[...] = v

    return pl.pallas_call(body, name=name, out_shape=shp, compiler_params=_cparams())(*args)


def _s5_prep_bwd(args, cts, name):
    shp = [jax.ShapeDtypeStruct(a.shape, F32) for a in args]

    def body(*refs):
        _, vjp = jax.vjp(_s5_prep_fn, *[r[...] for r in refs[:5]])
        for o_ref, v in zip(refs[9:], vjp(tuple(r[...] for r in refs[5:9]))):
            o_ref[...] = v

    return pl.pallas_call(body, name=name, out_shape=shp, compiler_params=_cparams())(*args, *cts)


def _s5_blockdiag_in(bb):
    G, I, P = bb.shape
    nb = G // S5_GPB
    return jnp.einsum("jgip,gh->jgihp", bb.reshape(nb, S5_GPB, I, P), jnp.eye(S5_GPB, dtype=bb.dtype)).reshape(
        nb, S5_GPB * I, S5_GPB * P)


def _s5_blockdiag_in_t(d):
    nb = d.shape[0]
    d = d.reshape(nb, S5_GPB, S5_GROUP, S5_GPB, S5_STATE)
    return jnp.einsum("jgihp,gh->jgip", d, jnp.eye(S5_GPB, dtype=d.dtype)).reshape(nb * S5_GPB, S5_GROUP, S5_STATE)


def _s5_blockdiag_out(c):
    G, I, P = c.shape
    nb = G // S5_GPB
    return jnp.einsum("jgip,gh->jgphi", c.reshape(nb, S5_GPB, I, P), jnp.eye(S5_GPB, dtype=c.dtype)).reshape(
        nb, S5_GPB * P, S5_GPB * I)


def _s5_blockdiag_out_t(d):
    nb = d.shape[0]
    d = d.reshape(nb, S5_GPB, S5_STATE, S5_GPB, S5_GROUP)
    return jnp.einsum("jgphi,gh->jgip", d, jnp.eye(S5_GPB, dtype=d.dtype)).reshape(nb * S5_GPB, S5_GROUP, S5_STATE)


def _s5_powers(lr, li, n):
    out = []
    for _ in range(int(math.log2(n))):
        out.append((lr, li))
        lr, li = _cmul(lr, li, lr, li)
    return out


def _s5_local_scan(sr, si, powers, up):
    shift = _shift_up if up else _shift_down
    for k, (pr, pi) in enumerate(powers):
        d = 1 << k
        tr_, ti_ = _cmul(pr, pi, shift(sr, d), shift(si, d))
        sr, si = sr + tr_, si + ti_
    return sr, si


def _dot_hi(a, b, ca=1, cb=0):
    return lax.dot_general(a, b, (((ca,), (cb,)), ((), ())), precision=HI3, preferred_element_type=F32)


def _s5_specs(nt, rev, ublk):
    tc = S5_TCHUNK

    def at(t):
        return nt - 1 - t if rev else t

    return [
        pl.BlockSpec((tc, LANES), lambda j, b, t: (b * nt + at(t), j + ublk)),
        pl.BlockSpec((1, S5_SLANES), lambda j, b, t: (0, j)),
        pl.BlockSpec((1, S5_SLANES), lambda j, b, t: (0, j)),
        pl.BlockSpec((1, LANES, S5_SLANES), lambda j, b, t: (j, 0, 0)),
        pl.BlockSpec((1, LANES, S5_SLANES), lambda j, b, t: (j, 0, 0)),
        pl.BlockSpec((1, S5_SLANES, LANES), lambda j, b, t: (j, 0, 0)),
        pl.BlockSpec((1, S5_SLANES, LANES), lambda j, b, t: (j, 0, 0)),
        pl.BlockSpec((1, LANES), lambda j, b, t: (0, j)),
    ]


def _s5_chunk_states(u, lr, li, b_re, b_im, cr, ci, powers):
    bur, bui = _dot_hi(u, b_re), _dot_hi(u, b_im)
    row = lax.broadcasted_iota(jnp.int32, bur.shape, 0)
    inr, ini = _cmul(lr, li, cr, ci)
    bur = bur + jnp.where(row == 0, inr, 0.0)
    bui = bui + jnp.where(row == 0, ini, 0.0)
    return _s5_local_scan(bur, bui, powers, False)


def _s5_scan_fwd(u, lam_re, lam_im, Bre, Bim, Cre, Cim, dskip, B, name):
    u, ublk = u
    T = u.shape[0]
    nb = Bre.shape[0]
    Wd = nb * LANES
    tc = S5_TCHUNK
    nt = T // B // tc
    L = nb * S5_SLANES

    def body(u_ref, lr_ref, li_ref, br_ref, bi_ref, cr_ref, ci_ref, d_ref, y_ref, csr_ref, csi_ref, car_ref, cai_ref):
        @pl.when(pl.program_id(2) == 0)
        def _():
            car_ref[...] = jnp.zeros_like(car_ref)
            cai_ref[...] = jnp.zeros_like(cai_ref)

        csr_ref[0, 0] = car_ref[...]
        csi_ref[0, 0] = cai_ref[...]
        u_ = u_ref[...]
        lr, li = lr_ref[...], li_ref[...]
        sr, si = _s5_chunk_states(u_, lr, li, br_ref[0], bi_ref[0], car_ref[0:1, :], cai_ref[0:1, :],
                                  _s5_powers(lr, li, tc))
        y_ref[...] = _dot_hi(sr, cr_ref[0]) - _dot_hi(si, ci_ref[0]) + d_ref[...] * u_
        row = lax.broadcasted_iota(jnp.int32, sr.shape, 0)
        car_ref[0:1, :] = jnp.sum(jnp.where(row == tc - 1, sr, 0.0), axis=0, keepdims=True)
        cai_ref[0:1, :] = jnp.sum(jnp.where(row == tc - 1, si, 0.0), axis=0, keepdims=True)

    cs_shape = jax.ShapeDtypeStruct((B, nt, SUBLANES, L), F32)
    cs_spec = pl.BlockSpec((1, 1, SUBLANES, S5_SLANES), lambda j, b, t: (b, t, 0, j))
    return pl.pallas_call(
        body, name=name, out_shape=[jax.ShapeDtypeStruct((T, Wd), F32), cs_shape, cs_shape],
        grid=(nb, B, nt), in_specs=_s5_specs(nt, False, ublk),
        out_specs=[pl.BlockSpec((tc, LANES), lambda j, b, t: (b * nt + t, j)), cs_spec, cs_spec],
        scratch_shapes=[pltpu.VMEM((SUBLANES, S5_SLANES), F32)] * 2,
        compiler_params=_cparams(("parallel", "parallel", "arbitrary")),
    )(u, lam_re, lam_im, Bre, Bim, Cre, Cim, dskip)


def _s5_scan_bwd(u, lam_re, lam_im, Bre, Bim, Cre, Cim, dskip, csr, csi, dy, B, name):
    u, ublk = u
    T = u.shape[0]
    nb = Bre.shape[0]
    Wd = nb * LANES
    tc = S5_TCHUNK
    nt = T // B // tc
    L = nb * S5_SLANES

    def body(u_ref, lr_ref, li_ref, br_ref, bi_ref, cr_ref, ci_ref, d_ref, csr_ref, csi_ref, dy_ref,
             du_ref, dlr_ref, dli_ref, dbr_ref, dbi_ref, dcr_ref, dci_ref, dd_ref, gr_ref, gi_ref):
        first = jnp.logical_and(pl.program_id(1) == 0, pl.program_id(2) == 0)

        @pl.when(pl.program_id(2) == 0)
        def _():
            gr_ref[...] = jnp.zeros_like(gr_ref)
            gi_ref[...] = jnp.zeros_like(gi_ref)

        @pl.when(first)
        def _():
            for r in (dlr_ref, dli_ref, dbr_ref, dbi_ref, dcr_ref, dci_ref, dd_ref):
                r[...] = jnp.zeros_like(r)

        u_, dy_ = u_ref[...], dy_ref[...]
        lr, li = lr_ref[...], li_ref[...]
        powers = _s5_powers(lr, li, tc)
        c_in_r, c_in_i = csr_ref[0, 0, 0:1, :], csi_ref[0, 0, 0:1, :]
        sr, si = _s5_chunk_states(u_, lr, li, br_ref[0], bi_ref[0], c_in_r, c_in_i, powers)
        dcr_ref[0] += _dot_hi(sr, dy_, 0, 0)
        dci_ref[0] -= _dot_hi(si, dy_, 0, 0)
        dd_ref[...] += jnp.sum(dy_ * u_, axis=0, keepdims=True)
        row = lax.broadcasted_iota(jnp.int32, sr.shape, 0)
        gr = _dot_hi(dy_, cr_ref[0], 1, 1)
        gi = -_dot_hi(dy_, ci_ref[0], 1, 1)
        inr, ini = _cmul(lr, -li, gr_ref[0:1, :], gi_ref[0:1, :])
        gr = gr + jnp.where(row == tc - 1, inr, 0.0)
        gi = gi + jnp.where(row == tc - 1, ini, 0.0)
        gr, gi = _s5_local_scan(gr, gi, [(pr, -pi) for pr, pi in powers], True)
        gr_ref[0:1, :] = jnp.sum(jnp.where(row == 0, gr, 0.0), axis=0, keepdims=True)
        gi_ref[0:1, :] = jnp.sum(jnp.where(row == 0, gi, 0.0), axis=0, keepdims=True)
        pr_ = _shift_down(sr, 1) + jnp.where(row == 0, c_in_r, 0.0)
        pi_ = _shift_down(si, 1) + jnp.where(row == 0, c_in_i, 0.0)
        dlr_ref[...] += jnp.sum(gr * pr_ + gi * pi_, axis=0, keepdims=True)
        dli_ref[...] += jnp.sum(gi * pr_ - gr * pi_, axis=0, keepdims=True)
        dbr_ref[0] += _dot_hi(u_, gr, 0, 0)
        dbi_ref[0] += _dot_hi(u_, gi, 0, 0)
        du_ref[...] = dy_ * d_ref[...] + _dot_hi(gr, br_ref[0], 1, 1) + _dot_hi(gi, bi_ref[0], 1, 1)

    cs_spec = pl.BlockSpec((1, 1, SUBLANES, S5_SLANES), lambda j, b, t: (b, nt - 1 - t, 0, j))
    rrow = pl.BlockSpec((tc, LANES), lambda j, b, t: (b * nt + nt - 1 - t, j))
    lam_spec = pl.BlockSpec((1, S5_SLANES), lambda j, b, t: (0, j))
    b_spec = pl.BlockSpec((1, LANES, S5_SLANES), lambda j, b, t: (j, 0, 0))
    c_spec = pl.BlockSpec((1, S5_SLANES, LANES), lambda j, b, t: (j, 0, 0))
    return pl.pallas_call(
        body, name=name,
        out_shape=[jax.ShapeDtypeStruct((T, Wd), F32)] + [jax.ShapeDtypeStruct((1, L), F32)] * 2
        + [jax.ShapeDtypeStruct((nb, LANES, S5_SLANES), F32)] * 2
        + [jax.ShapeDtypeStruct((nb, S5_SLANES, LANES), F32)] * 2 + [jax.ShapeDtypeStruct((1, Wd), F32)],
        grid=(nb, B, nt), in_specs=_s5_specs(nt, True, ublk) + [cs_spec, cs_spec, rrow],
        out_specs=[rrow, lam_spec, lam_spec, b_spec, b_spec, c_spec, c_spec,
                   pl.BlockSpec((1, LANES), lambda j, b, t: (0, j))],
        scratch_shapes=[pltpu.VMEM((SUBLANES, S5_SLANES), F32)] * 2,
        compiler_params=_cparams(("arbitrary", "arbitrary", "arbitrary")),
    )(u, lam_re, lam_im, Bre, Bim, Cre, Cim, dskip, csr, csi, dy)


def _f_gelu(y):
    return 0.5 * y * (1.0 + jnp.tanh(math.sqrt(2.0 / math.pi) * (y + 0.044715 * (y * y * y))))


def _f_glu(pv, pg, bv, bg):
    return (pv + bv) * jax.nn.sigmoid(pg + bg)


def _s5_params(p):
    prep_in = (p["a_re"], p["a_im"], p["log_step"][:, None], jnp.swapaxes(p["b_re"], 1, 2), jnp.swapaxes(p["b_im"], 1, 2))
    return prep_in


def _s5_fwd(u, p, glu_w, B, tag):
    Wd = p["d"].shape[0]
    prep_in = _s5_params(p)
    lbr, lbi, bbr, bbi = _s5_prep_fwd(prep_in, f"{tag}_prep")
    ops = (lbr.reshape(1, -1), lbi.reshape(1, -1), _s5_blockdiag_in(bbr), _s5_blockdiag_in(bbi),
           _s5_blockdiag_out(p["c_re"]), _s5_blockdiag_out(p["c_im"]), p["d"][None])
    y, csr, csi = _s5_scan_fwd(u, *ops, B, f"{tag}_scan")
    (yg,) = _ew_fwd(_f_gelu, [y], [], [], [(Wd, BF16, False)], name=f"{tag}_gelu")
    pj = _mm(yg, glu_w, name=f"{tag}_glu")
    bv, bg = p["glu_b"][None, :Wd], p["glu_b"][None, Wd:]
    (out,) = _ew_fwd(_f_glu, [(pj, Wd, 0), (pj, Wd, 1)], [], [bv, bg], [(Wd, BF16, False)], name=f"{tag}_gate")
    return out, (u, prep_in, ops, csr, csi, y, yg, pj, bv, bg)


def _s5_bwd(dout, saved, p, glu_w, B, tag):
    u, prep_in, ops, csr, csi, y, yg, pj, bv, bg = saved
    Wd = p["d"].shape[0]
    dpv, dpg, dbv, dbg = _ew_bwd(_f_glu, [(pj, Wd, 0), (pj, Wd, 1)], [], [bv, bg], [dout], [True] * 4,
                                 row_dtypes=[BF16, BF16], name=f"{tag}_gate_bwd")
    dpj = jnp.concatenate([dpv, dpg], axis=1)
    d_glu_w = _mm(yg, dpj, ta=True, out_dtype=BF16, name=f"{tag}_glu_dw")
    dyg = _mm(dpj, glu_w, tb=True, name=f"{tag}_glu_dx")
    (dy,) = _ew_bwd(_f_gelu, [y], [], [], [dyg], [True], name=f"{tag}_gelu_bwd")
    du, dlr, dli, dBr, dBi, dCr, dCi, dd = _s5_scan_bwd(u, *ops, csr, csi, dy, B, f"{tag}_scan_bwd")
    G = p["a_re"].shape[0]
    cts = (dlr.reshape(G, S5_STATE), dli.reshape(G, S5_STATE), _s5_blockdiag_in_t(dBr), _s5_blockdiag_in_t(dBi))
    da_re, da_im, dls, db_re, db_im = _s5_prep_bwd(prep_in, cts, f"{tag}_prep_bwd")
    small = dict(a_re=da_re, a_im=da_im, log_step=dls[:, 0], b_re=jnp.swapaxes(db_re, 1, 2),
                 b_im=jnp.swapaxes(db_im, 1, 2), c_re=_s5_blockdiag_out_t(dCr), c_im=_s5_blockdiag_out_t(dCi),
                 d=dd[0], glu_b=jnp.concatenate([dbv[0], dbg[0]]))
    return du, d_glu_w, small


DIL_GW = DIL_HPG * DIL_DIM


def _f_qknorm(scale, x, w):
    n = x.shape[1]
    ri = lax.broadcasted_iota(jnp.int32, (n, n), 0)
    ci = lax.broadcasted_iota(jnp.int32, (n, n), 1)
    seg = jnp.where(ri // DIL_DIM == ci // DIL_DIM, 1.0 / DIL_DIM, 0.0)
    ms = jnp.dot(x * x, seg, precision=HI, preferred_element_type=F32)
    return x * lax.rsqrt(ms + EPS) * (w * scale)


def _dil_to_blocks(x, B):
    T = x.shape[0]
    S = T // B
    parts = []
    for gi, (_, dil) in enumerate(DIL_PAIRS):
        xg = x[:, gi * DIL_GW:(gi + 1) * DIL_GW].reshape(B, S // dil, dil, DIL_HPG, DIL_DIM)
        parts.append(xg.transpose(0, 2, 3, 1, 4).reshape(-1, DIL_DIM))
    return jnp.concatenate(parts, axis=0)


def _dil_from_blocks(y, B):
    n = y.shape[0] // len(DIL_PAIRS)
    T = n // DIL_HPG
    S = T // B
    parts = []
    for gi, (_, dil) in enumerate(DIL_PAIRS):
        yg = y[gi * n:(gi + 1) * n].reshape(B, dil, DIL_HPG, S // dil, DIL_DIM)
        parts.append(yg.transpose(0, 3, 1, 2, 4).reshape(T, DIL_GW))
    return jnp.concatenate(parts, axis=1)


DIL_BPS = 8


def _dil_block_fn(slope, has_prev, q, kp, kc, vp, vc):
    G, n, _ = q.shape
    qi = lax.broadcasted_iota(jnp.int32, (G, n, n), 1)
    kj = lax.broadcasted_iota(jnp.int32, (G, n, n), 2)
    dist = (qi - kj).astype(F32)
    sc = _bmm(q, kc, 2, 2) - slope * dist
    sp = _bmm(q, kp, 2, 2) - slope * (dist + n)
    sc = jnp.where(qi >= kj, sc, -1e30)
    sp = jnp.where(jnp.logical_and(kj >= qi, has_prev > 0.5), sp, -1e30)
    m = lax.stop_gradient(jnp.maximum(jnp.max(sc, axis=2, keepdims=True), jnp.max(sp, axis=2, keepdims=True)))
    pc = jnp.exp(sc - m)
    pp = jnp.exp(sp - m)
    l = jnp.sum(pc, axis=2, keepdims=True) + jnp.sum(pp, axis=2, keepdims=True)
    o = (_bmm(pp, vp, 2, 1) + _bmm(pc, vc, 2, 1)) / l
    return o, jnp.broadcast_to(m + jnp.log(l), o.shape)


def _dil_tables(B, S):
    nh = len(DIL_PAIRS) * DIL_HPG
    slopes, has_prev = [], []
    for gi, (_, dil) in enumerate(DIL_PAIRS):
        nbk = S // dil // DIL_BLK
        for seq in range(B * dil * DIL_HPG):
            head = gi * DIL_HPG + seq % DIL_HPG
            for n in range(nbk):
                slopes.append(dil * 2.0 ** (-ALIBI_MAX * (head + 1) / nh))
                has_prev.append(1.0 if n > 0 else 0.0)
    shape = (len(slopes), 1, 1)
    return jnp.asarray(np.array(slopes, np.float32).reshape(shape)), jnp.asarray(np.array(has_prev, np.float32).reshape(shape))


def _dil_blocks3(t):
    return t.reshape(-1, DIL_BLK, DIL_DIM)


def _dil_prev(t3):
    return jnp.concatenate([jnp.zeros_like(t3[:1]), t3[:-1]], axis=0)


def _dil_attn_fwd(qb, kb, vb, B, S, name):
    q3, k3, v3 = _dil_blocks3(qb), _dil_blocks3(kb), _dil_blocks3(vb)
    nbt = q3.shape[0]
    slope, has_prev = _dil_tables(B, S)

    def body(s_ref, h_ref, q_ref, kp_ref, kc_ref, vp_ref, vc_ref, o_ref, l_ref):
        o, l = _dil_block_fn(s_ref[...], h_ref[...], q_ref[...], kp_ref[...], kc_ref[...], vp_ref[...], vc_ref[...])
        o_ref[...] = o
        l_ref[...] = l

    blk = pl.BlockSpec((DIL_BPS, DIL_BLK, DIL_DIM), lambda m: (m, 0, 0))
    tab = pl.BlockSpec((DIL_BPS, 1, 1), lambda m: (m, 0, 0))
    o, l = pl.pallas_call(
        body, name=name, out_shape=[jax.ShapeDtypeStruct(q3.shape, F32)] * 2, grid=(nbt // DIL_BPS,),
        in_specs=[tab, tab] + [blk] * 5, out_specs=[blk, blk], compiler_params=_cparams(("parallel",)),
    )(slope, has_prev, q3, _dil_prev(k3), k3, _dil_prev(v3), v3)
    return o.reshape(qb.shape), l.reshape(qb.shape)


def _dil_attn_bwd(qb, kb, vb, do, dl, B, S, name):
    q3, k3, v3 = _dil_blocks3(qb), _dil_blocks3(kb), _dil_blocks3(vb)
    nbt = q3.shape[0]
    slope, has_prev = _dil_tables(B, S)

    def body(s_ref, h_ref, q_ref, kp_ref, kc_ref, vp_ref, vc_ref, do_ref, dl_ref, *outs):
        _, vjp = jax.vjp(functools.partial(_dil_block_fn, s_ref[...], h_ref[...]),
                         q_ref[...], kp_ref[...], kc_ref[...], vp_ref[...], vc_ref[...])
        for o_ref, g in zip(outs, vjp((do_ref[...], dl_ref[...]))):
            o_ref[...] = g

    blk = pl.BlockSpec((DIL_BPS, DIL_BLK, DIL_DIM), lambda m: (m, 0, 0))
    tab = pl.BlockSpec((DIL_BPS, 1, 1), lambda m: (m, 0, 0))
    outs = pl.pallas_call(
        body, name=name, out_shape=[jax.ShapeDtypeStruct(q3.shape, F32)] * 5, grid=(nbt // DIL_BPS,),
        in_specs=[tab, tab] + [blk] * 7, out_specs=[blk] * 5, compiler_params=_cparams(("parallel",)),
    )(slope, has_prev, q3, _dil_prev(k3), k3, _dil_prev(v3), v3, _dil_blocks3(do), _dil_blocks3(dl))
    return [t.reshape(qb.shape) for t in outs]


def _f_dil_merge(o0, o1, o2, l0, l1, l2):
    m = lax.stop_gradient(jnp.maximum(jnp.maximum(l0, l1), l2))
    e0, e1, e2 = jnp.exp(l0 - m), jnp.exp(l1 - m), jnp.exp(l2 - m)
    return (e0 * o0 + e1 * o1 + e2 * o2) / (e0 + e1 + e2)


def _dil_fwd(qkv, q_norm, k_norm, B, tag):
    qkv, fb = qkv
    T = qkv.shape[0]
    S = T // B
    Wd = len(DIL_PAIRS) * DIL_GW
    nblk = Wd // LANES
    (qn,) = _ew_fwd(functools.partial(_f_qknorm, DIL_DIM ** -0.5), [(qkv, LANES, fb)], [], [q_norm],
                    [(Wd, F32, True)], ncb=nblk, name=f"{tag}_qnorm")
    (kn,) = _ew_fwd(functools.partial(_f_qknorm, 1.0), [(qkv, LANES, fb + nblk)], [], [k_norm],
                    [(Wd, F32, True)], ncb=nblk, name=f"{tag}_knorm")
    v0 = (fb + 2 * nblk) * LANES
    qb, kb, vb = _dil_to_blocks(qn, B), _dil_to_blocks(kn, B), _dil_to_blocks(qkv[:, v0:v0 + Wd], B)
    ob, lb = _dil_attn_fwd(qb, kb, vb, B, S, f"{tag}_attn")
    o, l = _dil_from_blocks(ob, B), _dil_from_blocks(lb, B)
    gw = DIL_GW
    rows = [(o, gw, 0), (o, gw, 1), (o, gw, 2), (l, gw, 0), (l, gw, 1), (l, gw, 2)]
    (y,) = _ew_fwd(_f_dil_merge, rows, [], [], [(gw, BF16, False)], name=f"{tag}_merge")
    return y, (qkv, fb, qb, kb, vb, o, l)


def _dil_bwd(dy, saved, q_norm, k_norm, B, tag):
    qkv, fb, qb, kb, vb, o, l = saved
    T = qkv.shape[0]
    S = T // B
    Wd = len(DIL_PAIRS) * DIL_GW
    nblk = Wd // LANES
    gw = DIL_GW
    rows = [(o, gw, 0), (o, gw, 1), (o, gw, 2), (l, gw, 0), (l, gw, 1), (l, gw, 2)]
    g = _ew_bwd(_f_dil_merge, rows, [], [], [dy], [True] * 6, name=f"{tag}_merge_bwd")
    do = _dil_to_blocks(jnp.concatenate(g[:3], axis=1), B)
    dl = _dil_to_blocks(jnp.concatenate(g[3:], axis=1), B)
    dq, dkp, dkc, dvp, dvc = _dil_attn_bwd(qb, kb, vb, do, dl, B, S, f"{tag}_attn_bwd")
    nxt = lambda t: jnp.concatenate([t[DIL_BLK:], jnp.zeros((DIL_BLK, DIL_DIM), t.dtype)], axis=0)
    dqn = _dil_from_blocks(dq, B)
    dkn = _dil_from_blocks(dkc + nxt(dkp), B)
    dv = _dil_from_blocks(dvc + nxt(dvp), B)
    dq_raw, dqw = _ew_bwd(functools.partial(_f_qknorm, DIL_DIM ** -0.5), [(qkv, LANES, fb)], [], [q_norm],
                          [(dqn, LANES, 0)], [True, True], ncb=nblk, name=f"{tag}_qnorm_bwd")
    dk_raw, dkw = _ew_bwd(functools.partial(_f_qknorm, 1.0), [(qkv, LANES, fb + nblk)], [], [k_norm],
                          [(dkn, LANES, 0)], [True, True], ncb=nblk, name=f"{tag}_knorm_bwd")
    return jnp.concatenate([dq_raw, dk_raw, dv], axis=1), dqw, dkw


MESH_ID = pl.DeviceIdType.MESH
ANY = pl.BlockSpec(memory_space=pl.ANY)


def _my_place():
    return lax.axis_index("x"), lax.axis_index("y"), lax.axis_index("c")


def _flat_index(px, py, pc):
    return 4 * px + 2 * py + pc


def _all_gather(x, name):
    R_, C = x.shape

    def body(x_ref, out_ref, send_sems, recv_sems, local_sem):
        mx, my, mc = _my_place()
        me, sibling = (mx, my, mc), (mx, my, 1 - mc)
        chips = [(1 - mx, my), (mx, 1 - my), (1 - mx, 1 - my)]

        def rows(p):
            return out_ref.at[_flat_index(*p)]

        def copy(k, block, to, src=None):
            return pltpu.make_async_remote_copy(
                src_ref=rows(block) if src is None else src, dst_ref=rows(block),
                send_sem=send_sems.at[k], recv_sem=recv_sems.at[k], device_id=to, device_id_type=MESH_ID)

        mine = pltpu.make_async_copy(x_ref, rows(me), local_sem)
        mine.start()
        first = [copy(0, me, sibling, src=x_ref)]
        first += [copy(1 + j, me, (*chip, mc), src=x_ref) for j, chip in enumerate(chips)]
        for cp in first:
            cp.start()
        passed = [copy(4 + j, (*chip, mc), sibling) for j, chip in enumerate(chips)]
        for j, chip in enumerate(chips):
            copy(1 + j, (*chip, mc), me).wait_recv()
            passed[j].start()
        copy(0, sibling, me).wait_recv()
        for j, chip in enumerate(chips):
            copy(4 + j, (*chip, 1 - mc), me).wait_recv()
        for cp in first + passed:
            cp.wait_send()
        mine.wait()

    return pl.pallas_call(
        body, name=name, out_shape=jax.ShapeDtypeStruct((N_DEV, R_, C), x.dtype),
        in_specs=[ANY], out_specs=ANY,
        scratch_shapes=[pltpu.SemaphoreType.DMA((7,)), pltpu.SemaphoreType.DMA((7,)), pltpu.SemaphoreType.DMA],
        compiler_params=pltpu.CompilerParams(has_side_effects=True),
    )(x)


def _all_gather_halves(x, name):
    R_, C = x.shape
    hr = R_ // 2

    def body(x_ref, out_ref, send_sems, recv_sems, local_sem):
        mx, my, mc = _my_place()
        me, sibling = (mx, my, mc), (mx, my, 1 - mc)
        chips = [(1 - mx, my), (mx, 1 - my), (1 - mx, 1 - my)]

        def rows(p, h=None):
            blk = out_ref.at[_flat_index(*p)]
            return blk if h is None else blk.at[pl.ds(h * hr, hr)]

        def copy(k, block, to, h=None, src=None):
            return pltpu.make_async_remote_copy(
                src_ref=rows(block, h) if src is None else src, dst_ref=rows(block, h),
                send_sem=send_sems.at[k], recv_sem=recv_sems.at[k], device_id=to, device_id_type=MESH_ID)

        mine = pltpu.make_async_copy(x_ref, rows(me), local_sem)
        mine.start()
        first = [copy(0, me, sibling, src=x_ref)]
        first += [copy(1 + 2 * j + h, me, (*chip, mc), h, src=x_ref.at[pl.ds(h * hr, hr)])
                  for j, chip in enumerate(chips) for h in (0, 1)]
        for cp in first:
            cp.start()
        passed = [copy(7 + 2 * j + h, (*chip, mc), sibling, h) for j, chip in enumerate(chips) for h in (0, 1)]
        for j, chip in enumerate(chips):
            for h in (0, 1):
                copy(1 + 2 * j + h, (*chip, mc), me, h).wait_recv()
                passed[2 * j + h].start()
        copy(0, sibling, me).wait_recv()
        for j, chip in enumerate(chips):
            for h in (0, 1):
                copy(7 + 2 * j + h, (*chip, 1 - mc), me, h).wait_recv()
        for cp in first + passed:
            cp.wait_send()
        mine.wait()

    return pl.pallas_call(
        body, name=name, out_shape=jax.ShapeDtypeStruct((N_DEV, R_, C), x.dtype),
        in_specs=[ANY], out_specs=ANY,
        scratch_shapes=[pltpu.SemaphoreType.DMA((13,)), pltpu.SemaphoreType.DMA((13,)), pltpu.SemaphoreType.DMA],
        compiler_params=pltpu.CompilerParams(has_side_effects=True),
    )(x)


def _exchange_sibling(x, name):
    nchip, _, R_, C = x.shape

    def body(x_ref, out_ref, send_sems, recv_sems):
        mx, my, mc = _my_place()
        sibling = (mx, my, 1 - mc)
        copies = [pltpu.make_async_remote_copy(
            src_ref=x_ref.at[k, 1 - mc], dst_ref=out_ref.at[k], send_sem=send_sems.at[k], recv_sem=recv_sems.at[k],
            device_id=sibling, device_id_type=MESH_ID) for k in range(nchip)]
        for cp in copies:
            cp.start()
        for cp in copies:
            cp.wait_recv()
        for cp in copies:
            cp.wait_send()

    return pl.pallas_call(
        body, name=name, out_shape=jax.ShapeDtypeStruct((nchip, R_, C), x.dtype), in_specs=[ANY], out_specs=ANY,
        scratch_shapes=[pltpu.SemaphoreType.DMA((nchip,)), pltpu.SemaphoreType.DMA((nchip,))],
        compiler_params=pltpu.CompilerParams(has_side_effects=True),
    )(x)


def _exchange_chips(x, name):
    nchip, R_, C = x.shape

    def body(x_ref, out_ref, send_sems, recv_sems, local_sem):
        mx, my, mc = _my_place()
        mk = 2 * mx + my
        chips = [(1 - mx, my), (mx, 1 - my), (1 - mx, 1 - my)]
        mine = pltpu.make_async_copy(x_ref.at[mk], out_ref.at[mk], local_sem)
        mine.start()
        copies = [pltpu.make_async_remote_copy(
            src_ref=x_ref.at[2 * px + py], dst_ref=out_ref.at[mk], send_sem=send_sems.at[j], recv_sem=recv_sems.at[j],
            device_id=(px, py, mc), device_id_type=MESH_ID) for j, (px, py) in enumerate(chips)]
        for cp in copies:
            cp.start()
        for j, (px, py) in enumerate(chips):
            pltpu.make_async_remote_copy(
                src_ref=x_ref.at[mk], dst_ref=out_ref.at[2 * px + py], send_sem=send_sems.at[j],
                recv_sem=recv_sems.at[j], device_id=(px, py, mc), device_id_type=MESH_ID).wait_recv()
        for cp in copies:
            cp.wait_send()
        mine.wait()

    return pl.pallas_call(
        body, name=name, out_shape=jax.ShapeDtypeStruct(x.shape, x.dtype), in_specs=[ANY], out_specs=ANY,
        scratch_shapes=[pltpu.SemaphoreType.DMA((3,)), pltpu.SemaphoreType.DMA((3,)), pltpu.SemaphoreType.DMA],
        compiler_params=pltpu.CompilerParams(has_side_effects=True),
    )(x)


BLOCK_BYTES = 2 * 1024 * 1024


def _rows_for(R_, row_bytes):
    for t in (2048, 1024, 512, 256, 128, 64, 32, 16, 8):
        if R_ % t == 0 and t * row_bytes <= BLOCK_BYTES:
            return t
    return R_


def _add_pieces(a, b, name):
    n, R_, C = a.shape
    tr = _rows_for(R_, C * a.dtype.itemsize)

    def body(a_ref, b_ref, o_ref):
        o_ref[...] = (a_ref[...].astype(F32) + b_ref[...].astype(F32)).astype(o_ref.dtype)

    spec = pl.BlockSpec((1, tr, C), lambda k, i: (k, i, 0))
    return pl.pallas_call(
        body, name=name, out_shape=jax.ShapeDtypeStruct(a.shape, a.dtype), grid=(n, R_ // tr),
        in_specs=[spec, spec], out_specs=spec, compiler_params=_cparams(("parallel", "parallel")),
    )(a, b)


def _reduce_to_owner(buf, mc, tag):
    _, R_, C = buf.shape
    buf4 = buf.reshape(N_DEV // 2, 2, R_, C)
    from_sibling = _exchange_sibling(buf4, f"scatter_sib_{tag}")
    mine = lax.dynamic_index_in_dim(buf4, mc, axis=1, keepdims=False)
    chip_sums = _add_pieces(mine, from_sibling, f"add_sib_{tag}")
    return _sum0(_exchange_chips(chip_sums, f"scatter_chips_{tag}"), f"sum_grads_{tag}")


def _sum0(x, name):
    n, R_, C = x.shape
    tr = _rows_for(R_, n * C * x.dtype.itemsize)

    def body(x_ref, o_ref):
        acc = x_ref[0].astype(F32)
        for k in range(1, n):
            acc = acc + x_ref[k].astype(F32)
        o_ref[...] = acc

    return pl.pallas_call(
        body, name=name, out_shape=jax.ShapeDtypeStruct((R_, C), F32), grid=(R_ // tr,),
        in_specs=[pl.BlockSpec((n, tr, C), lambda i: (0, i, 0))], out_specs=pl.BlockSpec((tr, C), lambda i: (i, 0)),
        compiler_params=_cparams(("parallel",)),
    )(x)


PACK_ROWS = 256


def _pack(arrs, dtype, width):
    flat = jnp.concatenate([a.astype(dtype).reshape(-1) for a in arrs])
    quantum = width * PACK_ROWS
    pad = (-flat.shape[0]) % quantum
    if pad:
        flat = jnp.concatenate([flat, jnp.zeros((pad,), dtype)])
    return flat.reshape(-1, width)


def _unpack(flat, shapes):
    out, off = [], 0
    for s in shapes:
        n = int(np.prod(s))
        out.append(flat[..., off:off + n].reshape(flat.shape[:-1] + tuple(s)))
        off += n
    return out


def _ada_fwd(c_all, ada_w, bias, name):
    M, D = c_all.shape
    n = ada_w.shape[1]
    tn = _pick(n, (768, 512, 256, 128))

    def body(c_ref, w_ref, b_ref, o_ref):
        c_ = c_ref[...]
        a = (c_ * jax.nn.sigmoid(c_)).astype(BF16)
        o_ref[...] = jnp.dot(a, w_ref[...].astype(BF16), preferred_element_type=F32) + b_ref[...]

    return pl.pallas_call(
        body, name=name, out_shape=jax.ShapeDtypeStruct((M, n), F32), grid=(n // tn,),
        in_specs=[pl.BlockSpec((M, D), lambda j: (0, 0)), pl.BlockSpec((D, tn), lambda j: (0, j)),
                  pl.BlockSpec((1, tn), lambda j: (0, j))],
        out_specs=pl.BlockSpec((M, tn), lambda j: (0, j)), compiler_params=_cparams(("parallel",)),
    )(c_all, ada_w, bias)


def _ada_bwd(c_all, dmod, name):
    M, D = c_all.shape
    n = dmod.shape[1]
    tn = _pick(n, (768, 512, 256, 128))

    def body(c_ref, d_ref, o_ref):
        c_ = c_ref[...]
        a = (c_ * jax.nn.sigmoid(c_)).astype(BF16)
        o_ref[...] = lax.dot_general(a, d_ref[...].astype(BF16), (((0,), (0,)), ((), ())), preferred_element_type=F32)

    return pl.pallas_call(
        body, name=name, out_shape=jax.ShapeDtypeStruct((D, n), F32), grid=(n // tn,),
        in_specs=[pl.BlockSpec((M, D), lambda j: (0, 0)), pl.BlockSpec((M, tn), lambda j: (0, j))],
        out_specs=pl.BlockSpec((D, tn), lambda j: (0, j)), compiler_params=_cparams(("parallel",)),
    )(c_all, dmod)


def _loss_head(y, target, name, tr=256):
    T, D = y.shape

    def body(y_ref, t_ref, l_ref, d_ref):
        e = y_ref[...] - t_ref[...]
        d_ref[...] = e * (1.0 / D)
        part = jnp.sum(jnp.sum(e * e, axis=1, keepdims=True), axis=0, keepdims=True) * (0.5 / D)

        @pl.when(pl.program_id(0) == 0)
        def _():
            l_ref[...] = jnp.zeros_like(l_ref)

        l_ref[...] += jnp.broadcast_to(part, l_ref.shape)

    row = pl.BlockSpec((tr, D), lambda i: (i, 0))
    return pl.pallas_call(
        body, name=name, out_shape=[jax.ShapeDtypeStruct((1, LANES), F32), jax.ShapeDtypeStruct((T, D), F32)],
        grid=(T // tr,), in_specs=[row, row], out_specs=[pl.BlockSpec((1, LANES), lambda i: (0, 0)), row],
        compiler_params=_cparams(("arbitrary",)),
    )(y, target)


def _adamw(w, g, m, v, name):
    shape = w.shape
    C = shape[-1]
    R_ = int(np.prod(shape[:-1]))
    w2, g2, m2, v2 = [a.reshape(R_, C) for a in (w, g, m, v)]
    tr = _pick(R_, (256, 128, 64, 32, 16, 8)) if R_ > 8 else R_
    c1 = 1.0 / (1.0 - ADAM_B1 ** ADAM_STEP)
    c2 = 1.0 / (1.0 - ADAM_B2 ** ADAM_STEP)

    def body(w_ref, g_ref, m_ref, v_ref, d_ref, nm_ref, nv_ref):
        g_ = g_ref[...]
        nm = ADAM_B1 * m_ref[...] + (1.0 - ADAM_B1) * g_
        nv = ADAM_B2 * v_ref[...] + (1.0 - ADAM_B2) * (g_ * g_)
        d_ref[...] = -ADAM_LR * ((nm * c1) / (jnp.sqrt(nv * c2) + ADAM_EPS) + ADAM_WD * w_ref[...])
        nm_ref[...] = nm
        nv_ref[...] = nv

    spec = pl.BlockSpec((tr, C), lambda i: (i, 0))
    outs = pl.pallas_call(
        body, name=name, out_shape=[jax.ShapeDtypeStruct((R_, C), F32)] * 3, grid=(R_ // tr,),
        in_specs=[spec] * 4, out_specs=[spec] * 3, compiler_params=_cparams(("parallel",)),
    )(w2, g2, m2, v2)
    return [o.reshape(shape) for o in outs]


GDN_W = GDN_HEADS * GDN_DIM
IN_SPLITS = (3 * GDN_W, GDN_W, GDN_HEADS, GDN_HEADS, 768, 3 * 768, None)
IN_PAD_GATES = LANES - 2 * GDN_HEADS


def _f_merge(pa, pb, pc, ga, gb, gc):
    return jax.nn.sigmoid(ga) * pa + jax.nn.sigmoid(gb) * pb + jax.nn.sigmoid(gc) * pc


def _f_id_rmsmod(x, sh, sc, g):
    return x, _rms(x, g) * (1.0 + sc) + sh


def _in_layout(D):
    widths = [3 * D, 3 * GDN_W, GDN_W, LANES, 768, 3 * 768]
    offs = np.concatenate([[0], np.cumsum(widths)]).tolist()
    total = -(-offs[-1] // 768) * 768
    return offs, total


def _pad_w_in(w_in):
    D = w_in.shape[0]
    offs, total = _in_layout(D)
    cut = 4 * GDN_W + 2 * GDN_HEADS
    ng = w_in.shape[1] - 3 * D
    return jnp.concatenate([w_in[:, ng:], w_in[:, :cut], jnp.zeros((D, IN_PAD_GATES), w_in.dtype), w_in[:, cut:ng],
                            jnp.zeros((D, total - offs[-1]), w_in.dtype)], axis=1)


def _unpad_w_in(d):
    D = d.shape[0]
    offs, _ = _in_layout(D)
    cut = 4 * GDN_W + 2 * GDN_HEADS
    g0 = 3 * D
    return jnp.concatenate([d[:, g0:g0 + cut], d[:, g0 + cut + IN_PAD_GATES:offs[-1]], d[:, :g0]], axis=1)


def _layer_fwd(x0, mod, W, sm, B, tag):
    T, D = x0.shape
    S = T // B
    sh1, sc1, g1, sh2, sc2, g2, sh3, sc3, g3 = mod
    n1, nm, n3 = sm["norm_ffn1"][None], sm["norm_mix"][None], sm["norm_ffn2"][None]
    (h1,) = _ew_fwd(_f_rmsmod, [x0], [sh1, sc1], [n1], [(D, BF16, False)], seq=S, name=f"{tag}_norm1")
    y1, sv1 = _ffn_fwd(h1, W["ffn1_w1"], W["ffn1_w3"], W["ffn1_w2"], f"{tag}_ffn1")
    x1, h2 = _ew_fwd(functools.partial(_f_res_rmsmod, FFN_RES), [x0, y1], [g1, sh2, sc2], [nm],
                     [(D, F32, False), (D, BF16, False)], seq=S, name=f"{tag}_res1")
    P = _mm(h2, W["w_in"], name=f"{tag}_in")
    offs, _ = _in_layout(D)
    qkv_a, z, ba, u, qkv_c = [(P, off // LANES) for off in offs[1:6]]
    lane8 = lambda v: jnp.pad(v, (GDN_HEADS, LANES - 2 * GDN_HEADS))[None]
    gate_params = (lane8(sm["gdn_a_log"]), lane8(sm["gdn_dt_bias"]))
    out_norm = sm["gdn_out_norm"][None]
    y_a, sva = _gdn_fwd(qkv_a, z, ba, sm["gdn_conv"], gate_params, out_norm, B, f"{tag}_gdn")
    s5p = {k[3:]: v for k, v in sm.items() if k.startswith("s5_")}
    y_b, svb = _s5_fwd(u, s5p, W["s5_glu_w"], B, f"{tag}_s5")
    qn2, kn2 = jnp.tile(sm["dil_q_norm"], 2)[None], jnp.tile(sm["dil_k_norm"], 2)[None]
    y_c, svc = _dil_fwd(qkv_c, qn2, kn2, B, f"{tag}_dil")
    pa = _mm(y_a, W["w_branch_a"], name=f"{tag}_pa")
    pb = _mm(y_b, W["w_branch_b"], name=f"{tag}_pb")
    pc = _mm(y_c, W["w_branch_c"], name=f"{tag}_pc")
    mrows = [pa, pb, pc, (P, D, 0), (P, D, 1), (P, D, 2)]
    (merged,) = _ew_fwd(_f_merge, mrows, [], [], [(D, BF16, False)], tr=128, name=f"{tag}_merge")
    mo = _mm(merged, W["w_out"], name=f"{tag}_out")
    x2, h3 = _ew_fwd(functools.partial(_f_res_rmsmod, 1.0), [x1, mo], [g2, sh3, sc3], [n3],
                     [(D, F32, False), (D, BF16, False)], seq=S, name=f"{tag}_res2")
    y3, sv3 = _ffn_fwd(h3, W["ffn2_w1"], W["ffn2_w3"], W["ffn2_w2"], f"{tag}_ffn2")
    (x3,) = _ew_fwd(functools.partial(_f_res, FFN_RES), [x2, y3], [g3], [], [(D, F32, False)], seq=S,
                    name=f"{tag}_res3")
    saved = dict(x0=x0, x1=x1, x2=x2, y1=y1, y3=y3, mo=mo, h2=h2, sv1=sv1, sv3=sv3, sva=sva, svb=svb, svc=svc,
                 y_a=y_a, y_b=y_b, y_c=y_c, mrows=mrows, merged=merged, gate_params=gate_params, out_norm=out_norm,
                 s5p=s5p, qn2=qn2, kn2=kn2)
    return x3, saved


def _layer_bwd(dx3, sv, mod, W, sm, B, tag):
    T, D = dx3.shape
    S = T // B
    sh1, sc1, g1, sh2, sc2, g2, sh3, sc3, g3 = mod
    n1, nm, n3 = sm["norm_ffn1"][None], sm["norm_mix"][None], sm["norm_ffn2"][None]
    big, small = {}, {}
    dx2, dy3, dg3 = _ew_bwd(functools.partial(_f_res, FFN_RES), [sv["x2"], sv["y3"]], [g3], [], [dx3], [True] * 3,
                            seq=S, row_dtypes=[F32, BF16], name=f"{tag}_res3_bwd")
    dh3, big["ffn2_w1"], big["ffn2_w3"], big["ffn2_w2"] = _ffn_bwd(
        dy3, sv["sv3"], W["ffn2_w1"], W["ffn2_w3"], W["ffn2_w2"], f"{tag}_ffn2")
    dx1, dmo, dg2, dsh3, dsc3, dn3 = _ew_bwd(
        functools.partial(_f_res_rmsmod, 1.0), [sv["x1"], sv["mo"]], [g2, sh3, sc3], [n3], [dx2, dh3], [True] * 6,
        seq=S, row_dtypes=[F32, BF16], name=f"{tag}_res2_bwd")
    big["w_out"] = _mm(sv["merged"], dmo, ta=True, out_dtype=BF16, name=f"{tag}_out_dw")
    dmerged = _mm(dmo, W["w_out"], tb=True, name=f"{tag}_out_dx")
    dpa, dpb, dpc, dga, dgb, dgc = _ew_bwd(_f_merge, sv["mrows"], [], [], [dmerged], [True] * 6, tr=128,
                                           row_dtypes=[BF16] * 6, name=f"{tag}_merge_bwd")
    big["w_branch_a"] = _mm(sv["y_a"], dpa, ta=True, out_dtype=BF16, name=f"{tag}_pa_dw")
    big["w_branch_b"] = _mm(sv["y_b"], dpb, ta=True, out_dtype=BF16, name=f"{tag}_pb_dw")
    big["w_branch_c"] = _mm(sv["y_c"], dpc, ta=True, out_dtype=BF16, name=f"{tag}_pc_dw")
    dy_a = _mm(dpa, W["w_branch_a"], tb=True, name=f"{tag}_pa_dx")
    dy_b = _mm(dpb, W["w_branch_b"], tb=True, name=f"{tag}_pb_dx")
    dy_c = _mm(dpc, W["w_branch_c"], tb=True, name=f"{tag}_pc_dx")
    dqkv_a, dz, dba, gdn_small = _gdn_bwd(dy_a, sv["sva"], sm["gdn_conv"], sv["gate_params"], sv["out_norm"], B,
                                          f"{tag}_gdn")
    du, big["s5_glu_w"], s5_small = _s5_bwd(dy_b, sv["svb"], sv["s5p"], W["s5_glu_w"], B, f"{tag}_s5")
    dqkv_c, dqw, dkw = _dil_bwd(dy_c, sv["svc"], sv["qn2"], sv["kn2"], B, f"{tag}_dil")
    offs, total = _in_layout(D)
    dP = jnp.concatenate([t.astype(BF16) for t in (dga, dgb, dgc, dqkv_a, dz, dba, du, dqkv_c)]
                         + [jnp.zeros((T, total - offs[-1]), BF16)], axis=1)
    big["w_in"] = _mm(sv["h2"], dP, ta=True, out_dtype=BF16, name=f"{tag}_in_dw")
    dh2 = _mm(dP, W["w_in"], tb=True, name=f"{tag}_in_dx")
    dx0a, dy1, dg1, dsh2, dsc2, dnm = _ew_bwd(
        functools.partial(_f_res_rmsmod, FFN_RES), [sv["x0"], sv["y1"]], [g1, sh2, sc2], [nm], [dx1, dh2], [True] * 6,
        seq=S, row_dtypes=[F32, BF16], name=f"{tag}_res1_bwd")
    dh1, big["ffn1_w1"], big["ffn1_w3"], big["ffn1_w2"] = _ffn_bwd(
        dy1, sv["sv1"], W["ffn1_w1"], W["ffn1_w3"], W["ffn1_w2"], f"{tag}_ffn1")
    dx0, dsh1, dsc1, dn1 = _ew_bwd(_f_id_rmsmod, [sv["x0"]], [sh1, sc1], [n1], [dx0a, dh1], [True] * 4, seq=S,
                                   name=f"{tag}_norm1_bwd")
    half = DIL_DIM
    small.update(norm_ffn1=dn1[0], norm_mix=dnm[0], norm_ffn2=dn3[0], gdn_conv=gdn_small["conv"],
                 gdn_a_log=gdn_small["a_log"][0, GDN_HEADS:2 * GDN_HEADS],
                 gdn_dt_bias=gdn_small["dt_bias"][0, GDN_HEADS:2 * GDN_HEADS], gdn_out_norm=gdn_small["out_norm"][0],
                 dil_q_norm=dqw[0, :half] + dqw[0, half:], dil_k_norm=dkw[0, :half] + dkw[0, half:])
    small.update({"s5_" + k: v for k, v in s5_small.items()})
    dmod = [dsh1, dsc1, dg1, dsh2, dsc2, dg2, dsh3, dsc3, dg3]
    return dx0, big, small, dmod


WEIGHTS = ['ada_w', 'ada_b', 'norm_ffn1', 'ffn1_w1', 'ffn1_w3', 'ffn1_w2', 'norm_mix', 'w_in', 'gdn_conv', 'gdn_a_log',
           'gdn_dt_bias', 'gdn_out_norm', 's5_a_re', 's5_a_im', 's5_b_re', 's5_b_im', 's5_c_re', 's5_c_im', 's5_d',
           's5_log_step', 's5_glu_w', 's5_glu_b', 'dil_q_norm', 'dil_k_norm', 'w_branch_a', 'w_branch_b', 'w_branch_c',
           'w_out', 'norm_ffn2', 'ffn2_w1', 'ffn2_w3', 'ffn2_w2']
BIG = dict(ffn1_w1=True, ffn1_w3=True, ffn1_w2=False, w_in=True, s5_glu_w=True, w_branch_a=True, w_branch_b=True,
           w_branch_c=True, w_out=False, ffn2_w1=True, ffn2_w3=True, ffn2_w2=False)
FFN_W = ("ffn1_w1", "ffn1_w3", "ffn1_w2", "ffn2_w1", "ffn2_w3", "ffn2_w2")
SMALL = ['norm_ffn1', 'norm_mix', 'norm_ffn2', 'gdn_conv', 'gdn_a_log', 'gdn_dt_bias', 'gdn_out_norm', 's5_a_re',
         's5_a_im', 's5_b_re', 's5_b_im', 's5_c_re', 's5_c_im', 's5_d', 's5_log_step', 's5_glu_b', 'dil_q_norm',
         'dil_k_norm']


def _full_from_shards(g, cols):
    n, r, c = g.shape
    return jnp.transpose(g, (1, 0, 2)).reshape(r, n * c) if cols else g.reshape(n * r, c)


def _shards_from_full(w, cols):
    if cols:
        r, nc = w.shape
        return jnp.transpose(w.reshape(r, N_DEV, nc // N_DEV), (1, 0, 2))
    nr, c = w.shape
    return w.reshape(N_DEV, nr // N_DEV, c)


def kernel(x, c, ada_w, ada_b, norm_ffn1, ffn1_w1, ffn1_w3, ffn1_w2, norm_mix, w_in, gdn_conv, gdn_a_log, gdn_dt_bias, gdn_out_norm, s5_a_re, s5_a_im, s5_b_re, s5_b_im, s5_c_re, s5_c_im, s5_d, s5_log_step, s5_glu_w, s5_glu_b, dil_q_norm, dil_k_norm, w_branch_a, w_branch_b, w_branch_c, w_out, norm_ffn2, ffn2_w1, ffn2_w3, ffn2_w2, loss_target, m_ada_w, m_ada_b, m_norm_ffn1, m_ffn1_w1, m_ffn1_w3, m_ffn1_w2, m_norm_mix, m_w_in, m_gdn_conv, m_gdn_a_log, m_gdn_dt_bias, m_gdn_out_norm, m_s5_a_re, m_s5_a_im, m_s5_b_re, m_s5_b_im, m_s5_c_re, m_s5_c_im, m_s5_d, m_s5_log_step, m_s5_glu_w, m_s5_glu_b, m_dil_q_norm, m_dil_k_norm, m_w_branch_a, m_w_branch_b, m_w_branch_c, m_w_out, m_norm_ffn2, m_ffn2_w1, m_ffn2_w3, m_ffn2_w2, v_ada_w, v_ada_b, v_norm_ffn1, v_ffn1_w1, v_ffn1_w3, v_ffn1_w2, v_norm_mix, v_w_in, v_gdn_conv, v_gdn_a_log, v_gdn_dt_bias, v_gdn_out_norm, v_s5_a_re, v_s5_a_im, v_s5_b_re, v_s5_b_im, v_s5_c_re, v_s5_c_im, v_s5_d, v_s5_log_step, v_s5_glu_w, v_s5_glu_b, v_dil_q_norm, v_dil_k_norm, v_w_branch_a, v_w_branch_b, v_w_branch_c, v_w_out, v_norm_ffn2, v_ffn2_w1, v_ffn2_w3, v_ffn2_w2):
    env = dict(locals())
    w = {n: env[n] for n in WEIGHTS}
    m = {n: env["m_" + n] for n in WEIGHTS}
    v = {n: env["v_" + n] for n in WEIGHTS}
    L = ada_w.shape[0]
    B, S, D = x.shape
    T = B * S
    me = _flat_index(*_my_place())

    big_keys = [(n, l) for l in range(L) for n in BIG]
    groups = {}
    for n, l in big_keys:
        r, cc = w[n].shape[1:]
        groups.setdefault((BIG[n], r if BIG[n] else cc), []).append((n, l))
    shards = {}
    for (cols, dim), keys in groups.items():
        if cols and dim == D:
            for n, l in keys:
                shards[(n, l)] = _all_gather_halves(w[n][l].astype(BF16), f"gather_{n}_{l}")
            continue
        buf = jnp.concatenate([w[n][l].astype(BF16) for n, l in keys], axis=1 if cols else 0)
        got = _all_gather(buf, f"gather_weights_{'c' if cols else 'r'}{dim}")
        off = 0
        for n, l in keys:
            k = w[n].shape[2] if cols else w[n].shape[1]
            shards[(n, l)] = got[:, :, off:off + k] if cols else got[:, off:off + k, :]
            off += k
    small_in = _pack([jnp.pad(c, ((0, SUBLANES - B), (0, 0))), gdn_conv], F32, LANES)
    c_g, conv_g = _unpack(_all_gather(small_in, "gather_cond").reshape(N_DEV, -1),
                          [(SUBLANES, D), gdn_conv.shape])
    c_all = c_g[:, :B].reshape(N_DEV * B, D)
    conv_full = jnp.transpose(conv_g, (1, 2, 0, 3)).reshape(L, GDN_CONV, -1)
    Ws = []
    for l in range(L):
        Wl = {n: shards[(n, l)] for n in FFN_W}
        Wl.update({n: _full_from_shards(shards[(n, l)], BIG[n])
                   for n in ("s5_glu_w", "w_branch_a", "w_branch_b", "w_branch_c", "w_out")})
        Wl["w_in"] = _pad_w_in(_full_from_shards(shards[("w_in", l)], True))
        Ws.append(Wl)
    sms = [dict({n: w[n][l] for n in SMALL}, gdn_conv=conv_full[l]) for l in range(L)]

    n_ada = ada_w.shape[2]
    bias = lax.dynamic_slice(ada_b, (0, me * n_ada), (L, n_ada))
    mod_cols = jnp.concatenate([_ada_fwd(c_all, ada_w[l], bias[l][None], f"ada{l}") for l in range(L)], axis=0)
    mod_g = _all_gather(mod_cols, "gather_mod").reshape(N_DEV, L, N_DEV * B, n_ada)
    mod_mine = lax.dynamic_slice(mod_g, (0, 0, me * B, 0), (N_DEV, L, B, n_ada))
    mod_mine = jnp.transpose(mod_mine, (1, 2, 0, 3)).reshape(L, B, N_DEV * n_ada)
    mods = [[mod_mine[l][:, None, k * D:(k + 1) * D] for k in range(9)] for l in range(L)]

    h = x.reshape(T, D)
    saved = []
    for l in range(L):
        h, sv = _layer_fwd(h, mods[l], Ws[l], sms[l], B, f"l{l}")
        saved.append(sv)
    loss_row, dh = _loss_head(h, loss_target.reshape(T, D), "loss")
    loss = lax.psum(loss_row[0, 0], ("x", "y", "c"))
    bigs, smalls, dmods = [None] * L, [None] * L, [None] * L
    for l in reversed(range(L)):
        dh, bigs[l], smalls[l], dmods[l] = _layer_bwd(dh, saved[l], mods[l], Ws[l], sms[l], B, f"l{l}")
    grad_x = dh.reshape(B, S, D)

    def grad_pieces(n, l):
        if n in FFN_W:
            return bigs[l][n]
        full = _unpad_w_in(bigs[l]["w_in"]) if n == "w_in" else bigs[l][n]
        return _shards_from_full(full, BIG[n])

    g = dict()
    mc = lax.axis_index("c")
    for (cols, dim), keys in groups.items():
        tag = f"{'c' if cols else 'r'}{dim}"
        if cols and dim == D:
            for n, l in keys:
                g.setdefault(n, [None] * L)[l] = _reduce_to_owner(grad_pieces(n, l), mc, f"{n}_{l}")
            continue
        buf = jnp.concatenate([grad_pieces(n, l) for n, l in keys], axis=2 if cols else 1)
        summed = _reduce_to_owner(buf, mc, tag)
        off = 0
        for n, l in keys:
            k = w[n].shape[2] if cols else w[n].shape[1]
            g.setdefault(n, [None] * L)[l] = summed[:, off:off + k] if cols else summed[off:off + k, :]
            off += k
    g = {n: jnp.stack(ts) for n, ts in g.items()}

    small_keys = [(n, l) for l in range(L) for n in SMALL]
    small_flat = _pack([smalls[l][n] for n, l in small_keys], F32, LANES)
    small_sum = _sum0(_all_gather(small_flat, "gather_small_grads"), "sum_small_grads")
    small_full = {}
    for (n, l), t in zip(small_keys, _unpack(small_sum.reshape(-1), [smalls[l][n].shape for n, l in small_keys])):
        small_full.setdefault(n, [None] * L)[l] = t
    for n, ts in small_full.items():
        g[n] = jnp.stack(ts)
    n_conv = gdn_conv.shape[2]
    g["gdn_conv"] = lax.dynamic_slice(g["gdn_conv"], (0, 0, me * n_conv), (L, GDN_CONV, n_conv))

    dmod_mine = jnp.stack([jnp.concatenate([d[:, 0] for d in dmods[l]], axis=1) for l in range(L)])
    dmod_in = jnp.pad(dmod_mine.reshape(L * B, -1), ((0, SUBLANES - L * B), (0, 0)))
    dmod_g = _all_gather(dmod_in, "gather_dmod")[:, :L * B].reshape(N_DEV, L, B, -1)
    dmod_all = jnp.transpose(dmod_g, (1, 0, 2, 3)).reshape(L, N_DEV * B, -1)
    g["ada_b"] = _sum0(dmod_all.reshape(L, N_DEV * B, -1, LANES).transpose(1, 0, 2, 3).reshape(N_DEV * B, -1, LANES),
                       "sum_ada_b").reshape(L, -1)
    dmod_cols = lax.dynamic_slice(dmod_all, (0, 0, me * n_ada), (L, N_DEV * B, n_ada))
    g["ada_w"] = jnp.stack([_ada_bwd(c_all, dmod_cols[l], f"ada{l}_bwd") for l in range(L)])

    upd = {n: _adamw(w[n], g[n], m[n], v[n], f"adamw_{n}") for n in WEIGHTS}
    return (loss, grad_x, *[g[n] for n in WEIGHTS], *[upd[n][0] for n in WEIGHTS],
            *[upd[n][1] for n in WEIGHTS], *[upd[n][2] for n in WEIGHTS])
```
